```python
import jax, jax.numpy as jnp
from jax import lax
import numpy as np

D_MODEL = 1024
BATCH = 2
SEQ = 16384
DEPTH = 4

GRID_W = 64
CTX_LEN = 256
N_MIXERS = 2
POOL_WINDOWS = (2, 4, 8, 16)
N_POOL_GROUPS = 4
POOL_GROUP_DIM = D_MODEL // N_POOL_GROUPS
HEAD_DIM = 64
N_Q_HEADS = D_MODEL // HEAD_DIM
N_KV_HEADS = N_Q_HEADS // 4
Q_PER_KV = N_Q_HEADS // N_KV_HEADS
QKV_DIM = (N_Q_HEADS + 2 * N_KV_HEADS) * HEAD_DIM
WINDOW = 128
BLOCK_Q = 128
ROPE_THETA = 10000.0
N_EXPERTS = 16
EC_CAPACITY_FACTOR = 2
D_EXPERT = 2048
NORM_EPS = 1e-6
NEG_INF = -1e30

kernel_name = "hybrid_pool_swa_ecmoe_diffusion"


def rmsnorm(x, g):
    xf = x.astype(jnp.float32)
    y = xf * lax.rsqrt(jnp.mean(xf * xf, axis=-1, keepdims=True) + NORM_EPS)
    return (y * g.astype(jnp.float32)).astype(x.dtype)


def ada_params(cvec, w, b):
    m = jax.nn.silu(cvec) @ w + b
    return jnp.split(m, 6, axis=-1)


def modulate(h, shift, scale):
    return h * (1 + scale) + shift


def rope_2d(x):
    B, L, H, Dh = x.shape
    rows = L // GRID_W
    row = jnp.repeat(jnp.arange(rows), GRID_W).astype(jnp.float32)
    col = jnp.tile(jnp.arange(GRID_W), rows).astype(jnp.float32)
    n_freq = Dh // 4
    inv = ROPE_THETA ** (-jnp.arange(n_freq, dtype=jnp.float32) / n_freq)
    ang = jnp.concatenate([row[:, None] * inv, col[:, None] * inv], axis=-1)
    cos = jnp.cos(ang)[None, :, None, :]
    sin = jnp.sin(ang)[None, :, None, :]
    xp = x.astype(jnp.float32).reshape(B, L, H, Dh // 2, 2)
    x0, x1 = xp[..., 0], xp[..., 1]
    out = jnp.stack([x0 * cos - x1 * sin, x0 * sin + x1 * cos], axis=-1)
    return out.reshape(B, L, H, Dh).astype(x.dtype)


def pool_mix(h, w_pool, scale):
    B, L, D = h.shape
    hf = h.astype(jnp.float32)
    cs = jnp.concatenate([jnp.zeros((B, 1, D), jnp.float32), jnp.cumsum(hf, axis=1)], axis=1)
    t = jnp.arange(L)
    groups = []
    for g, w in enumerate(POOL_WINDOWS):
        half = w // 2
        lo = jnp.clip(t - half, 0, L)
        hi = jnp.clip(t + half, 0, L)
        sl = slice(g * POOL_GROUP_DIM, (g + 1) * POOL_GROUP_DIM)
        csg = cs[..., sl]
        mean = (csg[:, hi] - csg[:, lo]) / (hi - lo).astype(jnp.float32)[None, :, None]
        groups.append(mean - hf[..., sl])
    p = jnp.stack(groups, axis=2)
    y = jnp.einsum('blgc,gcd->blgd', p, w_pool.astype(jnp.float32)).reshape(B, L, D)
    return (y * scale.astype(jnp.float32)).astype(h.dtype)


def project_qkv(h, w_qkv, q_norm, k_norm):
    B, L, _ = h.shape
    proj = h @ w_qkv
    nq = N_Q_HEADS * HEAD_DIM
    nk = N_KV_HEADS * HEAD_DIM
    q = rmsnorm(proj[..., :nq].reshape(B, L, N_Q_HEADS, HEAD_DIM), q_norm)
    k = rmsnorm(proj[..., nq:nq + nk].reshape(B, L, N_KV_HEADS, HEAD_DIM), k_norm)
    v = proj[..., nq + nk:].reshape(B, L, N_KV_HEADS, HEAD_DIM)
    return q, k, v


def project_kv(h, w_qkv, k_norm):
    B, L, _ = h.shape
    nq = N_Q_HEADS * HEAD_DIM
    nk = N_KV_HEADS * HEAD_DIM
    proj = h @ w_qkv[:, nq:]
    k = rmsnorm(proj[..., :nk].reshape(B, L, N_KV_HEADS, HEAD_DIM), k_norm)
    v = proj[..., nk:].reshape(B, L, N_KV_HEADS, HEAD_DIM)
    return k, v


def attn_latent(q, k, v, kc, vc, sink):
    B, L, _, _ = q.shape
    Lc = kc.shape[1]
    nb = L // BLOCK_Q
    band = BLOCK_Q + 2 * WINDOW
    scale = HEAD_DIM ** -0.5
    kpad = jnp.pad(k, ((0, 0), (WINDOW, WINDOW), (0, 0), (0, 0)))
    vpad = jnp.pad(v, ((0, 0), (WINDOW, WINDOW), (0, 0), (0, 0)))
    sink_b = jnp.broadcast_to(sink.astype(jnp.float32).reshape(N_KV_HEADS, Q_PER_KV)[None, :, :, None, None],
                              (B, N_KV_HEADS, Q_PER_KV, BLOCK_Q, 1))

    def block(bi):
        start = bi * BLOCK_Q
        qb = lax.dynamic_slice_in_dim(q, start, BLOCK_Q, axis=1).reshape(B, BLOCK_Q, N_KV_HEADS, Q_PER_KV, HEAD_DIM)
        kb = lax.dynamic_slice_in_dim(kpad, start, band, axis=1)
        vb = lax.dynamic_slice_in_dim(vpad, start, band, axis=1)
        qi = start + jnp.arange(BLOCK_Q)
        kj = start - WINDOW + jnp.arange(band)
        mask = (kj[None, :] >= 0) & (kj[None, :] < L) & (jnp.abs(qi[:, None] - kj[None, :]) <= WINDOW)
        s_loc = jnp.einsum('bqkgd,bskd->bkgqs', qb, kb).astype(jnp.float32) * scale
        s_loc = jnp.where(mask[None, None, None], s_loc, NEG_INF)
        s_ctx = jnp.einsum('bqkgd,bskd->bkgqs', qb, kc).astype(jnp.float32) * scale
        p = jax.nn.softmax(jnp.concatenate([s_loc, s_ctx, sink_b], axis=-1), axis=-1).astype(v.dtype)
        o = (jnp.einsum('bkgqs,bskd->bqkgd', p[..., :band], vb)
             + jnp.einsum('bkgqs,bskd->bqkgd', p[..., band:band + Lc], vc))
        return o.reshape(B, BLOCK_Q, N_Q_HEADS * HEAD_DIM)

    out = lax.map(block, jnp.arange(nb))
    return jnp.transpose(out, (1, 0, 2, 3)).reshape(B, L, N_Q_HEADS * HEAD_DIM)


def attn_context(qc, kc, vc, sink):
    B, Lc, _, _ = qc.shape
    scale = HEAD_DIM ** -0.5
    qg = qc.reshape(B, Lc, N_KV_HEADS, Q_PER_KV, HEAD_DIM)
    s = jnp.einsum('bqkgd,bskd->bkgqs', qg, kc).astype(jnp.float32) * scale
    sink_b = jnp.broadcast_to(sink.astype(jnp.float32).reshape(N_KV_HEADS, Q_PER_KV)[None, :, :, None, None],
                              (B, N_KV_HEADS, Q_PER_KV, Lc, 1))
    p = jax.nn.softmax(jnp.concatenate([s, sink_b], axis=-1), axis=-1).astype(vc.dtype)
    o = jnp.einsum('bkgqs,bskd->bqkgd', p[..., :Lc], vc)
    return o.reshape(B, Lc, N_Q_HEADS * HEAD_DIM)


def ec_moe(h, w_router, w_gate, w_up, w_down):
    B, N, D = h.shape
    C = EC_CAPACITY_FACTOR * N // N_EXPERTS
    aff = jax.nn.softmax(jnp.einsum('bnd,de->bne', h, w_router).astype(jnp.float32), axis=-1)
    vals, idx = lax.top_k(jnp.swapaxes(aff, 1, 2), C)
    xs = jax.vmap(lambda hb, ib: hb[ib])(h, idx)
    a = jnp.einsum('becd,edf->becf', xs, w_gate)
    u = jnp.einsum('becd,edf->becf', xs, w_up)
    y = jnp.einsum('becf,efd->becd', jax.nn.silu(a) * u, w_down) * vals[..., None].astype(h.dtype)
    out = jax.vmap(lambda ib, yb: jnp.zeros((N, D), yb.dtype).at[ib.reshape(-1)].add(yb.reshape(-1, D)))(idx, y)
    return out


def setup_inputs(seed: int = 0) -> dict:
    key = jax.random.key(seed)
    ks = jax.random.split(key, 20)
    n_pool = (DEPTH + 1) // 2
    n_attn = DEPTH // 2
    D = D_MODEL
    nrm = jax.random.normal
    f32 = jnp.float32
    return {
        "x": nrm(ks[0], (BATCH, SEQ, D), f32),
        "c": nrm(ks[1], (BATCH, D), f32),
        "ctx": nrm(ks[2], (BATCH, CTX_LEN, D), f32),
        "c_ctx": nrm(ks[3], (D,), f32),
        "w_ada": nrm(ks[4], (DEPTH, D, 6 * D), f32) * (0.5 * D ** -0.5),
        "b_ada": nrm(ks[5], (DEPTH, 6 * D), f32) * 0.02,
        "norm1_g": 1.0 + 0.05 * nrm(ks[6], (DEPTH, D), f32),
        "norm2_g": 1.0 + 0.05 * nrm(ks[7], (DEPTH, D), f32),
        "pool_w": nrm(ks[8], (n_pool, N_POOL_GROUPS, POOL_GROUP_DIM, POOL_GROUP_DIM), f32) * POOL_GROUP_DIM ** -0.5,
        "pool_scale": 1.0 + 0.05 * nrm(ks[9], (n_pool, D), f32),
        "attn_w_qkv": nrm(ks[10], (n_attn, D, QKV_DIM), f32) * D ** -0.5,
        "attn_w_o": nrm(ks[11], (n_attn, N_Q_HEADS * HEAD_DIM, D), f32) * (N_Q_HEADS * HEAD_DIM) ** -0.5,
        "attn_q_norm": 1.0 + 0.05 * nrm(ks[12], (n_attn, HEAD_DIM), f32),
        "attn_k_norm": 1.0 + 0.05 * nrm(ks[13], (n_attn, HEAD_DIM), f32),
        "attn_sink": 0.5 * nrm(ks[14], (n_attn, N_Q_HEADS), f32),
        "router_w": nrm(ks[15], (DEPTH, D, N_EXPERTS), f32) * D ** -0.5,
        "exp_w_gate": nrm(ks[16], (DEPTH, N_EXPERTS, D, D_EXPERT), f32) * D ** -0.5,
        "exp_w_up": nrm(ks[17], (DEPTH, N_EXPERTS, D, D_EXPERT), f32) * D ** -0.5,
        "exp_w_down": nrm(ks[18], (DEPTH, N_EXPERTS, D_EXPERT, D), f32) * D_EXPERT ** -0.5,
    }


def reference(x, c, ctx, c_ctx, w_ada, b_ada, norm1_g, norm2_g, pool_w, pool_scale,
              attn_w_qkv, attn_w_o, attn_q_norm, attn_k_norm, attn_sink,
              router_w, exp_w_gate, exp_w_up, exp_w_down):
    for i in range(DEPTH):
        update_ctx = i < DEPTH - 1
        j = i // N_MIXERS
        is_pool = (i % N_MIXERS) == 0
        sh1, sc1, g1, sh2, sc2, g2 = [m[:, None, :] for m in ada_params(c, w_ada[i], b_ada[i])]
        csh1, csc1, cg1, csh2, csc2, cg2 = [m[None, None, :] for m in ada_params(c_ctx, w_ada[i], b_ada[i])]

        h = modulate(rmsnorm(x, norm1_g[i]), sh1, sc1)
        if is_pool:
            y = pool_mix(h, pool_w[j], pool_scale[j])
            if update_ctx:
                hc = modulate(rmsnorm(ctx, norm1_g[i]), csh1, csc1)
                yc = pool_mix(hc, pool_w[j], pool_scale[j])
        else:
            hc = modulate(rmsnorm(ctx, norm1_g[i]), csh1, csc1)
            q, k, v = project_qkv(h, attn_w_qkv[j], attn_q_norm[j], attn_k_norm[j])
            q = rope_2d(q)
            k = rope_2d(k)
            if update_ctx:
                qc, kc, vc = project_qkv(hc, attn_w_qkv[j], attn_q_norm[j], attn_k_norm[j])
                yc = attn_context(qc, kc, vc, attn_sink[j]) @ attn_w_o[j]
            else:
                kc, vc = project_kv(hc, attn_w_qkv[j], attn_k_norm[j])
            y = attn_latent(q, k, v, kc, vc, attn_sink[j]) @ attn_w_o[j]

        x = x + g1 * y
        h2 = modulate(rmsnorm(x, norm2_g[i]), sh2, sc2)
        x = x + g2 * ec_moe(h2, router_w[i], exp_w_gate[i], exp_w_up[i], exp_w_down[i])

        if update_ctx:
            ctx = ctx + cg1 * yc
            hc2 = modulate(rmsnorm(ctx, norm2_g[i]), csh2, csc2)
            ctx = ctx + cg2 * ec_moe(hc2, router_w[i], exp_w_gate[i], exp_w_up[i], exp_w_down[i])
    return x
```

```python
import functools

import jax
import jax.numpy as jnp
from jax import lax
from jax.experimental import pallas as pl
from jax.experimental.pallas import tpu as pltpu

F32 = jnp.float32
BF16 = jnp.bfloat16
I32 = jnp.int32

D_MODEL = 1024
DEPTH = 4
GRID_W = 64
POOL_WINDOWS = (2, 4, 8, 16)
POOL_GROUP_DIM = 256
HEAD_DIM = 64
N_Q_HEADS = 16
N_KV_HEADS = 4
Q_PER_KV = 4
WINDOW = 128
BLOCK_Q = 128
ROPE_THETA = 10000.0
N_EXPERTS = 16
D_EXPERT = 2048
NORM_EPS = 1e-6
NEG_INF = -1e30

LANES = 128
BF16_ROWS = 16
HALO = 8
TOK_TILE = 256
VMEM_LIMIT = 56 * 1024 * 1024

NT_DIMS = (((1,), (1,)), ((), ()))


def _cparams(sem, vmem=None):
    return pltpu.CompilerParams(dimension_semantics=sem, vmem_limit_bytes=vmem or VMEM_LIMIT)


def _norm_mod(x, g, shift, scale):
    ms = jnp.mean(x * x, axis=-1, keepdims=True)
    return (x * lax.rsqrt(ms + NORM_EPS)) * g * (1.0 + scale) + shift


def _ada_kernel(cc_ref, w_ref, b_ref, o_ref):
    cc = cc_ref[...]
    s = cc * jax.nn.sigmoid(cc)
    o_ref[0] = jnp.dot(s, w_ref[0], preferred_element_type=F32,
                       precision=lax.Precision.HIGHEST) + b_ref[0]


def _ada(cc, w_ada, b_ada):
    depth, d, six_d = w_ada.shape
    tn = 1536
    return pl.pallas_call(
        _ada_kernel,
        grid=(depth, six_d // tn),
        in_specs=[
            pl.BlockSpec((8, d), lambda i, j: (0, 0)),
            pl.BlockSpec((1, d, tn), lambda i, j: (i, 0, j)),
            pl.BlockSpec((1, 1, tn), lambda i, j: (i, 0, j)),
        ],
        out_specs=pl.BlockSpec((1, 8, tn), lambda i, j: (i, 0, j)),
        out_shape=jax.ShapeDtypeStruct((depth, 8, six_d), F32),
        compiler_params=_cparams(("arbitrary", "arbitrary")),
        name="ada",
    )(cc, w_ada, b_ada.reshape(depth, 1, six_d))


def _route(x1, n2g, sh2, sc2, wr_hi, wr_lo):
    h2 = _norm_mod(x1, n2g, sh2, sc2)
    h_hi = h2.astype(BF16)
    h_lo = (h2 - h_hi.astype(F32)).astype(BF16)
    lg = (lax.dot_general(wr_hi, h_hi, NT_DIMS, preferred_element_type=F32)
          + lax.dot_general(wr_hi, h_lo, NT_DIMS, preferred_element_type=F32)
          + lax.dot_general(wr_lo, h_hi, NT_DIMS, preferred_element_type=F32))
    m = jnp.max(lg, axis=0, keepdims=True)
    ex = jnp.exp(lg - m)
    aff = ex / jnp.sum(ex, axis=0, keepdims=True)
    return h_hi, aff


def _pool_kernel(x_ref, xp_ref, xn_ref, n1g_ref, sh1_ref, sc1_ref, g1_ref, pw_ref, ps_ref,
                 n2g_ref, sh2_ref, sc2_ref, wrh_ref, wrl_ref,
                 x1_ref, h2_ref, aff_ref, hs_ref, *, n_tokens):
    t = pl.program_id(1)
    nt = pl.num_programs(1)
    tt = x_ref.shape[1]
    n1g, sh1, sc1 = n1g_ref[...], sh1_ref[0], sc1_ref[0]
    x = x_ref[0]
    h = _norm_mod(x, n1g, sh1, sc1)
    hp = _norm_mod(xp_ref[0], n1g, sh1, sc1) * (t > 0).astype(F32)
    hn = _norm_mod(xn_ref[0], n1g, sh1, sc1) * (t < nt - 1).astype(F32)
    hs_ref[0:HALO, :] = hp
    hs_ref[HALO:HALO + tt, :] = h
    hs_ref[HALO + tt:HALO + tt + HALO, :] = hn
    pos = t * tt + lax.broadcasted_iota(I32, (tt, 1), 0)
    ys = []
    for g, w in enumerate(POOL_WINDOWS):
        half = w // 2
        c0 = g * POOL_GROUP_DIM
        s = hs_ref[HALO - half:HALO - half + tt, c0:c0 + POOL_GROUP_DIM]
        for k in range(1, w):
            s = s + hs_ref[HALO - half + k:HALO - half + k + tt, c0:c0 + POOL_GROUP_DIM]
        lo = jnp.clip(pos - half, 0, n_tokens)
        hi = jnp.clip(pos + half, 0, n_tokens)
        p = s / (hi - lo).astype(F32) - h[:, c0:c0 + POOL_GROUP_DIM]
        ys.append(jnp.dot(p.astype(BF16), pw_ref[g], preferred_element_type=F32))
    y = jnp.concatenate(ys, axis=1) * ps_ref[...]
    x1 = x + g1_ref[0] * y
    x1_ref[0] = x1
    h2, aff = _route(x1, n2g_ref[...], sh2_ref[0], sc2_ref[0], wrh_ref[...], wrl_ref[...])
    h2_ref[0] = h2
    aff_ref[0] = aff


def _row_spec(d):
    return pl.BlockSpec((1, d), lambda b, t: (0, 0))


def _mod_spec(d):
    return pl.BlockSpec((1, 1, d), lambda b, t: (b, 0, 0))


def _pool_layer(x, n1g, sh1, sc1, g1, pw, ps, n2g, sh2, sc2, wrh, wrl, tt):
    bsz, n, d = x.shape
    nb8 = n // HALO
    r = tt // HALO
    kern = functools.partial(_pool_kernel, n_tokens=n)
    return pl.pallas_call(
        kern,
        grid=(bsz, n // tt),
        in_specs=[
            pl.BlockSpec((1, tt, d), lambda b, t: (b, t, 0)),
            pl.BlockSpec((1, HALO, d), lambda b, t: (b, jnp.maximum(t * r - 1, 0), 0)),
            pl.BlockSpec((1, HALO, d), lambda b, t: (b, jnp.minimum((t + 1) * r, nb8 - 1), 0)),
            _row_spec(d), _mod_spec(d), _mod_spec(d), _mod_spec(d),
            pl.BlockSpec((4, POOL_GROUP_DIM, POOL_GROUP_DIM), lambda b, t: (0, 0, 0)),
            _row_spec(d),
            _row_spec(d), _mod_spec(d), _mod_spec(d),
            pl.BlockSpec((N_EXPERTS, d), lambda b, t: (0, 0)),
            pl.BlockSpec((N_EXPERTS, d), lambda b, t: (0, 0)),
        ],
        out_specs=[
            pl.BlockSpec((1, tt, d), lambda b, t: (b, t, 0)),
            pl.BlockSpec((1, tt, d), lambda b, t: (b, t, 0)),
            pl.BlockSpec((1, N_EXPERTS, tt), lambda b, t: (b, 0, t)),
        ],
        out_shape=[
            jax.ShapeDtypeStruct((bsz, n, d), F32),
            jax.ShapeDtypeStruct((bsz, n, d), BF16),
            jax.ShapeDtypeStruct((bsz, N_EXPERTS, n), F32),
        ],
        scratch_shapes=[pltpu.VMEM((tt + 2 * HALO, d), F32)],
        compiler_params=_cparams(("arbitrary", "arbitrary")),
        name="pool_route",
    )(x, x, x, n1g, sh1, sc1, g1, pw, ps, n2g, sh2, sc2, wrh, wrl)


def _head_rms(v, bavg, gain):
    sq = v * v
    sq_hi = sq.astype(BF16)
    sq_lo = (sq - sq_hi.astype(F32)).astype(BF16)
    ms = (jnp.dot(sq_hi, bavg, preferred_element_type=F32)
          + jnp.dot(sq_lo, bavg, preferred_element_type=F32))
    return v * lax.rsqrt(ms + NORM_EPS) * gain


def _rope(v, cos, sin_signed):
    lane = lax.broadcasted_iota(I32, v.shape, 1)
    fwd = pltpu.roll(v, 32, 1)
    bwd = pltpu.roll(v, 96, 1)
    partner = jnp.where((lane % HEAD_DIM) < HEAD_DIM // 2, bwd, fwd)
    return v * cos + partner * sin_signed


def _qkv_kernel(x_ref, n1g_ref, sh1_ref, sc1_ref, w_ref, bavg_ref, qg_ref, kg_ref,
                cos_ref, sin_ref, q_ref, k_ref, v_ref):
    h = _norm_mod(x_ref[0], n1g_ref[...], sh1_ref[0], sc1_ref[0]).astype(BF16)
    qkv = jnp.dot(h, w_ref[...], preferred_element_type=F32)
    nq = N_Q_HEADS * HEAD_DIM
    nk = N_KV_HEADS * HEAD_DIM
    bavg = bavg_ref[...]
    cos, sin = cos_ref[...], sin_ref[...]
    scale = HEAD_DIM ** -0.5
    for c in range(nq // 256):
        qn = _head_rms(qkv[:, c * 256:(c + 1) * 256], bavg, qg_ref[...])
        for s in range(2):
            qr = _rope(qn[:, s * LANES:(s + 1) * LANES], cos, sin) * scale
            for hh in range(2):
                head = c * 4 + s * 2 + hh
                q_ref[0, head] = qr[:, hh * HEAD_DIM:(hh + 1) * HEAD_DIM].astype(BF16)
    kn = _head_rms(qkv[:, nq:nq + nk], bavg, kg_ref[...])
    v = qkv[:, nq + nk:]
    for s in range(2):
        kr = _rope(kn[:, s * LANES:(s + 1) * LANES], cos, sin)
        for hh in range(2):
            head = s * 2 + hh
            k_ref[0, head] = kr[:, hh * HEAD_DIM:(hh + 1) * HEAD_DIM].astype(BF16)
            v_ref[0, head] = v[:, head * HEAD_DIM:(head + 1) * HEAD_DIM].astype(BF16)


def _qkv_layer(x, n1g, sh1, sc1, w, bavg, qg, kg, cos, sin, tt):
    bsz, n, d = x.shape
    return pl.pallas_call(
        _qkv_kernel,
        grid=(bsz, n // tt),
        in_specs=[
            pl.BlockSpec((1, tt, d), lambda b, t: (b, t, 0)),
            _row_spec(d), _mod_spec(d), _mod_spec(d),
            pl.BlockSpec(w.shape, lambda b, t: (0, 0)),
            pl.BlockSpec((256, 256), lambda b, t: (0, 0)),
            _row_spec(256), _row_spec(256),
            pl.BlockSpec((tt, LANES), lambda b, t: (t, 0)),
            pl.BlockSpec((tt, LANES), lambda b, t: (t, 0)),
        ],
        out_specs=[
            pl.BlockSpec((1, N_Q_HEADS, tt, HEAD_DIM), lambda b, t: (b, 0, t, 0)),
            pl.BlockSpec((1, N_KV_HEADS, tt, HEAD_DIM), lambda b, t: (b, 0, t, 0)),
            pl.BlockSpec((1, N_KV_HEADS, tt, HEAD_DIM), lambda b, t: (b, 0, t, 0)),
        ],
        out_shape=[
            jax.ShapeDtypeStruct((bsz, N_Q_HEADS, n, HEAD_DIM), BF16),
            jax.ShapeDtypeStruct((bsz, N_KV_HEADS, n, HEAD_DIM), BF16),
            jax.ShapeDtypeStruct((bsz, N_KV_HEADS, n, HEAD_DIM), BF16),
        ],
        compiler_params=_cparams(("arbitrary", "arbitrary")),
        name="qkv",
    )(x, n1g, sh1, sc1, w, bavg, qg, kg, cos, sin)


def _attn_kernel(sink_ref, q_ref, *refs, band):
    if band:
        kp_ref, kc_ref, kn_ref, vp_ref, vc_ref, vn_ref, kx_ref, vx_ref, o_ref = refs
    else:
        kx_ref, vx_ref, o_ref = refs
    blk = pl.program_id(1)
    nb = pl.num_programs(1)
    tq = q_ref.shape[2]
    lc = kx_ref.shape[2]
    if band:
        nband = BLOCK_Q + 2 * WINDOW
        row = lax.broadcasted_iota(I32, (tq, nband + lc), 0)
        col = lax.broadcasted_iota(I32, (tq, nband + lc), 1)
        dlt = col - row
        inband = ((dlt >= 0) & (dlt <= 2 * WINDOW)
                  & ((col >= WINDOW) | (blk > 0))
                  & ((col < WINDOW + BLOCK_Q) | (blk < nb - 1)))
        mask = inband | (col >= nband)
    for g in range(N_KV_HEADS):
        if band:
            kg = jnp.concatenate([kp_ref[0, g], kc_ref[0, g], kn_ref[0, g], kx_ref[0, g]], axis=0)
            vg = jnp.concatenate([vp_ref[0, g], vc_ref[0, g], vn_ref[0, g], vx_ref[0, g]], axis=0)
        else:
            kg = kx_ref[0, g]
            vg = vx_ref[0, g]
        for hh in range(Q_PER_KV):
            head = g * Q_PER_KV + hh
            s = lax.dot_general(q_ref[0, head], kg, NT_DIMS, preferred_element_type=F32)
            if band:
                s = jnp.where(mask, s, NEG_INF)
            sk = sink_ref[head]
            m = jnp.maximum(jnp.max(s, axis=-1, keepdims=True), sk)
            p = jnp.exp(s - m)
            den = jnp.sum(p, axis=-1, keepdims=True) + jnp.exp(sk - m)
            o = jnp.dot(p.astype(BF16), vg, preferred_element_type=F32) / den
            o_ref[0, :, head * HEAD_DIM:(head + 1) * HEAD_DIM] = o.astype(o_ref.dtype)


def _attn_layer(sink, q, k, v, kx, vx, band):
    bsz, _, n, _ = q.shape
    lc = kx.shape[2]
    tq = BLOCK_Q if band else n
    nb = n // tq
    qspec = pl.BlockSpec((1, N_Q_HEADS, tq, HEAD_DIM), lambda b, t: (b, 0, t, 0))
    xspec = pl.BlockSpec((1, N_KV_HEADS, lc, HEAD_DIM), lambda b, t: (b, 0, 0, 0))
    if band:
        prev = pl.BlockSpec((1, N_KV_HEADS, tq, HEAD_DIM), lambda b, t: (b, 0, jnp.maximum(t - 1, 0), 0))
        cur = pl.BlockSpec((1, N_KV_HEADS, tq, HEAD_DIM), lambda b, t: (b, 0, t, 0))
        nxt = pl.BlockSpec((1, N_KV_HEADS, tq, HEAD_DIM), lambda b, t: (b, 0, jnp.minimum(t + 1, nb - 1), 0))
        in_specs = [qspec, prev, cur, nxt, prev, cur, nxt, xspec, xspec]
        args = (q, k, k, k, v, v, v, kx, vx)
    else:
        in_specs = [qspec, xspec, xspec]
        args = (q, kx, vx)
    return pl.pallas_call(
        functools.partial(_attn_kernel, band=band),
        grid=(bsz, nb),
        in_specs=[pl.BlockSpec(memory_space=pltpu.SMEM)] + in_specs,
        out_specs=pl.BlockSpec((1, tq, N_Q_HEADS * HEAD_DIM), lambda b, t: (b, t, 0)),
        out_shape=jax.ShapeDtypeStruct((bsz, n, N_Q_HEADS * HEAD_DIM), BF16),
        compiler_params=_cparams(("arbitrary", "arbitrary")),
        name="attn_band" if band else "attn_ctx",
    )(sink, *args)


def _oproj_kernel(o_ref, x_ref, wo_ref, g1_ref, n2g_ref, sh2_ref, sc2_ref, wrh_ref, wrl_ref,
                  x1_ref, h2_ref, aff_ref):
    y = jnp.dot(o_ref[0], wo_ref[...], preferred_element_type=F32)
    x1 = x_ref[0] + g1_ref[0] * y
    x1_ref[0] = x1
    h2, aff = _route(x1, n2g_ref[...], sh2_ref[0], sc2_ref[0], wrh_ref[...], wrl_ref[...])
    h2_ref[0] = h2
    aff_ref[0] = aff


def _oproj_layer(o, x, wo, g1, n2g, sh2, sc2, wrh, wrl, tt):
    bsz, n, d = x.shape
    return pl.pallas_call(
        _oproj_kernel,
        grid=(bsz, n // tt),
        in_specs=[
            pl.BlockSpec((1, tt, d), lambda b, t: (b, t, 0)),
            pl.BlockSpec((1, tt, d), lambda b, t: (b, t, 0)),
            pl.BlockSpec((d, d), lambda b, t: (0, 0)),
            _mod_spec(d), _row_spec(d), _mod_spec(d), _mod_spec(d),
            pl.BlockSpec((N_EXPERTS, d), lambda b, t: (0, 0)),
            pl.BlockSpec((N_EXPERTS, d), lambda b, t: (0, 0)),
        ],
        out_specs=[
            pl.BlockSpec((1, tt, d), lambda b, t: (b, t, 0)),
            pl.BlockSpec((1, tt, d), lambda b, t: (b, t, 0)),
            pl.BlockSpec((1, N_EXPERTS, tt), lambda b, t: (b, 0, t)),
        ],
        out_shape=[
            jax.ShapeDtypeStruct((bsz, n, d), F32),
            jax.ShapeDtypeStruct((bsz, n, d), BF16),
            jax.ShapeDtypeStruct((bsz, N_EXPERTS, n), F32),
        ],
        compiler_params=_cparams(("arbitrary", "arbitrary")),
        name="oproj_route",
    )(o, x, wo, g1, n2g, sh2, sc2, wrh, wrl)


def _prefix_counts(mask, n_chunks):
    ne = mask.shape[0]
    m2 = mask.reshape(ne * n_chunks, LANES).astype(BF16)
    ri = lax.broadcasted_iota(I32, (LANES, LANES), 0)
    ci = lax.broadcasted_iota(I32, (LANES, LANES), 1)
    upper = (ri <= ci).astype(BF16)
    within = jnp.dot(m2, upper, preferred_element_type=F32).reshape(ne, n_chunks, LANES)
    tot = jnp.dot(m2, jnp.ones((LANES, LANES), BF16), preferred_element_type=F32).reshape(ne, n_chunks, LANES)
    rc = lax.broadcasted_iota(I32, (n_chunks, n_chunks), 0)
    cc = lax.broadcasted_iota(I32, (n_chunks, n_chunks), 1)
    strict_lower = (cc < rc).astype(F32)
    off = jnp.stack([jnp.dot(strict_lower, tot[e], preferred_element_type=F32) for e in range(ne)])
    return within + off, off


def _select_kernel(a_ref, slot_ref, offc_ref, *, cap):
    a = a_ref[0]
    ne, n_chunks, _ = a.shape
    bits = lax.bitcast_convert_type(a, I32)

    def count(pred):
        c = jnp.sum(pred.astype(F32), axis=1, keepdims=True)
        return jnp.sum(c, axis=2, keepdims=True)

    def body(i, thr):
        cand = thr | lax.shift_left(jnp.int32(1), 30 - i)
        return jnp.where(count(bits >= cand) >= cap, cand, thr)

    thr = lax.fori_loop(0, 31, body, jnp.zeros((ne, 1, 1), I32))
    gt = bits > thr
    eq = bits == thr
    need = cap - count(gt)
    eq_rank, _ = _prefix_counts(eq.astype(F32), n_chunks)
    sel = gt | (eq & (eq_rank <= need))
    sel_f = sel.astype(F32)
    incl, off = _prefix_counts(sel_f, n_chunks)
    slot_ref[0] = jnp.where(sel, (incl - 1.0).astype(I32), -1)
    offc_ref[0] = off.astype(I32)


def _select(aff4, cap):
    bsz, ne, n_chunks, cw = aff4.shape
    spec = pl.BlockSpec((1, ne, n_chunks, cw), lambda b: (b, 0, 0, 0))
    return pl.pallas_call(
        functools.partial(_select_kernel, cap=cap),
        grid=(bsz,),
        in_specs=[spec],
        out_specs=[spec, spec],
        out_shape=[jax.ShapeDtypeStruct(aff4.shape, I32), jax.ShapeDtypeStruct(aff4.shape, I32)],
        compiler_params=_cparams(("arbitrary",)),
        name="select",
    )(aff4)


def _window(cap, tile):
    return min(tile + BF16_ROWS, cap)


def _window_start(off, cap, win):
    start = lax.shift_left(lax.shift_right_logical(off, 4), 4)
    return pl.multiple_of(jnp.minimum(start, cap - win), BF16_ROWS)


def _gather_kernel(offs_ref, h2_ref, sl_ref, af_ref, xs_ref, val_ref, acc_ref, vacc_ref,
                   *, cap, tile, n_tiles):
    b, e, s = pl.program_id(0), pl.program_id(1), pl.program_id(2)
    ns = pl.num_programs(2)
    sub = h2_ref.shape[1] // tile
    win = _window(cap, tile)

    @pl.when(s == 0)
    def _():
        acc_ref[...] = jnp.zeros_like(acc_ref)
        vacc_ref[...] = jnp.zeros_like(vacc_ref)

    for u in range(sub):
        off = offs_ref[(b * N_EXPERTS + e) * n_tiles + s * sub + u]
        start = _window_start(off, cap, win)
        slm = sl_ref[0, 0, :, u * tile:(u + 1) * tile]
        afr = af_ref[0, 0, :, u * tile:(u + 1) * tile]
        rows = start + lax.broadcasted_iota(I32, (win, tile), 0)
        hit = rows == slm
        res = jnp.dot(hit.astype(BF16), h2_ref[0, u * tile:(u + 1) * tile, :],
                      preferred_element_type=F32)
        acc_ref[pl.ds(start, win), :] += res
        vacc_ref[pl.ds(start, win), :] += jnp.sum(jnp.where(hit, afr, 0.0), axis=1, keepdims=True)

    @pl.when(s == ns - 1)
    def _():
        xs_ref[0, 0] = acc_ref[...].astype(BF16)
        val_ref[0, 0] = vacc_ref[...]


def _gather(offs, h2, slot, aff, cap, tile):
    bsz, n, d = h2.shape
    tg = min(n, 2048)
    n_tiles = n // tile
    kern = functools.partial(_gather_kernel, cap=cap, tile=tile, n_tiles=n_tiles)
    return pl.pallas_call(
        kern,
        grid_spec=pltpu.PrefetchScalarGridSpec(
            num_scalar_prefetch=1,
            grid=(bsz, N_EXPERTS, n // tg),
            in_specs=[
                pl.BlockSpec((1, tg, d), lambda b, e, s, offs: (b, s, 0)),
                pl.BlockSpec((1, 1, 1, tg), lambda b, e, s, offs: (b, e, 0, s)),
                pl.BlockSpec((1, 1, 1, tg), lambda b, e, s, offs: (b, e, 0, s)),
            ],
            out_specs=[
                pl.BlockSpec((1, 1, cap, d), lambda b, e, s, offs: (b, e, 0, 0)),
                pl.BlockSpec((1, 1, cap, 1), lambda b, e, s, offs: (b, e, 0, 0)),
            ],
            scratch_shapes=[pltpu.VMEM((cap, d), F32), pltpu.VMEM((cap, 1), F32)],
        ),
        out_shape=[
            jax.ShapeDtypeStruct((bsz, N_EXPERTS, cap, d), BF16),
            jax.ShapeDtypeStruct((bsz, N_EXPERTS, cap, 1), F32),
        ],
        compiler_params=_cparams(("arbitrary", "arbitrary", "arbitrary")),
        name="gather",
    )(offs, h2, slot.reshape(bsz, N_EXPERTS, 1, n), aff.reshape(bsz, N_EXPERTS, 1, n))


def _mlp_kernel(*refs, has_ctx):
    if has_ctx:
        xl_ref, vl_ref, xc_ref, vc_ref, wg_ref, wu_ref, wd_ref, yl_ref, yc_ref, xs_ref, acc_ref = refs
    else:
        xl_ref, vl_ref, wg_ref, wu_ref, wd_ref, yl_ref, xs_ref, acc_ref = refs
    f = pl.program_id(2)
    nf = pl.num_programs(2)
    cl = xl_ref.shape[2]

    @pl.when(f == 0)
    def _():
        xs_ref[0:cl, :] = xl_ref[0, 0]
        if has_ctx:
            xs_ref[cl:, :] = xc_ref[0, 0]

    xs = xs_ref[...]
    a = jnp.dot(xs, wg_ref[0, 0].astype(BF16), preferred_element_type=F32)
    u = jnp.dot(xs, wu_ref[0, 0].astype(BF16), preferred_element_type=F32)
    hid = (a * jax.nn.sigmoid(a) * u).astype(BF16)
    part = jnp.dot(hid, wd_ref[0, 0].astype(BF16), preferred_element_type=F32)

    @pl.when(f == 0)
    def _():
        acc_ref[...] = part

    @pl.when(f > 0)
    def _():
        acc_ref[...] += part

    @pl.when(f == nf - 1)
    def _():
        yl_ref[0, 0] = (acc_ref[0:cl, :] * vl_ref[0, 0]).astype(BF16)
        if has_ctx:
            yc_ref[0, 0] = (acc_ref[cl:, :] * vc_ref[0, 0]).astype(BF16)


def _mlp(layer, xl, vl, xc, vc, wg, wu, wd):
    bsz, ne, cl, d = xl.shape
    fdim = wg.shape[-1]
    fc = 256
    has_ctx = xc is not None
    cc = xc.shape[2] if has_ctx else 0

    def row_specs(c):
        return [pl.BlockSpec((1, 1, c, d), lambda e, b, f: (b, e, 0, 0)),
                pl.BlockSpec((1, 1, c, 1), lambda e, b, f: (b, e, 0, 0))]

    in_specs = row_specs(cl) + (row_specs(cc) if has_ctx else []) + [
        pl.BlockSpec((1, 1, d, fc), lambda e, b, f: (layer, e, 0, f)),
        pl.BlockSpec((1, 1, d, fc), lambda e, b, f: (layer, e, 0, f)),
        pl.BlockSpec((1, 1, fc, d), lambda e, b, f: (layer, e, f, 0)),
    ]
    out_specs = [pl.BlockSpec((1, 1, cl, d), lambda e, b, f: (b, e, 0, 0))]
    out_shape = [jax.ShapeDtypeStruct((bsz, ne, cl, d), BF16)]
    if has_ctx:
        out_specs.append(pl.BlockSpec((1, 1, cc, d), lambda e, b, f: (b, e, 0, 0)))
        out_shape.append(jax.ShapeDtypeStruct((bsz, ne, cc, d), BF16))
    args = (xl, vl) + ((xc, vc) if has_ctx else ()) + (wg, wu, wd)
    return pl.pallas_call(
        functools.partial(_mlp_kernel, has_ctx=has_ctx),
        grid=(ne, bsz, fdim // fc),
        in_specs=in_specs,
        out_specs=out_specs,
        out_shape=out_shape,
        scratch_shapes=[pltpu.VMEM((cl + cc, d), BF16), pltpu.VMEM((cl + cc, d), F32)],
        compiler_params=_cparams(("arbitrary", "arbitrary", "arbitrary")),
        name="experts",
    )(*args)


def _combine_kernel(offs_ref, x1_ref, g2_ref, sl_ref, y_hbm, o_ref, ywin_ref, sems,
                    *, cap, tile, n_tiles):
    b, t = pl.program_id(0), pl.program_id(1)
    win = _window(cap, tile)

    def window_copy(e):
        off = offs_ref[(b * N_EXPERTS + e) * n_tiles + t]
        start = _window_start(off, cap, win)
        cp = pltpu.make_async_copy(y_hbm.at[b, e, pl.ds(start, win), :], ywin_ref.at[e], sems.at[e])
        return cp, start

    for e in range(N_EXPERTS):
        window_copy(e)[0].start()

    sl = sl_ref[0] + 1
    eye = (lax.broadcasted_iota(I32, (tile, tile), 0)
           == lax.broadcasted_iota(I32, (tile, tile), 1)).astype(BF16)
    d_hi = lax.shift_right_logical(sl, 6).astype(F32).astype(BF16)
    d_lo = (sl & 63).astype(F32).astype(BF16)
    slot_t = (lax.dot_general(eye, d_hi, NT_DIMS, preferred_element_type=F32) * 64.0
              + lax.dot_general(eye, d_lo, NT_DIMS, preferred_element_type=F32)).astype(I32) - 1

    acc = jnp.zeros((tile, x1_ref.shape[2]), F32)
    for e in range(N_EXPERTS):
        cp, start = window_copy(e)
        cp.wait()
        cols = start + lax.broadcasted_iota(I32, (tile, win), 1)
        hit = (cols == slot_t[:, e:e + 1]).astype(BF16)
        acc = acc + jnp.dot(hit, ywin_ref[e], preferred_element_type=F32)
    o_ref[0] = x1_ref[0] + g2_ref[0] * acc


def _combine(offs, x1, g2, slot, y, cap, tile):
    bsz, n, d = x1.shape
    n_tiles = n // tile
    win = _window(cap, tile)
    kern = functools.partial(_combine_kernel, cap=cap, tile=tile, n_tiles=n_tiles)
    return pl.pallas_call(
        kern,
        grid_spec=pltpu.PrefetchScalarGridSpec(
            num_scalar_prefetch=1,
            grid=(bsz, n_tiles),
            in_specs=[
                pl.BlockSpec((1, tile, d), lambda b, t, offs: (b, t, 0)),
                pl.BlockSpec((1, 1, d), lambda b, t, offs: (b, 0, 0)),
                pl.BlockSpec((1, N_EXPERTS, tile), lambda b, t, offs: (b, 0, t)),
                pl.BlockSpec(memory_space=pl.ANY),
            ],
            out_specs=pl.BlockSpec((1, tile, d), lambda b, t, offs: (b, t, 0)),
            scratch_shapes=[pltpu.VMEM((N_EXPERTS, win, d), BF16),
                            pltpu.SemaphoreType.DMA((N_EXPERTS,))],
        ),
        out_shape=jax.ShapeDtypeStruct((bsz, n, d), F32),
        compiler_params=_cparams(("arbitrary", "arbitrary")),
        name="combine",
    )(offs, x1, g2, slot, y)


def _deinterleave(w, n_heads):
    lead = w.shape[:-1]
    w = w.reshape(lead + (n_heads, HEAD_DIM // 2, 2))
    return jnp.swapaxes(w, -1, -2).reshape(lead + (n_heads * HEAD_DIM,))


def _rope_tables(n):
    rows = n // GRID_W
    row = jnp.repeat(jnp.arange(rows), GRID_W).astype(F32)
    col = jnp.tile(jnp.arange(GRID_W), rows).astype(F32)
    n_freq = HEAD_DIM // 4
    inv = ROPE_THETA ** (-jnp.arange(n_freq, dtype=F32) / n_freq)
    ang = jnp.concatenate([row[:, None] * inv, col[:, None] * inv], axis=-1)
    cos = jnp.tile(jnp.cos(ang), (1, 4))
    sin = jnp.sin(ang)
    sin_signed = jnp.tile(jnp.concatenate([-sin, sin], axis=-1), (1, 2))
    return cos, sin_signed


def _split_bf16(w):
    hi = w.astype(BF16)
    return hi, (w - hi.astype(F32)).astype(BF16)


def _route_select(aff, cap, tile):
    bsz, ne, n = aff.shape
    n_pad = max(n, HALO * LANES)
    a = aff if n_pad == n else jnp.pad(aff, ((0, 0), (0, 0), (0, n_pad - n)))
    slot4, offc4 = _select(a.reshape(bsz, ne, n_pad // LANES, LANES), cap)
    n_tiles = n // tile
    offs = offc4[:, :, ::tile // LANES, 0][:, :, :n_tiles].reshape(-1)
    return slot4.reshape(bsz, ne, n_pad)[:, :, :n], offs


def kernel(x, c, ctx, c_ctx, w_ada, b_ada, norm1_g, norm2_g, pool_w, pool_scale, attn_w_qkv, attn_w_o,
           attn_q_norm, attn_k_norm, attn_sink, router_w, exp_w_gate, exp_w_up, exp_w_down):
    bsz, n, d = x.shape
    lc = ctx.shape[1]
    cap = 2 * n // N_EXPERTS
    cap_c = 2 * lc // N_EXPERTS
    tile_c = min(TOK_TILE, lc)

    cc = jnp.zeros((8, d), F32).at[:bsz].set(c).at[bsz].set(c_ctx)
    mods = _ada(cc, w_ada, b_ada)

    cos, sin = _rope_tables(n)
    cos_c = jnp.ones((lc, LANES), F32)
    sin_c = jnp.zeros((lc, LANES), F32)
    blk = jnp.arange(256) // HEAD_DIM
    bavg = ((blk[:, None] == blk[None, :]).astype(F32) / HEAD_DIM).astype(BF16)

    for i in range(DEPTH):
        update_ctx = i < DEPTH - 1
        j = i // 2
        is_pool = (i % 2) == 0
        m = [mods[i, :, k * d:(k + 1) * d] for k in range(6)]
        sh1, sc1, g1, sh2, sc2, g2 = [v[:bsz, None, :] for v in m]
        csh1, csc1, cg1, csh2, csc2, cg2 = [jnp.broadcast_to(v[bsz][None, None, :], (bsz, 1, d)) for v in m]
        n1g = norm1_g[i][None, :]
        n2g = norm2_g[i][None, :]
        wrh, wrl = _split_bf16(router_w[i].T)

        if is_pool:
            pw = pool_w[j].astype(BF16)
            ps = pool_scale[j][None, :]
            x1, h2, aff = _pool_layer(x, n1g, sh1, sc1, g1, pw, ps, n2g, sh2, sc2, wrh, wrl, 512)
            if update_ctx:
                c1, hc2, affc = _pool_layer(ctx, n1g, csh1, csc1, cg1, pw, ps, n2g, csh2, csc2, wrh, wrl, lc)
        else:
            nq = N_Q_HEADS * HEAD_DIM
            nk = N_KV_HEADS * HEAD_DIM
            wqkv = attn_w_qkv[j]
            wperm = jnp.concatenate([_deinterleave(wqkv[:, :nq], N_Q_HEADS),
                                     _deinterleave(wqkv[:, nq:nq + nk], N_KV_HEADS),
                                     wqkv[:, nq + nk:]], axis=1).astype(BF16)
            qg = jnp.tile(_deinterleave(attn_q_norm[j], 1), 4)[None, :]
            kg = jnp.tile(_deinterleave(attn_k_norm[j], 1), 4)[None, :]
            wo = attn_w_o[j].astype(BF16)
            sink = attn_sink[j]
            q, k, v = _qkv_layer(x, n1g, sh1, sc1, wperm, bavg, qg, kg, cos, sin, 512)
            qc, kc, vc = _qkv_layer(ctx, n1g, csh1, csc1, wperm, bavg, qg, kg, cos_c, sin_c, lc)
            o = _attn_layer(sink, q, k, v, kc, vc, True)
            x1, h2, aff = _oproj_layer(o, x, wo, g1, n2g, sh2, sc2, wrh, wrl, 512)
            if update_ctx:
                oc = _attn_layer(sink, qc, None, None, kc, vc, False)
                c1, hc2, affc = _oproj_layer(oc, ctx, wo, cg1, n2g, csh2, csc2, wrh, wrl, lc)

        slot, offs = _route_select(aff, cap, TOK_TILE)
        xs, vals = _gather(offs, h2, slot, aff, cap, TOK_TILE)
        if update_ctx:
            slot_c, offs_c = _route_select(affc, cap_c, tile_c)
            xs_c, vals_c = _gather(offs_c, hc2, slot_c, affc, cap_c, tile_c)
            y, y_c = _mlp(i, xs, vals, xs_c, vals_c, exp_w_gate, exp_w_up, exp_w_down)
            ctx = _combine(offs_c, c1, cg2, slot_c, y_c, cap_c, tile_c)
        else:
            (y,) = _mlp(i, xs, vals, None, None, exp_w_gate, exp_w_up, exp_w_down)
        x = _combine(offs, x1, g2, slot, y, cap, TOK_TILE)
    return x
```

```python
import functools

import jax
import jax.numpy as jnp
from jax import lax
from jax.experimental import pallas as pl
from jax.experimental.pallas import tpu as pltpu

F32 = jnp.float32
BF16 = jnp.bfloat16
I32 = jnp.int32

D_MODEL = 1024
DEPTH = 4
GRID_W = 64
POOL_WINDOWS = (2, 4, 8, 16)
POOL_GROUP_DIM = 256
HEAD_DIM = 64
N_Q_HEADS = 16
N_KV_HEADS = 4
Q_PER_KV = 4
WINDOW = 128
BLOCK_Q = 128
ROPE_THETA = 10000.0
N_EXPERTS = 16
D_EXPERT = 2048
NORM_EPS = 1e-6
NEG_INF = -1e30

LANES = 128
BF16_ROWS = 16
HALO = 8
TOK_TILE = 256
VMEM_LIMIT = 56 * 1024 * 1024

NT_DIMS = (((1,), (1,)), ((), ()))


def _cparams(sem, vmem=None):
    return pltpu.CompilerParams(dimension_semantics=sem, vmem_limit_bytes=vmem or VMEM_LIMIT)


def _norm_mod(x, g, shift, scale):
    ms = jnp.mean(x * x, axis=-1, keepdims=True)
    return (x * lax.rsqrt(ms + NORM_EPS)) * g * (1.0 + scale) + shift


def _ada_kernel(cc_ref, w_ref, b_ref, o_ref):
    cc = cc_ref[...]
    s = cc * jax.nn.sigmoid(cc)
    o_ref[0] = jnp.dot(s, w_ref[0], preferred_element_type=F32,
                       precision=lax.Precision.HIGHEST) + b_ref[0]


def _ada(cc, w_ada, b_ada):
    depth, d, six_d = w_ada.shape
    tn = 1536
    return pl.pallas_call(
        _ada_kernel,
        grid=(depth, six_d // tn),
        in_specs=[
            pl.BlockSpec((8, d), lambda i, j: (0, 0)),
            pl.BlockSpec((1, d, tn), lambda i, j: (i, 0, j)),
            pl.BlockSpec((1, 1, tn), lambda i, j: (i, 0, j)),
        ],
        out_specs=pl.BlockSpec((1, 8, tn), lambda i, j: (i, 0, j)),
        out_shape=jax.ShapeDtypeStruct((depth, 8, six_d), F32),
        compiler_params=_cparams(("arbitrary", "arbitrary")),
        name="ada",
    )(cc, w_ada, b_ada.reshape(depth, 1, six_d))


def _route(x1, n2g, sh2, sc2, wr_hi, wr_lo):
    h2 = _norm_mod(x1, n2g, sh2, sc2)
    h_hi = h2.astype(BF16)
    h_lo = (h2 - h_hi.astype(F32)).astype(BF16)
    lg = (lax.dot_general(wr_hi, h_hi, NT_DIMS, preferred_element_type=F32)
          + lax.dot_general(wr_hi, h_lo, NT_DIMS, preferred_element_type=F32)
          + lax.dot_general(wr_lo, h_hi, NT_DIMS, preferred_element_type=F32))
    m = jnp.max(lg, axis=0, keepdims=True)
    ex = jnp.exp(lg - m)
    aff = ex / jnp.sum(ex, axis=0, keepdims=True)
    return h_hi, aff


def _pool_kernel(x_ref, xp_ref, xn_ref, n1g_ref, sh1_ref, sc1_ref, g1_ref, pw_ref, ps_ref,
                 n2g_ref, sh2_ref, sc2_ref, wrh_ref, wrl_ref,
                 x1_ref, h2_ref, aff_ref, hs_ref, *, n_tokens):
    t = pl.program_id(1)
    nt = pl.num_programs(1)
    tt = x_ref.shape[1]
    n1g, sh1, sc1 = n1g_ref[...], sh1_ref[0], sc1_ref[0]
    x = x_ref[0]
    h = _norm_mod(x, n1g, sh1, sc1)
    hp = _norm_mod(xp_ref[0], n1g, sh1, sc1) * (t > 0).astype(F32)
    hn = _norm_mod(xn_ref[0], n1g, sh1, sc1) * (t < nt - 1).astype(F32)
    hs_ref[0:HALO, :] = hp
    hs_ref[HALO:HALO + tt, :] = h
    hs_ref[HALO + tt:HALO + tt + HALO, :] = hn
    pos = t * tt + lax.broadcasted_iota(I32, (tt, 1), 0)
    ys = []
    for g, w in enumerate(POOL_WINDOWS):
        half = w // 2
        c0 = g * POOL_GROUP_DIM
        s = hs_ref[HALO - half:HALO - half + tt, c0:c0 + POOL_GROUP_DIM]
        for k in range(1, w):
            s = s + hs_ref[HALO - half + k:HALO - half + k + tt, c0:c0 + POOL_GROUP_DIM]
        lo = jnp.clip(pos - half, 0, n_tokens)
        hi = jnp.clip(pos + half, 0, n_tokens)
        p = s / (hi - lo).astype(F32) - h[:, c0:c0 + POOL_GROUP_DIM]
        ys.append(jnp.dot(p.astype(BF16), pw_ref[g], preferred_element_type=F32))
    y = jnp.concatenate(ys, axis=1) * ps_ref[...]
    x1 = x + g1_ref[0] * y
    x1_ref[0] = x1
    h2, aff = _route(x1, n2g_ref[...], sh2_ref[0], sc2_ref[0], wrh_ref[...], wrl_ref[...])
    h2_ref[0] = h2
    aff_ref[0] = aff


def _row_spec(d):
    return pl.BlockSpec((1, d), lambda b, t: (0, 0))


def _mod_spec(d):
    return pl.BlockSpec((1, 1, d), lambda b, t: (b, 0, 0))


def _pool_layer(x, n1g, sh1, sc1, g1, pw, ps, n2g, sh2, sc2, wrh, wrl, tt):
    bsz, n, d = x.shape
    nb8 = n // HALO
    r = tt // HALO
    kern = functools.partial(_pool_kernel, n_tokens=n)
    return pl.pallas_call(
        kern,
        grid=(bsz, n // tt),
        in_specs=[
            pl.BlockSpec((1, tt, d), lambda b, t: (b, t, 0)),
            pl.BlockSpec((1, HALO, d), lambda b, t: (b, jnp.maximum(t * r - 1, 0), 0)),
            pl.BlockSpec((1, HALO, d), lambda b, t: (b, jnp.minimum((t + 1) * r, nb8 - 1), 0)),
            _row_spec(d), _mod_spec(d), _mod_spec(d), _mod_spec(d),
            pl.BlockSpec((4, POOL_GROUP_DIM, POOL_GROUP_DIM), lambda b, t: (0, 0, 0)),
            _row_spec(d),
            _row_spec(d), _mod_spec(d), _mod_spec(d),
            pl.BlockSpec((N_EXPERTS, d), lambda b, t: (0, 0)),
            pl.BlockSpec((N_EXPERTS, d), lambda b, t: (0, 0)),
        ],
        out_specs=[
            pl.BlockSpec((1, tt, d), lambda b, t: (b, t, 0)),
            pl.BlockSpec((1, tt, d), lambda b, t: (b, t, 0)),
            pl.BlockSpec((1, N_EXPERTS, tt), lambda b, t: (b, 0, t)),
        ],
        out_shape=[
            jax.ShapeDtypeStruct((bsz, n, d), F32),
            jax.ShapeDtypeStruct((bsz, n, d), BF16),
            jax.ShapeDtypeStruct((bsz, N_EXPERTS, n), F32),
        ],
        scratch_shapes=[pltpu.VMEM((tt + 2 * HALO, d), F32)],
        compiler_params=_cparams(("arbitrary", "arbitrary")),
        name="pool_route",
    )(x, x, x, n1g, sh1, sc1, g1, pw, ps, n2g, sh2, sc2, wrh, wrl)


def _head_rms(v, bavg, gain):
    sq = v * v
    sq_hi = sq.astype(BF16)
    sq_lo = (sq - sq_hi.astype(F32)).astype(BF16)
    ms = (jnp.dot(sq_hi, bavg, preferred_element_type=F32)
          + jnp.dot(sq_lo, bavg, preferred_element_type=F32))
    return v * lax.rsqrt(ms + NORM_EPS) * gain


def _rope(v, cos, sin_signed):
    lane = lax.broadcasted_iota(I32, v.shape, 1)
    fwd = pltpu.roll(v, 32, 1)
    bwd = pltpu.roll(v, 96, 1)
    partner = jnp.where((lane % HEAD_DIM) < HEAD_DIM // 2, bwd, fwd)
    return v * cos + partner * sin_signed


def _qkv_kernel(x_ref, n1g_ref, sh1_ref, sc1_ref, w_ref, bavg_ref, qg_ref, kg_ref,
                cos_ref, sin_ref, q_ref, k_ref, v_ref):
    h = _norm_mod(x_ref[0], n1g_ref[...], sh1_ref[0], sc1_ref[0]).astype(BF16)
    qkv = jnp.dot(h, w_ref[...], preferred_element_type=F32)
    nq = N_Q_HEADS * HEAD_DIM
    nk = N_KV_HEADS * HEAD_DIM
    bavg = bavg_ref[...]
    cos, sin = cos_ref[...], sin_ref[...]
    scale = HEAD_DIM ** -0.5
    for c in range(nq // 256):
        qn = _head_rms(qkv[:, c * 256:(c + 1) * 256], bavg, qg_ref[...])
        for s in range(2):
            qr = _rope(qn[:, s * LANES:(s + 1) * LANES], cos, sin) * scale
            for hh in range(2):
                head = c * 4 + s * 2 + hh
                q_ref[0, head] = qr[:, hh * HEAD_DIM:(hh + 1) * HEAD_DIM].astype(BF16)
    kn = _head_rms(qkv[:, nq:nq + nk], bavg, kg_ref[...])
    v = qkv[:, nq + nk:]
    for s in range(2):
        kr = _rope(kn[:, s * LANES:(s + 1) * LANES], cos, sin)
        for hh in range(2):
            head = s * 2 + hh
            k_ref[0, head] = kr[:, hh * HEAD_DIM:(hh + 1) * HEAD_DIM].astype(BF16)
            v_ref[0, head] = v[:, head * HEAD_DIM:(head + 1) * HEAD_DIM].astype(BF16)


def _qkv_layer(x, n1g, sh1, sc1, w, bavg, qg, kg, cos, sin, tt):
    bsz, n, d = x.shape
    return pl.pallas_call(
        _qkv_kernel,
        grid=(bsz, n // tt),
        in_specs=[
            pl.BlockSpec((1, tt, d), lambda b, t: (b, t, 0)),
            _row_spec(d), _mod_spec(d), _mod_spec(d),
            pl.BlockSpec(w.shape, lambda b, t: (0, 0)),
            pl.BlockSpec((256, 256), lambda b, t: (0, 0)),
            _row_spec(256), _row_spec(256),
            pl.BlockSpec((tt, LANES), lambda b, t: (t, 0)),
            pl.BlockSpec((tt, LANES), lambda b, t: (t, 0)),
        ],
        out_specs=[
            pl.BlockSpec((1, N_Q_HEADS, tt, HEAD_DIM), lambda b, t: (b, 0, t, 0)),
            pl.BlockSpec((1, N_KV_HEADS, tt, HEAD_DIM), lambda b, t: (b, 0, t, 0)),
            pl.BlockSpec((1, N_KV_HEADS, tt, HEAD_DIM), lambda b, t: (b, 0, t, 0)),
        ],
        out_shape=[
            jax.ShapeDtypeStruct((bsz, N_Q_HEADS, n, HEAD_DIM), BF16),
            jax.ShapeDtypeStruct((bsz, N_KV_HEADS, n, HEAD_DIM), BF16),
            jax.ShapeDtypeStruct((bsz, N_KV_HEADS, n, HEAD_DIM), BF16),
        ],
        compiler_params=_cparams(("arbitrary", "arbitrary")),
        name="qkv",
    )(x, n1g, sh1, sc1, w, bavg, qg, kg, cos, sin)


def _attn_kernel(sink_ref, q_ref, *refs, band):
    if band:
        kp_ref, kc_ref, kn_ref, vp_ref, vc_ref, vn_ref, kx_ref, vx_ref, bias_ref, o_ref = refs
    else:
        kx_ref, vx_ref, o_ref = refs
    tq = q_ref.shape[2]
    for g in range(N_KV_HEADS):
        if band:
            kg = jnp.concatenate([kp_ref[0, g], kc_ref[0, g], kn_ref[0, g], kx_ref[0, g]], axis=0)
            vg = jnp.concatenate([vp_ref[0, g], vc_ref[0, g], vn_ref[0, g], vx_ref[0, g]], axis=0)
        else:
            kg = kx_ref[0, g]
            vg = vx_ref[0, g]
        vext = jnp.concatenate([vg, jnp.ones_like(vg)], axis=1)
        qg = q_ref[0, g * Q_PER_KV:(g + 1) * Q_PER_KV].reshape(Q_PER_KV * tq, HEAD_DIM)
        s = lax.dot_general(qg, kg, NT_DIMS, preferred_element_type=F32)
        if band:
            s = s + bias_ref[0]
        sk = jnp.concatenate([jnp.full((tq, 1), sink_ref[g * Q_PER_KV + hh], F32)
                              for hh in range(Q_PER_KV)], axis=0)
        m = jnp.maximum(jnp.max(s, axis=-1, keepdims=True), sk)
        p = jnp.exp(s - m).astype(BF16)
        oe = jnp.dot(p, vext, preferred_element_type=F32)
        den = oe[:, HEAD_DIM:HEAD_DIM + 1] + jnp.exp(sk - m)
        o = (oe[:, :HEAD_DIM] / den).astype(o_ref.dtype)
        for hh in range(Q_PER_KV):
            head = g * Q_PER_KV + hh
            o_ref[0, :, head * HEAD_DIM:(head + 1) * HEAD_DIM] = o[hh * tq:(hh + 1) * tq]


def _band_bias(lc):
    nband = BLOCK_Q + 2 * WINDOW
    row = jnp.arange(Q_PER_KV * BLOCK_Q)[:, None] % BLOCK_Q
    col = jnp.arange(nband + lc)[None, :]
    dlt = col - row
    inband = (dlt >= 0) & (dlt <= 2 * WINDOW)
    variants = [inband & (col >= WINDOW), inband, inband & (col < WINDOW + BLOCK_Q)]
    return jnp.stack([jnp.where(v | (col >= nband), 0.0, NEG_INF) for v in variants]).astype(F32)


def _attn_layer(sink, q, k, v, kx, vx, band):
    bsz, _, n, _ = q.shape
    lc = kx.shape[2]
    tq = BLOCK_Q if band else n
    nb = n // tq
    qspec = pl.BlockSpec((1, N_Q_HEADS, tq, HEAD_DIM), lambda b, t: (b, 0, t, 0))
    xspec = pl.BlockSpec((1, N_KV_HEADS, lc, HEAD_DIM), lambda b, t: (b, 0, 0, 0))
    if band:
        prev = pl.BlockSpec((1, N_KV_HEADS, tq, HEAD_DIM), lambda b, t: (b, 0, jnp.maximum(t - 1, 0), 0))
        cur = pl.BlockSpec((1, N_KV_HEADS, tq, HEAD_DIM), lambda b, t: (b, 0, t, 0))
        nxt = pl.BlockSpec((1, N_KV_HEADS, tq, HEAD_DIM), lambda b, t: (b, 0, jnp.minimum(t + 1, nb - 1), 0))
        bias = _band_bias(lc)
        bspec = pl.BlockSpec((1,) + bias.shape[1:],
                             lambda b, t: (jnp.where(t == 0, 0, jnp.where(t == nb - 1, 2, 1)), 0, 0))
        in_specs = [qspec, prev, cur, nxt, prev, cur, nxt, xspec, xspec, bspec]
        args = (q, k, k, k, v, v, v, kx, vx, bias)
    else:
        in_specs = [qspec, xspec, xspec]
        args = (q, kx, vx)
    return pl.pallas_call(
        functools.partial(_attn_kernel, band=band),
        grid=(bsz, nb),
        in_specs=[pl.BlockSpec(memory_space=pltpu.SMEM)] + in_specs,
        out_specs=pl.BlockSpec((1, tq, N_Q_HEADS * HEAD_DIM), lambda b, t: (b, t, 0)),
        out_shape=jax.ShapeDtypeStruct((bsz, n, N_Q_HEADS * HEAD_DIM), BF16),
        compiler_params=_cparams(("arbitrary", "arbitrary")),
        name="attn_band" if band else "attn_ctx",
    )(sink, *args)


def _oproj_kernel(o_ref, x_ref, wo_ref, g1_ref, n2g_ref, sh2_ref, sc2_ref, wrh_ref, wrl_ref,
                  x1_ref, h2_ref, aff_ref):
    y = jnp.dot(o_ref[0], wo_ref[...], preferred_element_type=F32)
    x1 = x_ref[0] + g1_ref[0] * y
    x1_ref[0] = x1
    h2, aff = _route(x1, n2g_ref[...], sh2_ref[0], sc2_ref[0], wrh_ref[...], wrl_ref[...])
    h2_ref[0] = h2
    aff_ref[0] = aff


def _oproj_layer(o, x, wo, g1, n2g, sh2, sc2, wrh, wrl, tt):
    bsz, n, d = x.shape
    return pl.pallas_call(
        _oproj_kernel,
        grid=(bsz, n // tt),
        in_specs=[
            pl.BlockSpec((1, tt, d), lambda b, t: (b, t, 0)),
            pl.BlockSpec((1, tt, d), lambda b, t: (b, t, 0)),
            pl.BlockSpec((d, d), lambda b, t: (0, 0)),
            _mod_spec(d), _row_spec(d), _mod_spec(d), _mod_spec(d),
            pl.BlockSpec((N_EXPERTS, d), lambda b, t: (0, 0)),
            pl.BlockSpec((N_EXPERTS, d), lambda b, t: (0, 0)),
        ],
        out_specs=[
            pl.BlockSpec((1, tt, d), lambda b, t: (b, t, 0)),
            pl.BlockSpec((1, tt, d), lambda b, t: (b, t, 0)),
            pl.BlockSpec((1, N_EXPERTS, tt), lambda b, t: (b, 0, t)),
        ],
        out_shape=[
            jax.ShapeDtypeStruct((bsz, n, d), F32),
            jax.ShapeDtypeStruct((bsz, n, d), BF16),
            jax.ShapeDtypeStruct((bsz, N_EXPERTS, n), F32),
        ],
        compiler_params=_cparams(("arbitrary", "arbitrary")),
        name="oproj_route",
    )(o, x, wo, g1, n2g, sh2, sc2, wrh, wrl)


def _prefix_counts(mask, n_chunks):
    ne = mask.shape[0]
    m2 = mask.reshape(ne * n_chunks, LANES).astype(BF16)
    ri = lax.broadcasted_iota(I32, (LANES, LANES), 0)
    ci = lax.broadcasted_iota(I32, (LANES, LANES), 1)
    upper = (ri <= ci).astype(BF16)
    within = jnp.dot(m2, upper, preferred_element_type=F32).reshape(ne, n_chunks, LANES)
    tot = jnp.dot(m2, jnp.ones((LANES, LANES), BF16), preferred_element_type=F32).reshape(ne, n_chunks, LANES)
    rc = lax.broadcasted_iota(I32, (n_chunks, n_chunks), 0)
    cc = lax.broadcasted_iota(I32, (n_chunks, n_chunks), 1)
    strict_lower = (cc < rc).astype(F32)
    off = jnp.stack([jnp.dot(strict_lower, tot[e], preferred_element_type=F32) for e in range(ne)])
    return within + off, off


def _select_kernel(a_ref, slot_ref, offc_ref, *, cap):
    a = a_ref[0]
    ne, n_chunks, _ = a.shape
    bits = lax.bitcast_convert_type(a, I32)

    def count(pred):
        c = jnp.sum(pred.astype(F32), axis=1, keepdims=True)
        return jnp.sum(c, axis=2, keepdims=True)

    def body(i, thr):
        cand = thr | lax.shift_left(jnp.int32(1), 30 - i)
        return jnp.where(count(bits >= cand) >= cap, cand, thr)

    thr = lax.fori_loop(0, 31, body, jnp.zeros((ne, 1, 1), I32))
    gt = bits > thr
    eq = bits == thr
    need = cap - count(gt)
    eq_rank, _ = _prefix_counts(eq.astype(F32), n_chunks)
    sel = gt | (eq & (eq_rank <= need))
    sel_f = sel.astype(F32)
    incl, off = _prefix_counts(sel_f, n_chunks)
    slot_ref[0] = jnp.where(sel, (incl - 1.0).astype(I32), -1)
    offc_ref[0] = off.astype(I32)


def _select(aff4, cap):
    bsz, ne, n_chunks, cw = aff4.shape
    spec = pl.BlockSpec((1, ne, n_chunks, cw), lambda b: (b, 0, 0, 0))
    return pl.pallas_call(
        functools.partial(_select_kernel, cap=cap),
        grid=(bsz,),
        in_specs=[spec],
        out_specs=[spec, spec],
        out_shape=[jax.ShapeDtypeStruct(aff4.shape, I32), jax.ShapeDtypeStruct(aff4.shape, I32)],
        compiler_params=_cparams(("arbitrary",)),
        name="select",
    )(aff4)


def _window(cap, tile):
    return min(tile + BF16_ROWS, cap)


def _window_start(off, cap, win):
    start = lax.shift_left(lax.shift_right_logical(off, 4), 4)
    return pl.multiple_of(jnp.minimum(start, cap - win), BF16_ROWS)


def _gather_kernel(offs_ref, h2_ref, sl_ref, af_ref, xs_ref, val_ref, acc_ref, vacc_ref,
                   *, cap, tile, n_tiles):
    b, e, s = pl.program_id(0), pl.program_id(1), pl.program_id(2)
    ns = pl.num_programs(2)
    sub = h2_ref.shape[1] // tile
    win = _window(cap, tile)

    @pl.when(s == 0)
    def _():
        acc_ref[...] = jnp.zeros_like(acc_ref)
        vacc_ref[...] = jnp.zeros_like(vacc_ref)

    for u in range(sub):
        off = offs_ref[(b * N_EXPERTS + e) * n_tiles + s * sub + u]
        start = _window_start(off, cap, win)
        slm = sl_ref[0, 0, :, u * tile:(u + 1) * tile]
        afr = af_ref[0, 0, :, u * tile:(u + 1) * tile]
        rows = start + lax.broadcasted_iota(I32, (win, tile), 0)
        hit = rows == slm
        res = jnp.dot(hit.astype(BF16), h2_ref[0, u * tile:(u + 1) * tile, :],
                      preferred_element_type=F32)
        acc_ref[pl.ds(start, win), :] += res
        vacc_ref[pl.ds(start, win), :] += jnp.sum(jnp.where(hit, afr, 0.0), axis=1, keepdims=True)

    @pl.when(s == ns - 1)
    def _():
        xs_ref[0, 0] = acc_ref[...].astype(BF16)
        val_ref[0, 0] = vacc_ref[...]


def _gather(offs, h2, slot, aff, cap, tile):
    bsz, n, d = h2.shape
    tg = min(n, 2048)
    n_tiles = n // tile
    kern = functools.partial(_gather_kernel, cap=cap, tile=tile, n_tiles=n_tiles)
    return pl.pallas_call(
        kern,
        grid_spec=pltpu.PrefetchScalarGridSpec(
            num_scalar_prefetch=1,
            grid=(bsz, N_EXPERTS, n // tg),
            in_specs=[
                pl.BlockSpec((1, tg, d), lambda b, e, s, offs: (b, s, 0)),
                pl.BlockSpec((1, 1, 1, tg), lambda b, e, s, offs: (b, e, 0, s)),
                pl.BlockSpec((1, 1, 1, tg), lambda b, e, s, offs: (b, e, 0, s)),
            ],
            out_specs=[
                pl.BlockSpec((1, 1, cap, d), lambda b, e, s, offs: (b, e, 0, 0)),
                pl.BlockSpec((1, 1, cap, 1), lambda b, e, s, offs: (b, e, 0, 0)),
            ],
            scratch_shapes=[pltpu.VMEM((cap, d), F32), pltpu.VMEM((cap, 1), F32)],
        ),
        out_shape=[
            jax.ShapeDtypeStruct((bsz, N_EXPERTS, cap, d), BF16),
            jax.ShapeDtypeStruct((bsz, N_EXPERTS, cap, 1), F32),
        ],
        compiler_params=_cparams(("arbitrary", "arbitrary", "arbitrary")),
        name="gather",
    )(offs, h2, slot.reshape(bsz, N_EXPERTS, 1, n), aff.reshape(bsz, N_EXPERTS, 1, n))


def _mlp_kernel(*refs, has_ctx):
    if has_ctx:
        xl_ref, vl_ref, xc_ref, vc_ref, wg_ref, wu_ref, wd_ref, yl_ref, yc_ref, xs_ref, acc_ref = refs
    else:
        xl_ref, vl_ref, wg_ref, wu_ref, wd_ref, yl_ref, xs_ref, acc_ref = refs
    f = pl.program_id(2)
    nf = pl.num_programs(2)
    cl = xl_ref.shape[2]

    @pl.when(f == 0)
    def _():
        xs_ref[0:cl, :] = xl_ref[0, 0]
        if has_ctx:
            xs_ref[cl:, :] = xc_ref[0, 0]

    xs = xs_ref[...]
    a = jnp.dot(xs, wg_ref[0, 0].astype(BF16), preferred_element_type=F32)
    u = jnp.dot(xs, wu_ref[0, 0].astype(BF16), preferred_element_type=F32)
    hid = (a * jax.nn.sigmoid(a) * u).astype(BF16)
    part = jnp.dot(hid, wd_ref[0, 0].astype(BF16), preferred_element_type=F32)

    @pl.when(f == 0)
    def _():
        acc_ref[...] = part

    @pl.when(f > 0)
    def _():
        acc_ref[...] += part

    @pl.when(f == nf - 1)
    def _():
        yl_ref[0, 0] = (acc_ref[0:cl, :] * vl_ref[0, 0]).astype(BF16)
        if has_ctx:
            yc_ref[0, 0] = (acc_ref[cl:, :] * vc_ref[0, 0]).astype(BF16)


def _mlp(layer, xl, vl, xc, vc, wg, wu, wd):
    bsz, ne, cl, d = xl.shape
    fdim = wg.shape[-1]
    fc = 256
    has_ctx = xc is not None
    cc = xc.shape[2] if has_ctx else 0

    def row_specs(c):
        return [pl.BlockSpec((1, 1, c, d), lambda e, b, f: (b, e, 0, 0)),
                pl.BlockSpec((1, 1, c, 1), lambda e, b, f: (b, e, 0, 0))]

    in_specs = row_specs(cl) + (row_specs(cc) if has_ctx else []) + [
        pl.BlockSpec((1, 1, d, fc), lambda e, b, f: (layer, e, 0, f)),
        pl.BlockSpec((1, 1, d, fc), lambda e, b, f: (layer, e, 0, f)),
        pl.BlockSpec((1, 1, fc, d), lambda e, b, f: (layer, e, f, 0)),
    ]
    out_specs = [pl.BlockSpec((1, 1, cl, d), lambda e, b, f: (b, e, 0, 0))]
    out_shape = [jax.ShapeDtypeStruct((bsz, ne, cl, d), BF16)]
    if has_ctx:
        out_specs.append(pl.BlockSpec((1, 1, cc, d), lambda e, b, f: (b, e, 0, 0)))
        out_shape.append(jax.ShapeDtypeStruct((bsz, ne, cc, d), BF16))
    args = (xl, vl) + ((xc, vc) if has_ctx else ()) + (wg, wu, wd)
    return pl.pallas_call(
        functools.partial(_mlp_kernel, has_ctx=has_ctx),
        grid=(ne, bsz, fdim // fc),
        in_specs=in_specs,
        out_specs=out_specs,
        out_shape=out_shape,
        scratch_shapes=[pltpu.VMEM((cl + cc, d), BF16), pltpu.VMEM((cl + cc, d), F32)],
        compiler_params=_cparams(("arbitrary", "arbitrary", "arbitrary")),
        name="experts",
    )(*args)


def _combine_window(cap, tile):
    return min(tile, cap)


def _combine_kernel(offs_ref, ovf_ref, x1_ref, g2_ref, sl_ref, y_hbm, o_ref, ywin_ref, yext_ref,
                    sems, semx, *, cap, tile, n_tiles):
    b, t = pl.program_id(0), pl.program_id(1)
    n_steps = pl.num_programs(0) * n_tiles
    win = _combine_window(cap, tile)
    step = b * n_tiles + t
    buf = lax.rem(step, 2)

    def start_of(bb, tt, e):
        return _window_start(offs_ref[(bb * N_EXPERTS + e) * n_tiles + tt], cap, win)

    def window_copy(bb, tt, bf, e):
        return pltpu.make_async_copy(y_hbm.at[bb, e, pl.ds(start_of(bb, tt, e), win), :],
                                     ywin_ref.at[bf, pl.ds(e * win, win), :], sems.at[bf, e])

    @pl.when(step == 0)
    def _():
        for e in range(N_EXPERTS):
            window_copy(b, t, buf, e).start()

    @pl.when(step < n_steps - 1)
    def _():
        wrap = t == n_tiles - 1
        b_next = jnp.where(wrap, b + 1, b)
        t_next = jnp.where(wrap, 0, t + 1)
        for e in range(N_EXPERTS):
            window_copy(b_next, t_next, 1 - buf, e).start()

    sl = sl_ref[0] + 1
    eye = (lax.broadcasted_iota(I32, (tile, tile), 0)
           == lax.broadcasted_iota(I32, (tile, tile), 1)).astype(BF16)
    d_hi = lax.shift_right_logical(sl, 6).astype(F32).astype(BF16)
    d_lo = (sl & 63).astype(F32).astype(BF16)
    slot_t = (lax.dot_general(eye, d_hi, NT_DIMS, preferred_element_type=F32) * 64.0
              + lax.dot_general(eye, d_lo, NT_DIMS, preferred_element_type=F32)).astype(I32) - 1

    lane = lax.broadcasted_iota(I32, (tile, win), 1)
    hits = [(lane == slot_t[:, e:e + 1] - start_of(b, t, e)).astype(BF16) for e in range(N_EXPERTS)]
    onehot = jnp.concatenate(hits, axis=1)
    for e in range(N_EXPERTS):
        window_copy(b, t, buf, e).wait()
    acc = jnp.dot(onehot, ywin_ref[buf], preferred_element_type=F32)
    o_ref[0] = x1_ref[0] + g2_ref[0] * acc

    if win < cap:
        @pl.when(ovf_ref[step] > 0)
        def _():
            def ext_start(e):
                s0 = start_of(b, t, e)
                return s0, pl.multiple_of(jnp.minimum(s0 + win, cap - BF16_ROWS), BF16_ROWS)

            def ext_copy(e):
                return pltpu.make_async_copy(y_hbm.at[b, e, pl.ds(ext_start(e)[1], BF16_ROWS), :],
                                             yext_ref.at[pl.ds(e * BF16_ROWS, BF16_ROWS), :], semx.at[e])

            for e in range(N_EXPERTS):
                ext_copy(e).start()
            lane_x = lax.broadcasted_iota(I32, (tile, BF16_ROWS), 1)
            hits_x = []
            for e in range(N_EXPERTS):
                ext_copy(e).wait()
                s0, s1 = ext_start(e)
                row = s1 + lane_x
                hits_x.append(((row == slot_t[:, e:e + 1]) & (row >= s0 + win)).astype(BF16))
            extra = jnp.dot(jnp.concatenate(hits_x, axis=1), yext_ref[...], preferred_element_type=F32)
            o_ref[0] += g2_ref[0] * extra


def _combine(offs, ovf, x1, g2, slot, y, cap, tile):
    bsz, n, d = x1.shape
    n_tiles = n // tile
    win = _combine_window(cap, tile)
    kern = functools.partial(_combine_kernel, cap=cap, tile=tile, n_tiles=n_tiles)
    return pl.pallas_call(
        kern,
        grid_spec=pltpu.PrefetchScalarGridSpec(
            num_scalar_prefetch=2,
            grid=(bsz, n_tiles),
            in_specs=[
                pl.BlockSpec((1, tile, d), lambda b, t, offs, ovf: (b, t, 0)),
                pl.BlockSpec((1, 1, d), lambda b, t, offs, ovf: (b, 0, 0)),
                pl.BlockSpec((1, N_EXPERTS, tile), lambda b, t, offs, ovf: (b, 0, t)),
                pl.BlockSpec(memory_space=pl.ANY),
            ],
            out_specs=pl.BlockSpec((1, tile, d), lambda b, t, offs, ovf: (b, t, 0)),
            scratch_shapes=[pltpu.VMEM((2, N_EXPERTS * win, d), BF16),
                            pltpu.VMEM((N_EXPERTS * BF16_ROWS, d), BF16),
                            pltpu.SemaphoreType.DMA((2, N_EXPERTS)),
                            pltpu.SemaphoreType.DMA((N_EXPERTS,))],
        ),
        out_shape=jax.ShapeDtypeStruct((bsz, n, d), F32),
        compiler_params=_cparams(("arbitrary", "arbitrary")),
        name="combine",
    )(offs, ovf, x1, g2, slot, y)


def _deinterleave(w, n_heads):
    lead = w.shape[:-1]
    w = w.reshape(lead + (n_heads, HEAD_DIM // 2, 2))
    return jnp.swapaxes(w, -1, -2).reshape(lead + (n_heads * HEAD_DIM,))


def _rope_tables(n):
    rows = n // GRID_W
    row = jnp.repeat(jnp.arange(rows), GRID_W).astype(F32)
    col = jnp.tile(jnp.arange(GRID_W), rows).astype(F32)
    n_freq = HEAD_DIM // 4
    inv = ROPE_THETA ** (-jnp.arange(n_freq, dtype=F32) / n_freq)
    ang = jnp.concatenate([row[:, None] * inv, col[:, None] * inv], axis=-1)
    cos = jnp.tile(jnp.cos(ang), (1, 4))
    sin = jnp.sin(ang)
    sin_signed = jnp.tile(jnp.concatenate([-sin, sin], axis=-1), (1, 2))
    return cos, sin_signed


def _split_bf16(w):
    hi = w.astype(BF16)
    return hi, (w - hi.astype(F32)).astype(BF16)


def _route_select(aff, cap, tile):
    bsz, ne, n = aff.shape
    n_pad = max(n, HALO * LANES)
    a = aff if n_pad == n else jnp.pad(aff, ((0, 0), (0, 0), (0, n_pad - n)))
    slot4, offc4 = _select(a.reshape(bsz, ne, n_pad // LANES, LANES), cap)
    n_tiles = n // tile
    offs = offc4[:, :, ::tile // LANES, 0][:, :, :n_tiles]
    win = _combine_window(cap, tile)
    ends = jnp.concatenate([offs[:, :, 1:], jnp.full((bsz, ne, 1), cap, I32)], axis=2)
    starts = jnp.minimum((offs >> 4) << 4, cap - win)
    ovf = jnp.any(ends > starts + win, axis=1).astype(I32).reshape(-1)
    return slot4.reshape(bsz, ne, n_pad)[:, :, :n], offs.reshape(-1), ovf


def kernel(x, c, ctx, c_ctx, w_ada, b_ada, norm1_g, norm2_g, pool_w, pool_scale, attn_w_qkv, attn_w_o,
           attn_q_norm, attn_k_norm, attn_sink, router_w, exp_w_gate, exp_w_up, exp_w_down):
    bsz, n, d = x.shape
    lc = ctx.shape[1]
    cap = 2 * n // N_EXPERTS
    cap_c = 2 * lc // N_EXPERTS
    tile_c = min(TOK_TILE, lc)

    cc = jnp.zeros((8, d), F32).at[:bsz].set(c).at[bsz].set(c_ctx)
    mods = _ada(cc, w_ada, b_ada)

    cos, sin = _rope_tables(n)
    cos_c = jnp.ones((lc, LANES), F32)
    sin_c = jnp.zeros((lc, LANES), F32)
    blk = jnp.arange(256) // HEAD_DIM
    bavg = ((blk[:, None] == blk[None, :]).astype(F32) / HEAD_DIM).astype(BF16)

    for i in range(DEPTH):
        update_ctx = i < DEPTH - 1
        j = i // 2
        is_pool = (i % 2) == 0
        m = [mods[i, :, k * d:(k + 1) * d] for k in range(6)]
        sh1, sc1, g1, sh2, sc2, g2 = [v[:bsz, None, :] for v in m]
        csh1, csc1, cg1, csh2, csc2, cg2 = [jnp.broadcast_to(v[bsz][None, None, :], (bsz, 1, d)) for v in m]
        n1g = norm1_g[i][None, :]
        n2g = norm2_g[i][None, :]
        wrh, wrl = _split_bf16(router_w[i].T)

        if is_pool:
            pw = pool_w[j].astype(BF16)
            ps = pool_scale[j][None, :]
            x1, h2, aff = _pool_layer(x, n1g, sh1, sc1, g1, pw, ps, n2g, sh2, sc2, wrh, wrl, 512)
            if update_ctx:
                c1, hc2, affc = _pool_layer(ctx, n1g, csh1, csc1, cg1, pw, ps, n2g, csh2, csc2, wrh, wrl, lc)
        else:
            nq = N_Q_HEADS * HEAD_DIM
            nk = N_KV_HEADS * HEAD_DIM
            wqkv = attn_w_qkv[j]
            wperm = jnp.concatenate([_deinterleave(wqkv[:, :nq], N_Q_HEADS),
                                     _deinterleave(wqkv[:, nq:nq + nk], N_KV_HEADS),
                                     wqkv[:, nq + nk:]], axis=1).astype(BF16)
            qg = jnp.tile(_deinterleave(attn_q_norm[j], 1), 4)[None, :]
            kg = jnp.tile(_deinterleave(attn_k_norm[j], 1), 4)[None, :]
            wo = attn_w_o[j].astype(BF16)
            sink = attn_sink[j]
            q, k, v = _qkv_layer(x, n1g, sh1, sc1, wperm, bavg, qg, kg, cos, sin, 512)
            qc, kc, vc = _qkv_layer(ctx, n1g, csh1, csc1, wperm, bavg, qg, kg, cos_c, sin_c, lc)
            o = _attn_layer(sink, q, k, v, kc, vc, True)
            x1, h2, aff = _oproj_layer(o, x, wo, g1, n2g, sh2, sc2, wrh, wrl, 512)
            if update_ctx:
                oc = _attn_layer(sink, qc, None, None, kc, vc, False)
                c1, hc2, affc = _oproj_layer(oc, ctx, wo, cg1, n2g, csh2, csc2, wrh, wrl, lc)

        slot, offs, ovf = _route_select(aff, cap, TOK_TILE)
        xs, vals = _gather(offs, h2, slot, aff, cap, TOK_TILE)
        if update_ctx:
            slot_c, offs_c, ovf_c = _route_select(affc, cap_c, tile_c)
            xs_c, vals_c = _gather(offs_c, hc2, slot_c, affc, cap_c, tile_c)
            y, y_c = _mlp(i, xs, vals, xs_c, vals_c, exp_w_gate, exp_w_up, exp_w_down)
            ctx = _combine(offs_c, ovf_c, c1, cg2, slot_c, y_c, cap_c, tile_c)
        else:
            (y,) = _mlp(i, xs, vals, None, None, exp_w_gate, exp_w_up, exp_w_down)
        x = _combine(offs, ovf, x1, g2, slot, y, cap, TOK_TILE)
    return x
```

```python
import functools

import jax
import jax.numpy as jnp
from jax import lax
from jax.experimental import pallas as pl
from jax.experimental.pallas import tpu as pltpu

F32 = jnp.float32
BF16 = jnp.bfloat16
I32 = jnp.int32

D_MODEL = 1024
DEPTH = 4
GRID_W = 64
POOL_WINDOWS = (2, 4, 8, 16)
POOL_GROUP_DIM = 256
HEAD_DIM = 64
N_Q_HEADS = 16
N_KV_HEADS = 4
Q_PER_KV = 4
WINDOW = 128
BLOCK_Q = 128
ROPE_THETA = 10000.0
N_EXPERTS = 16
D_EXPERT = 2048
NORM_EPS = 1e-6
NEG_INF = -1e30

LANES = 128
BF16_ROWS = 16
HALO = 8
TOK_TILE = 256
VMEM_LIMIT = 56 * 1024 * 1024

NT_DIMS = (((1,), (1,)), ((), ()))


def _cparams(sem, vmem=None):
    return pltpu.CompilerParams(dimension_semantics=sem, vmem_limit_bytes=vmem or VMEM_LIMIT)


def _norm_mod(x, g, shift, scale):
    ms = jnp.mean(x * x, axis=-1, keepdims=True)
    return (x * lax.rsqrt(ms + NORM_EPS)) * g * (1.0 + scale) + shift


def _ada_kernel(cc_ref, w_ref, b_ref, o_ref):
    cc = cc_ref[...]
    s = cc * jax.nn.sigmoid(cc)
    o_ref[0] = jnp.dot(s, w_ref[0], preferred_element_type=F32,
                       precision=lax.Precision.HIGHEST) + b_ref[0]


def _ada(cc, w_ada, b_ada):
    depth, d, six_d = w_ada.shape
    tn = 1536
    return pl.pallas_call(
        _ada_kernel,
        grid=(depth, six_d // tn),
        in_specs=[
            pl.BlockSpec((8, d), lambda i, j: (0, 0)),
            pl.BlockSpec((1, d, tn), lambda i, j: (i, 0, j)),
            pl.BlockSpec((1, 1, tn), lambda i, j: (i, 0, j)),
        ],
        out_specs=pl.BlockSpec((1, 8, tn), lambda i, j: (i, 0, j)),
        out_shape=jax.ShapeDtypeStruct((depth, 8, six_d), F32),
        compiler_params=_cparams(("arbitrary", "arbitrary")),
        name="ada",
    )(cc, w_ada, b_ada.reshape(depth, 1, six_d))


def _route(x1, n2g, sh2, sc2, wr_hi, wr_lo):
    h2 = _norm_mod(x1, n2g, sh2, sc2)
    h_hi = h2.astype(BF16)
    h_lo = (h2 - h_hi.astype(F32)).astype(BF16)
    lg = (lax.dot_general(wr_hi, h_hi, NT_DIMS, preferred_element_type=F32)
          + lax.dot_general(wr_hi, h_lo, NT_DIMS, preferred_element_type=F32)
          + lax.dot_general(wr_lo, h_hi, NT_DIMS, preferred_element_type=F32))
    m = jnp.max(lg, axis=0, keepdims=True)
    ex = jnp.exp(lg - m)
    aff = ex / jnp.sum(ex, axis=0, keepdims=True)
    return h_hi, aff


def _pool_kernel(x_ref, xp_ref, xn_ref, n1g_ref, sh1_ref, sc1_ref, g1_ref, pw_ref, ps_ref,
                 n2g_ref, sh2_ref, sc2_ref, wrh_ref, wrl_ref,
                 x1_ref, h2_ref, aff_ref, hs_ref, *, n_tokens):
    t = pl.program_id(1)
    nt = pl.num_programs(1)
    tt = x_ref.shape[1]
    n1g, sh1, sc1 = n1g_ref[...], sh1_ref[0], sc1_ref[0]
    x = x_ref[0]
    h = _norm_mod(x, n1g, sh1, sc1)
    hp = _norm_mod(xp_ref[0], n1g, sh1, sc1) * (t > 0).astype(F32)
    hn = _norm_mod(xn_ref[0], n1g, sh1, sc1) * (t < nt - 1).astype(F32)
    hs_ref[0:HALO, :] = hp
    hs_ref[HALO:HALO + tt, :] = h
    hs_ref[HALO + tt:HALO + tt + HALO, :] = hn
    pos = t * tt + lax.broadcasted_iota(I32, (tt, 1), 0)
    ys = []
    for g, w in enumerate(POOL_WINDOWS):
        half = w // 2
        c0 = g * POOL_GROUP_DIM
        s = hs_ref[HALO - half:HALO - half + tt, c0:c0 + POOL_GROUP_DIM]
        for k in range(1, w):
            s = s + hs_ref[HALO - half + k:HALO - half + k + tt, c0:c0 + POOL_GROUP_DIM]
        lo = jnp.clip(pos - half, 0, n_tokens)
        hi = jnp.clip(pos + half, 0, n_tokens)
        p = s / (hi - lo).astype(F32) - h[:, c0:c0 + POOL_GROUP_DIM]
        ys.append(jnp.dot(p.astype(BF16), pw_ref[g], preferred_element_type=F32))
    y = jnp.concatenate(ys, axis=1) * ps_ref[...]
    x1 = x + g1_ref[0] * y
    x1_ref[0] = x1
    h2, aff = _route(x1, n2g_ref[...], sh2_ref[0], sc2_ref[0], wrh_ref[...], wrl_ref[...])
    h2_ref[0] = h2
    aff_ref[0] = aff


def _row_spec(d):
    return pl.BlockSpec((1, d), lambda b, t: (0, 0))


def _mod_spec(d):
    return pl.BlockSpec((1, 1, d), lambda b, t: (b, 0, 0))


def _pool_layer(x, n1g, sh1, sc1, g1, pw, ps, n2g, sh2, sc2, wrh, wrl, tt):
    bsz, n, d = x.shape
    nb8 = n // HALO
    r = tt // HALO
    kern = functools.partial(_pool_kernel, n_tokens=n)
    return pl.pallas_call(
        kern,
        grid=(bsz, n // tt),
        in_specs=[
            pl.BlockSpec((1, tt, d), lambda b, t: (b, t, 0)),
            pl.BlockSpec((1, HALO, d), lambda b, t: (b, jnp.maximum(t * r - 1, 0), 0)),
            pl.BlockSpec((1, HALO, d), lambda b, t: (b, jnp.minimum((t + 1) * r, nb8 - 1), 0)),
            _row_spec(d), _mod_spec(d), _mod_spec(d), _mod_spec(d),
            pl.BlockSpec((4, POOL_GROUP_DIM, POOL_GROUP_DIM), lambda b, t: (0, 0, 0)),
            _row_spec(d),
            _row_spec(d), _mod_spec(d), _mod_spec(d),
            pl.BlockSpec((N_EXPERTS, d), lambda b, t: (0, 0)),
            pl.BlockSpec((N_EXPERTS, d), lambda b, t: (0, 0)),
        ],
        out_specs=[
            pl.BlockSpec((1, tt, d), lambda b, t: (b, t, 0)),
            pl.BlockSpec((1, tt, d), lambda b, t: (b, t, 0)),
            pl.BlockSpec((1, N_EXPERTS, tt), lambda b, t: (b, 0, t)),
        ],
        out_shape=[
            jax.ShapeDtypeStruct((bsz, n, d), F32),
            jax.ShapeDtypeStruct((bsz, n, d), BF16),
            jax.ShapeDtypeStruct((bsz, N_EXPERTS, n), F32),
        ],
        scratch_shapes=[pltpu.VMEM((tt + 2 * HALO, d), F32)],
        compiler_params=_cparams(("arbitrary", "arbitrary")),
        name="pool_route",
    )(x, x, x, n1g, sh1, sc1, g1, pw, ps, n2g, sh2, sc2, wrh, wrl)


def _head_rms(v, bavg, gain):
    sq = v * v
    sq_hi = sq.astype(BF16)
    sq_lo = (sq - sq_hi.astype(F32)).astype(BF16)
    ms = (jnp.dot(sq_hi, bavg, preferred_element_type=F32)
          + jnp.dot(sq_lo, bavg, preferred_element_type=F32))
    return v * lax.rsqrt(ms + NORM_EPS) * gain


def _rope(v, cos, sin_signed):
    lane = lax.broadcasted_iota(I32, v.shape, 1)
    fwd = pltpu.roll(v, 32, 1)
    bwd = pltpu.roll(v, 96, 1)
    partner = jnp.where((lane % HEAD_DIM) < HEAD_DIM // 2, bwd, fwd)
    return v * cos + partner * sin_signed


def _qkv_kernel(x_ref, n1g_ref, sh1_ref, sc1_ref, w_ref, bavg_ref, qg_ref, kg_ref,
                cos_ref, sin_ref, q_ref, k_ref, v_ref):
    h = _norm_mod(x_ref[0], n1g_ref[...], sh1_ref[0], sc1_ref[0]).astype(BF16)
    qkv = jnp.dot(h, w_ref[...], preferred_element_type=F32)
    nq = N_Q_HEADS * HEAD_DIM
    nk = N_KV_HEADS * HEAD_DIM
    bavg = bavg_ref[...]
    cos, sin = cos_ref[...], sin_ref[...]
    scale = HEAD_DIM ** -0.5
    for c in range(nq // 256):
        qn = _head_rms(qkv[:, c * 256:(c + 1) * 256], bavg, qg_ref[...])
        for s in range(2):
            qr = _rope(qn[:, s * LANES:(s + 1) * LANES], cos, sin) * scale
            for hh in range(2):
                head = c * 4 + s * 2 + hh
                q_ref[0, head] = qr[:, hh * HEAD_DIM:(hh + 1) * HEAD_DIM].astype(BF16)
    kn = _head_rms(qkv[:, nq:nq + nk], bavg, kg_ref[...])
    v = qkv[:, nq + nk:]
    for s in range(2):
        kr = _rope(kn[:, s * LANES:(s + 1) * LANES], cos, sin)
        for hh in range(2):
            head = s * 2 + hh
            k_ref[0, head] = kr[:, hh * HEAD_DIM:(hh + 1) * HEAD_DIM].astype(BF16)
            v_ref[0, head] = v[:, head * HEAD_DIM:(head + 1) * HEAD_DIM].astype(BF16)


def _qkv_layer(x, n1g, sh1, sc1, w, bavg, qg, kg, cos, sin, tt):
    bsz, n, d = x.shape
    return pl.pallas_call(
        _qkv_kernel,
        grid=(bsz, n // tt),
        in_specs=[
            pl.BlockSpec((1, tt, d), lambda b, t: (b, t, 0)),
            _row_spec(d), _mod_spec(d), _mod_spec(d),
            pl.BlockSpec(w.shape, lambda b, t: (0, 0)),
            pl.BlockSpec((256, 256), lambda b, t: (0, 0)),
            _row_spec(256), _row_spec(256),
            pl.BlockSpec((tt, LANES), lambda b, t: (t, 0)),
            pl.BlockSpec((tt, LANES), lambda b, t: (t, 0)),
        ],
        out_specs=[
            pl.BlockSpec((1, N_Q_HEADS, tt, HEAD_DIM), lambda b, t: (b, 0, t, 0)),
            pl.BlockSpec((1, N_KV_HEADS, tt, HEAD_DIM), lambda b, t: (b, 0, t, 0)),
            pl.BlockSpec((1, N_KV_HEADS, tt, HEAD_DIM), lambda b, t: (b, 0, t, 0)),
        ],
        out_shape=[
            jax.ShapeDtypeStruct((bsz, N_Q_HEADS, n, HEAD_DIM), BF16),
            jax.ShapeDtypeStruct((bsz, N_KV_HEADS, n, HEAD_DIM), BF16),
            jax.ShapeDtypeStruct((bsz, N_KV_HEADS, n, HEAD_DIM), BF16),
        ],
        compiler_params=_cparams(("arbitrary", "arbitrary")),
        name="qkv",
    )(x, n1g, sh1, sc1, w, bavg, qg, kg, cos, sin)


def _attn_kernel(sink_ref, q_ref, *refs, band):
    if band:
        kp_ref, kc_ref, kn_ref, vp_ref, vc_ref, vn_ref, kx_ref, vx_ref, bias_ref, o_ref = refs
    else:
        kx_ref, vx_ref, o_ref = refs
    tq = q_ref.shape[2]
    for g in range(N_KV_HEADS):
        if band:
            kg = jnp.concatenate([kp_ref[0, g], kc_ref[0, g], kn_ref[0, g], kx_ref[0, g]], axis=0)
            vg = jnp.concatenate([vp_ref[0, g], vc_ref[0, g], vn_ref[0, g], vx_ref[0, g]], axis=0)
        else:
            kg = kx_ref[0, g]
            vg = vx_ref[0, g]
        vext = jnp.concatenate([vg, jnp.ones_like(vg)], axis=1)
        qg = q_ref[0, g * Q_PER_KV:(g + 1) * Q_PER_KV].reshape(Q_PER_KV * tq, HEAD_DIM)
        s = lax.dot_general(qg, kg, NT_DIMS, preferred_element_type=F32)
        if band:
            s = s + bias_ref[0]
        sk = jnp.concatenate([jnp.full((tq, 1), sink_ref[g * Q_PER_KV + hh], F32)
                              for hh in range(Q_PER_KV)], axis=0)
        m = jnp.maximum(jnp.max(s, axis=-1, keepdims=True), sk)
        p = jnp.exp(s - m).astype(BF16)
        oe = jnp.dot(p, vext, preferred_element_type=F32)
        den = oe[:, HEAD_DIM:HEAD_DIM + 1] + jnp.exp(sk - m)
        o = (oe[:, :HEAD_DIM] / den).astype(o_ref.dtype)
        for hh in range(Q_PER_KV):
            head = g * Q_PER_KV + hh
            o_ref[0, :, head * HEAD_DIM:(head + 1) * HEAD_DIM] = o[hh * tq:(hh + 1) * tq]


def _band_bias(lc):
    nband = BLOCK_Q + 2 * WINDOW
    row = jnp.arange(Q_PER_KV * BLOCK_Q)[:, None] % BLOCK_Q
    col = jnp.arange(nband + lc)[None, :]
    dlt = col - row
    inband = (dlt >= 0) & (dlt <= 2 * WINDOW)
    variants = [inband & (col >= WINDOW), inband, inband & (col < WINDOW + BLOCK_Q)]
    return jnp.stack([jnp.where(v | (col >= nband), 0.0, NEG_INF) for v in variants]).astype(F32)


def _attn_layer(sink, q, k, v, kx, vx, band):
    bsz, _, n, _ = q.shape
    lc = kx.shape[2]
    tq = BLOCK_Q if band else n
    nb = n // tq
    qspec = pl.BlockSpec((1, N_Q_HEADS, tq, HEAD_DIM), lambda b, t: (b, 0, t, 0))
    xspec = pl.BlockSpec((1, N_KV_HEADS, lc, HEAD_DIM), lambda b, t: (b, 0, 0, 0))
    if band:
        prev = pl.BlockSpec((1, N_KV_HEADS, tq, HEAD_DIM), lambda b, t: (b, 0, jnp.maximum(t - 1, 0), 0))
        cur = pl.BlockSpec((1, N_KV_HEADS, tq, HEAD_DIM), lambda b, t: (b, 0, t, 0))
        nxt = pl.BlockSpec((1, N_KV_HEADS, tq, HEAD_DIM), lambda b, t: (b, 0, jnp.minimum(t + 1, nb - 1), 0))
        bias = _band_bias(lc)
        bspec = pl.BlockSpec((1,) + bias.shape[1:],
                             lambda b, t: (jnp.where(t == 0, 0, jnp.where(t == nb - 1, 2, 1)), 0, 0))
        in_specs = [qspec, prev, cur, nxt, prev, cur, nxt, xspec, xspec, bspec]
        args = (q, k, k, k, v, v, v, kx, vx, bias)
    else:
        in_specs = [qspec, xspec, xspec]
        args = (q, kx, vx)
    return pl.pallas_call(
        functools.partial(_attn_kernel, band=band),
        grid=(bsz, nb),
        in_specs=[pl.BlockSpec(memory_space=pltpu.SMEM)] + in_specs,
        out_specs=pl.BlockSpec((1, tq, N_Q_HEADS * HEAD_DIM), lambda b, t: (b, t, 0)),
        out_shape=jax.ShapeDtypeStruct((bsz, n, N_Q_HEADS * HEAD_DIM), BF16),
        compiler_params=_cparams(("arbitrary", "arbitrary")),
        name="attn_band" if band else "attn_ctx",
    )(sink, *args)


def _oproj_kernel(o_ref, x_ref, wo_ref, g1_ref, n2g_ref, sh2_ref, sc2_ref, wrh_ref, wrl_ref,
                  x1_ref, h2_ref, aff_ref):
    y = jnp.dot(o_ref[0], wo_ref[...], preferred_element_type=F32)
    x1 = x_ref[0] + g1_ref[0] * y
    x1_ref[0] = x1
    h2, aff = _route(x1, n2g_ref[...], sh2_ref[0], sc2_ref[0], wrh_ref[...], wrl_ref[...])
    h2_ref[0] = h2
    aff_ref[0] = aff


def _oproj_layer(o, x, wo, g1, n2g, sh2, sc2, wrh, wrl, tt):
    bsz, n, d = x.shape
    return pl.pallas_call(
        _oproj_kernel,
        grid=(bsz, n // tt),
        in_specs=[
            pl.BlockSpec((1, tt, d), lambda b, t: (b, t, 0)),
            pl.BlockSpec((1, tt, d), lambda b, t: (b, t, 0)),
            pl.BlockSpec((d, d), lambda b, t: (0, 0)),
            _mod_spec(d), _row_spec(d), _mod_spec(d), _mod_spec(d),
            pl.BlockSpec((N_EXPERTS, d), lambda b, t: (0, 0)),
            pl.BlockSpec((N_EXPERTS, d), lambda b, t: (0, 0)),
        ],
        out_specs=[
            pl.BlockSpec((1, tt, d), lambda b, t: (b, t, 0)),
            pl.BlockSpec((1, tt, d), lambda b, t: (b, t, 0)),
            pl.BlockSpec((1, N_EXPERTS, tt), lambda b, t: (b, 0, t)),
        ],
        out_shape=[
            jax.ShapeDtypeStruct((bsz, n, d), F32),
            jax.ShapeDtypeStruct((bsz, n, d), BF16),
            jax.ShapeDtypeStruct((bsz, N_EXPERTS, n), F32),
        ],
        compiler_params=_cparams(("arbitrary", "arbitrary")),
        name="oproj_route",
    )(o, x, wo, g1, n2g, sh2, sc2, wrh, wrl)


def _prefix_counts(mask, n_chunks):
    ne = mask.shape[0]
    m2 = mask.reshape(ne * n_chunks, LANES).astype(BF16)
    ri = lax.broadcasted_iota(I32, (LANES, LANES), 0)
    ci = lax.broadcasted_iota(I32, (LANES, LANES), 1)
    upper = (ri <= ci).astype(BF16)
    within = jnp.dot(m2, upper, preferred_element_type=F32).reshape(ne, n_chunks, LANES)
    tot = jnp.dot(m2, jnp.ones((LANES, LANES), BF16), preferred_element_type=F32).reshape(ne, n_chunks, LANES)
    rc = lax.broadcasted_iota(I32, (n_chunks, n_chunks), 0)
    cc = lax.broadcasted_iota(I32, (n_chunks, n_chunks), 1)
    strict_lower = (cc < rc).astype(F32)
    off = jnp.stack([jnp.dot(strict_lower, tot[e], preferred_element_type=F32) for e in range(ne)])
    return within + off, off


def _select_kernel(a_ref, slot_ref, offc_ref, *, cap):
    a = a_ref[0]
    ne, n_chunks, _ = a.shape
    bits = lax.bitcast_convert_type(a, I32)

    def count(pred):
        c = jnp.sum(pred.astype(F32), axis=1, keepdims=True)
        return jnp.sum(c, axis=2, keepdims=True)

    def body(i, thr):
        cand = thr | lax.shift_left(jnp.int32(1), 30 - i)
        return jnp.where(count(bits >= cand) >= cap, cand, thr)

    thr = lax.fori_loop(0, 31, body, jnp.zeros((ne, 1, 1), I32))
    gt = bits > thr
    eq = bits == thr
    need = cap - count(gt)
    eq_rank, _ = _prefix_counts(eq.astype(F32), n_chunks)
    sel = gt | (eq & (eq_rank <= need))
    sel_f = sel.astype(F32)
    incl, off = _prefix_counts(sel_f, n_chunks)
    slot_ref[0] = jnp.where(sel, (incl - 1.0).astype(I32), -1)
    offc_ref[0] = off.astype(I32)


def _select(aff4, cap):
    bsz, ne, n_chunks, cw = aff4.shape
    spec = pl.BlockSpec((1, ne, n_chunks, cw), lambda b: (b, 0, 0, 0))
    return pl.pallas_call(
        functools.partial(_select_kernel, cap=cap),
        grid=(bsz,),
        in_specs=[spec],
        out_specs=[spec, spec],
        out_shape=[jax.ShapeDtypeStruct(aff4.shape, I32), jax.ShapeDtypeStruct(aff4.shape, I32)],
        compiler_params=_cparams(("arbitrary",)),
        name="select",
    )(aff4)


def _window(cap, tile):
    return min(tile + BF16_ROWS, cap)


def _window_start(off, cap, win):
    start = lax.shift_left(lax.shift_right_logical(off, 4), 4)
    return pl.multiple_of(jnp.minimum(start, cap - win), BF16_ROWS)


def _gather_kernel(offs_ref, h2_ref, sl_ref, af_ref, xs_ref, val_ref, acc_ref, vacc_ref,
                   *, cap, tile, n_tiles):
    b, e, s = pl.program_id(0), pl.program_id(1), pl.program_id(2)
    ns = pl.num_programs(2)
    sub = h2_ref.shape[1] // tile
    win = _window(cap, tile)

    @pl.when(s == 0)
    def _():
        acc_ref[...] = jnp.zeros_like(acc_ref)
        vacc_ref[...] = jnp.zeros_like(vacc_ref)

    for u in range(sub):
        off = offs_ref[(b * N_EXPERTS + e) * n_tiles + s * sub + u]
        start = _window_start(off, cap, win)
        slm = sl_ref[0, 0, :, u * tile:(u + 1) * tile]
        afr = af_ref[0, 0, :, u * tile:(u + 1) * tile]
        rows = start + lax.broadcasted_iota(I32, (win, tile), 0)
        hit = rows == slm
        res = jnp.dot(hit.astype(BF16), h2_ref[0, u * tile:(u + 1) * tile, :],
                      preferred_element_type=F32)
        acc_ref[pl.ds(start, win), :] += res
        vacc_ref[pl.ds(start, win), :] += jnp.sum(jnp.where(hit, afr, 0.0), axis=1, keepdims=True)

    @pl.when(s == ns - 1)
    def _():
        xs_ref[0, 0] = acc_ref[...].astype(BF16)
        val_ref[0, 0] = vacc_ref[...]


def _gather(offs, h2, slot, aff, cap, tile):
    bsz, n, d = h2.shape
    tg = min(n, 2048)
    n_tiles = n // tile
    kern = functools.partial(_gather_kernel, cap=cap, tile=tile, n_tiles=n_tiles)
    return pl.pallas_call(
        kern,
        grid_spec=pltpu.PrefetchScalarGridSpec(
            num_scalar_prefetch=1,
            grid=(bsz, N_EXPERTS, n // tg),
            in_specs=[
                pl.BlockSpec((1, tg, d), lambda b, e, s, offs: (b, s, 0)),
                pl.BlockSpec((1, 1, 1, tg), lambda b, e, s, offs: (b, e, 0, s)),
                pl.BlockSpec((1, 1, 1, tg), lambda b, e, s, offs: (b, e, 0, s)),
            ],
            out_specs=[
                pl.BlockSpec((1, 1, cap, d), lambda b, e, s, offs: (b, e, 0, 0)),
                pl.BlockSpec((1, 1, cap, 1), lambda b, e, s, offs: (b, e, 0, 0)),
            ],
            scratch_shapes=[pltpu.VMEM((cap, d), F32), pltpu.VMEM((cap, 1), F32)],
        ),
        out_shape=[
            jax.ShapeDtypeStruct((bsz, N_EXPERTS, cap, d), BF16),
            jax.ShapeDtypeStruct((bsz, N_EXPERTS, cap, 1), F32),
        ],
        compiler_params=_cparams(("arbitrary", "arbitrary", "arbitrary")),
        name="gather",
    )(offs, h2, slot.reshape(bsz, N_EXPERTS, 1, n), aff.reshape(bsz, N_EXPERTS, 1, n))


def _up_kernel(*refs, has_ctx):
    if has_ctx:
        xl_ref, xc_ref, wg_ref, wu_ref, h_ref, xs_ref = refs
        cl = xl_ref.shape[2]

        @pl.when(pl.program_id(2) == 0)
        def _():
            xs_ref[0:cl, :] = xl_ref[0, 0]
            xs_ref[cl:, :] = xc_ref[0, 0]

        xs = xs_ref[...]
    else:
        xl_ref, wg_ref, wu_ref, h_ref = refs
        xs = xl_ref[0, 0]
    a = jnp.dot(xs, wg_ref[0, 0].astype(BF16), preferred_element_type=F32)
    u = jnp.dot(xs, wu_ref[0, 0].astype(BF16), preferred_element_type=F32)
    h_ref[0, 0] = (a * jax.nn.sigmoid(a) * u).astype(BF16)


def _down_kernel(*refs, has_ctx):
    if has_ctx:
        h_ref, vl_ref, vc_ref, wd_ref, y_ref = refs
    else:
        h_ref, vl_ref, wd_ref, y_ref = refs
    cl = vl_ref.shape[2]
    res = jnp.dot(h_ref[0, 0], wd_ref[0, 0].astype(BF16), preferred_element_type=F32)
    y_ref[0, 0, 0:cl, :] = (res[0:cl] * vl_ref[0, 0]).astype(BF16)
    if has_ctx:
        y_ref[0, 0, cl:, :] = (res[cl:] * vc_ref[0, 0]).astype(BF16)


def _mlp(layer, xl, vl, xc, vc, wg, wu, wd):
    bsz, ne, cl, d = xl.shape
    fdim = wg.shape[-1]
    has_ctx = xc is not None
    cc = xc.shape[2] if has_ctx else 0
    rows = cl + cc
    fc = 512
    dn = 512

    xspecs = [pl.BlockSpec((1, 1, cl, d), lambda e, b, f: (b, e, 0, 0))]
    if has_ctx:
        xspecs.append(pl.BlockSpec((1, 1, cc, d), lambda e, b, f: (b, e, 0, 0)))
    hid = pl.pallas_call(
        functools.partial(_up_kernel, has_ctx=has_ctx),
        grid=(ne, bsz, fdim // fc),
        in_specs=xspecs + [
            pl.BlockSpec((1, 1, d, fc), lambda e, b, f: (layer, e, 0, f)),
            pl.BlockSpec((1, 1, d, fc), lambda e, b, f: (layer, e, 0, f)),
        ],
        out_specs=pl.BlockSpec((1, 1, rows, fc), lambda e, b, f: (b, e, 0, f)),
        out_shape=jax.ShapeDtypeStruct((bsz, ne, rows, fdim), BF16),
        scratch_shapes=[pltpu.VMEM((rows, d), BF16)] if has_ctx else [],
        compiler_params=_cparams(("arbitrary", "arbitrary", "arbitrary")),
        name="experts_up",
    )(*((xl, xc) if has_ctx else (xl,)), wg, wu)

    vspecs = [pl.BlockSpec((1, 1, cl, 1), lambda e, b, j: (b, e, 0, 0))]
    if has_ctx:
        vspecs.append(pl.BlockSpec((1, 1, cc, 1), lambda e, b, j: (b, e, 0, 0)))
    return pl.pallas_call(
        functools.partial(_down_kernel, has_ctx=has_ctx),
        grid=(ne, bsz, d // dn),
        in_specs=[pl.BlockSpec((1, 1, rows, fdim), lambda e, b, j: (b, e, 0, 0))] + vspecs + [
            pl.BlockSpec((1, 1, fdim, dn), lambda e, b, j: (layer, e, 0, j)),
        ],
        out_specs=pl.BlockSpec((1, 1, rows, dn), lambda e, b, j: (b, e, 0, j)),
        out_shape=jax.ShapeDtypeStruct((bsz, ne, rows, d), BF16),
        compiler_params=_cparams(("arbitrary", "arbitrary", "arbitrary")),
        name="experts_down",
    )(hid, *((vl, vc) if has_ctx else (vl,)), wd)


def _combine_window(cap, tile):
    return min(tile, cap)


def _combine_kernel(offs_ref, ovf_ref, x1_ref, g2_ref, sl_ref, y_hbm, o_ref, ywin_ref, yext_ref,
                    sems, semx, *, cap, tile, n_tiles, base):
    b, t = pl.program_id(0), pl.program_id(1)
    n_steps = pl.num_programs(0) * n_tiles
    win = _combine_window(cap, tile)
    step = b * n_tiles + t
    buf = lax.rem(step, 2)

    def start_of(bb, tt, e):
        return _window_start(offs_ref[(bb * N_EXPERTS + e) * n_tiles + tt], cap, win)

    def window_copy(bb, tt, bf, e):
        src = pl.multiple_of(base + start_of(bb, tt, e), BF16_ROWS)
        return pltpu.make_async_copy(y_hbm.at[bb, e, pl.ds(src, win), :],
                                     ywin_ref.at[bf, pl.ds(e * win, win), :], sems.at[bf, e])

    @pl.when(step == 0)
    def _():
        for e in range(N_EXPERTS):
            window_copy(b, t, buf, e).start()

    @pl.when(step < n_steps - 1)
    def _():
        wrap = t == n_tiles - 1
        b_next = jnp.where(wrap, b + 1, b)
        t_next = jnp.where(wrap, 0, t + 1)
        for e in range(N_EXPERTS):
            window_copy(b_next, t_next, 1 - buf, e).start()

    sl = sl_ref[0] + 1
    eye = (lax.broadcasted_iota(I32, (tile, tile), 0)
           == lax.broadcasted_iota(I32, (tile, tile), 1)).astype(BF16)
    d_hi = lax.shift_right_logical(sl, 6).astype(F32).astype(BF16)
    d_lo = (sl & 63).astype(F32).astype(BF16)
    slot_t = (lax.dot_general(eye, d_hi, NT_DIMS, preferred_element_type=F32) * 64.0
              + lax.dot_general(eye, d_lo, NT_DIMS, preferred_element_type=F32)).astype(I32) - 1

    lane = lax.broadcasted_iota(I32, (tile, win), 1)
    hits = [(lane == slot_t[:, e:e + 1] - start_of(b, t, e)).astype(BF16) for e in range(N_EXPERTS)]
    onehot = jnp.concatenate(hits, axis=1)
    for e in range(N_EXPERTS):
        window_copy(b, t, buf, e).wait()
    acc = jnp.dot(onehot, ywin_ref[buf], preferred_element_type=F32)
    o_ref[0] = x1_ref[0] + g2_ref[0] * acc

    if win < cap:
        @pl.when(ovf_ref[step] > 0)
        def _():
            def ext_start(e):
                s0 = start_of(b, t, e)
                return s0, pl.multiple_of(jnp.minimum(s0 + win, cap - BF16_ROWS), BF16_ROWS)

            def ext_copy(e):
                src = pl.multiple_of(base + ext_start(e)[1], BF16_ROWS)
                return pltpu.make_async_copy(y_hbm.at[b, e, pl.ds(src, BF16_ROWS), :],
                                             yext_ref.at[pl.ds(e * BF16_ROWS, BF16_ROWS), :], semx.at[e])

            for e in range(N_EXPERTS):
                ext_copy(e).start()
            lane_x = lax.broadcasted_iota(I32, (tile, BF16_ROWS), 1)
            hits_x = []
            for e in range(N_EXPERTS):
                ext_copy(e).wait()
                s0, s1 = ext_start(e)
                row = s1 + lane_x
                hits_x.append(((row == slot_t[:, e:e + 1]) & (row >= s0 + win)).astype(BF16))
            extra = jnp.dot(jnp.concatenate(hits_x, axis=1), yext_ref[...], preferred_element_type=F32)
            o_ref[0] += g2_ref[0] * extra


def _combine(offs, ovf, x1, g2, slot, y, cap, tile, base):
    bsz, n, d = x1.shape
    n_tiles = n // tile
    win = _combine_window(cap, tile)
    kern = functools.partial(_combine_kernel, cap=cap, tile=tile, n_tiles=n_tiles, base=base)
    return pl.pallas_call(
        kern,
        grid_spec=pltpu.PrefetchScalarGridSpec(
            num_scalar_prefetch=2,
            grid=(bsz, n_tiles),
            in_specs=[
                pl.BlockSpec((1, tile, d), lambda b, t, offs, ovf: (b, t, 0)),
                pl.BlockSpec((1, 1, d), lambda b, t, offs, ovf: (b, 0, 0)),
                pl.BlockSpec((1, N_EXPERTS, tile), lambda b, t, offs, ovf: (b, 0, t)),
                pl.BlockSpec(memory_space=pl.ANY),
            ],
            out_specs=pl.BlockSpec((1, tile, d), lambda b, t, offs, ovf: (b, t, 0)),
            scratch_shapes=[pltpu.VMEM((2, N_EXPERTS * win, d), BF16),
                            pltpu.VMEM((N_EXPERTS * BF16_ROWS, d), BF16),
                            pltpu.SemaphoreType.DMA((2, N_EXPERTS)),
                            pltpu.SemaphoreType.DMA((N_EXPERTS,))],
        ),
        out_shape=jax.ShapeDtypeStruct((bsz, n, d), F32),
        compiler_params=_cparams(("arbitrary", "arbitrary")),
        name="combine",
    )(offs, ovf, x1, g2, slot, y)


def _deinterleave(w, n_heads):
    lead = w.shape[:-1]
    w = w.reshape(lead + (n_heads, HEAD_DIM // 2, 2))
    return jnp.swapaxes(w, -1, -2).reshape(lead + (n_heads * HEAD_DIM,))


def _rope_tables(n):
    rows = n // GRID_W
    row = jnp.repeat(jnp.arange(rows), GRID_W).astype(F32)
    col = jnp.tile(jnp.arange(GRID_W), rows).astype(F32)
    n_freq = HEAD_DIM // 4
    inv = ROPE_THETA ** (-jnp.arange(n_freq, dtype=F32) / n_freq)
    ang = jnp.concatenate([row[:, None] * inv, col[:, None] * inv], axis=-1)
    cos = jnp.tile(jnp.cos(ang), (1, 4))
    sin = jnp.sin(ang)
    sin_signed = jnp.tile(jnp.concatenate([-sin, sin], axis=-1), (1, 2))
    return cos, sin_signed


def _split_bf16(w):
    hi = w.astype(BF16)
    return hi, (w - hi.astype(F32)).astype(BF16)


def _route_select(aff, cap, tile):
    bsz, ne, n = aff.shape
    n_pad = max(n, HALO * LANES)
    a = aff if n_pad == n else jnp.pad(aff, ((0, 0), (0, 0), (0, n_pad - n)))
    slot4, offc4 = _select(a.reshape(bsz, ne, n_pad // LANES, LANES), cap)
    n_tiles = n // tile
    offs = offc4[:, :, ::tile // LANES, 0][:, :, :n_tiles]
    win = _combine_window(cap, tile)
    ends = jnp.concatenate([offs[:, :, 1:], jnp.full((bsz, ne, 1), cap, I32)], axis=2)
    starts = jnp.minimum((offs >> 4) << 4, cap - win)
    ovf = jnp.any(ends > starts + win, axis=1).astype(I32).reshape(-1)
    return slot4.reshape(bsz, ne, n_pad)[:, :, :n], offs.reshape(-1), ovf


def kernel(x, c, ctx, c_ctx, w_ada, b_ada, norm1_g, norm2_g, pool_w, pool_scale, attn_w_qkv, attn_w_o,
           attn_q_norm, attn_k_norm, attn_sink, router_w, exp_w_gate, exp_w_up, exp_w_down):
    bsz, n, d = x.shape
    lc = ctx.shape[1]
    cap = 2 * n // N_EXPERTS
    cap_c = 2 * lc // N_EXPERTS
    tile_c = min(TOK_TILE, lc)

    cc = jnp.zeros((8, d), F32).at[:bsz].set(c).at[bsz].set(c_ctx)
    mods = _ada(cc, w_ada, b_ada)

    cos, sin = _rope_tables(n)
    cos_c = jnp.ones((lc, LANES), F32)
    sin_c = jnp.zeros((lc, LANES), F32)
    blk = jnp.arange(256) // HEAD_DIM
    bavg = ((blk[:, None] == blk[None, :]).astype(F32) / HEAD_DIM).astype(BF16)

    for i in range(DEPTH):
        update_ctx = i < DEPTH - 1
        j = i // 2
        is_pool = (i % 2) == 0
        m = [mods[i, :, k * d:(k + 1) * d] for k in range(6)]
        sh1, sc1, g1, sh2, sc2, g2 = [v[:bsz, None, :] for v in m]
        csh1, csc1, cg1, csh2, csc2, cg2 = [jnp.broadcast_to(v[bsz][None, None, :], (bsz, 1, d)) for v in m]
        n1g = norm1_g[i][None, :]
        n2g = norm2_g[i][None, :]
        wrh, wrl = _split_bf16(router_w[i].T)

        if is_pool:
            pw = pool_w[j].astype(BF16)
            ps = pool_scale[j][None, :]
            x1, h2, aff = _pool_layer(x, n1g, sh1, sc1, g1, pw, ps, n2g, sh2, sc2, wrh, wrl, 512)
            if update_ctx:
                c1, hc2, affc = _pool_layer(ctx, n1g, csh1, csc1, cg1, pw, ps, n2g, csh2, csc2, wrh, wrl, lc)
        else:
            nq = N_Q_HEADS * HEAD_DIM
            nk = N_KV_HEADS * HEAD_DIM
            wqkv = attn_w_qkv[j]
            wperm = jnp.concatenate([_deinterleave(wqkv[:, :nq], N_Q_HEADS),
                                     _deinterleave(wqkv[:, nq:nq + nk], N_KV_HEADS),
                                     wqkv[:, nq + nk:]], axis=1).astype(BF16)
            qg = jnp.tile(_deinterleave(attn_q_norm[j], 1), 4)[None, :]
            kg = jnp.tile(_deinterleave(attn_k_norm[j], 1), 4)[None, :]
            wo = attn_w_o[j].astype(BF16)
            sink = attn_sink[j]
            q, k, v = _qkv_layer(x, n1g, sh1, sc1, wperm, bavg, qg, kg, cos, sin, 512)
            qc, kc, vc = _qkv_layer(ctx, n1g, csh1, csc1, wperm, bavg, qg, kg, cos_c, sin_c, lc)
            o = _attn_layer(sink, q, k, v, kc, vc, True)
            x1, h2, aff = _oproj_layer(o, x, wo, g1, n2g, sh2, sc2, wrh, wrl, 512)
            if update_ctx:
                oc = _attn_layer(sink, qc, None, None, kc, vc, False)
                c1, hc2, affc = _oproj_layer(oc, ctx, wo, cg1, n2g, csh2, csc2, wrh, wrl, lc)

        slot, offs, ovf = _route_select(aff, cap, TOK_TILE)
        xs, vals = _gather(offs, h2, slot, aff, cap, TOK_TILE)
        if update_ctx:
            slot_c, offs_c, ovf_c = _route_select(affc, cap_c, tile_c)
            xs_c, vals_c = _gather(offs_c, hc2, slot_c, affc, cap_c, tile_c)
            y = _mlp(i, xs, vals, xs_c, vals_c, exp_w_gate, exp_w_up, exp_w_down)
            ctx = _combine(offs_c, ovf_c, c1, cg2, slot_c, y, cap_c, tile_c, cap)
        else:
            y = _mlp(i, xs, vals, None, None, exp_w_gate, exp_w_up, exp_w_down)
        x = _combine(offs, ovf, x1, g2, slot, y, cap, TOK_TILE, 0)
    return x
```

```python
import functools

import jax
import jax.numpy as jnp
from jax import lax
from jax.experimental import pallas as pl
from jax.experimental.pallas import tpu as pltpu

F32 = jnp.float32
BF16 = jnp.bfloat16
I32 = jnp.int32

D_MODEL = 1024
DEPTH = 4
GRID_W = 64
POOL_WINDOWS = (2, 4, 8, 16)
POOL_GROUP_DIM = 256
HEAD_DIM = 64
N_Q_HEADS = 16
N_KV_HEADS = 4
Q_PER_KV = 4
WINDOW = 128
BLOCK_Q = 128
ROPE_THETA = 10000.0
N_EXPERTS = 16
D_EXPERT = 2048
NORM_EPS = 1e-6
NEG_INF = -1e30

LANES = 128
BF16_ROWS = 16
HALO = 8
TOK_TILE = 256
SOFTMAX_ROWS = 32
GATHER_EXPERTS = 4
COMBINE_GROUP = 4
SMALL_WINDOW = 64
VMEM_LIMIT = 56 * 1024 * 1024

NT_DIMS = (((1,), (1,)), ((), ()))


def _cparams(sem, vmem=None):
    return pltpu.CompilerParams(dimension_semantics=sem, vmem_limit_bytes=vmem or VMEM_LIMIT)


def _norm_mod(x, g, shift, scale):
    ms = jnp.mean(x * x, axis=-1, keepdims=True)
    return (x * lax.rsqrt(ms + NORM_EPS)) * g * (1.0 + scale) + shift


def _ada_kernel(cc_ref, w_ref, b_ref, o_ref):
    cc = cc_ref[...]
    s = cc * jax.nn.sigmoid(cc)
    o_ref[0] = jnp.dot(s, w_ref[0], preferred_element_type=F32,
                       precision=lax.Precision.HIGHEST) + b_ref[0]


def _ada(cc, w_ada, b_ada):
    depth, d, six_d = w_ada.shape
    tn = 1536
    return pl.pallas_call(
        _ada_kernel,
        grid=(depth, six_d // tn),
        in_specs=[
            pl.BlockSpec((8, d), lambda i, j: (0, 0)),
            pl.BlockSpec((1, d, tn), lambda i, j: (i, 0, j)),
            pl.BlockSpec((1, 1, tn), lambda i, j: (i, 0, j)),
        ],
        out_specs=pl.BlockSpec((1, 8, tn), lambda i, j: (i, 0, j)),
        out_shape=jax.ShapeDtypeStruct((depth, 8, six_d), F32),
        compiler_params=_cparams(("arbitrary", "arbitrary")),
        name="ada",
    )(cc, w_ada, b_ada.reshape(depth, 1, six_d))


def _route(x1, n2g, sh2, sc2, wr_hi, wr_lo):
    h2 = _norm_mod(x1, n2g, sh2, sc2)
    h_hi = h2.astype(BF16)
    h_lo = (h2 - h_hi.astype(F32)).astype(BF16)
    lg = (lax.dot_general(wr_hi, h_hi, NT_DIMS, preferred_element_type=F32)
          + lax.dot_general(wr_hi, h_lo, NT_DIMS, preferred_element_type=F32)
          + lax.dot_general(wr_lo, h_hi, NT_DIMS, preferred_element_type=F32))
    m = jnp.max(lg, axis=0, keepdims=True)
    ex = jnp.exp(lg - m)
    aff = ex / jnp.sum(ex, axis=0, keepdims=True)
    return h_hi, aff


def _pool_kernel(x_ref, xp_ref, xn_ref, n1g_ref, sh1_ref, sc1_ref, g1_ref, pw_ref, ps_ref,
                 n2g_ref, sh2_ref, sc2_ref, wrh_ref, wrl_ref,
                 x1_ref, h2_ref, aff_ref, hs_ref, *, n_tokens):
    t = pl.program_id(1)
    nt = pl.num_programs(1)
    tt = x_ref.shape[1]
    n1g, sh1, sc1 = n1g_ref[...], sh1_ref[0], sc1_ref[0]
    x = x_ref[0]
    h = _norm_mod(x, n1g, sh1, sc1)
    hp = _norm_mod(xp_ref[0], n1g, sh1, sc1) * (t > 0).astype(F32)
    hn = _norm_mod(xn_ref[0], n1g, sh1, sc1) * (t < nt - 1).astype(F32)
    hs_ref[0:HALO, :] = hp
    hs_ref[HALO:HALO + tt, :] = h
    hs_ref[HALO + tt:HALO + tt + HALO, :] = hn
    pos = t * tt + lax.broadcasted_iota(I32, (tt, 1), 0)
    ys = []
    for g, w in enumerate(POOL_WINDOWS):
        half = w // 2
        c0 = g * POOL_GROUP_DIM
        s = hs_ref[HALO - half:HALO - half + tt, c0:c0 + POOL_GROUP_DIM]
        for k in range(1, w):
            s = s + hs_ref[HALO - half + k:HALO - half + k + tt, c0:c0 + POOL_GROUP_DIM]
        lo = jnp.clip(pos - half, 0, n_tokens)
        hi = jnp.clip(pos + half, 0, n_tokens)
        p = s / (hi - lo).astype(F32) - h[:, c0:c0 + POOL_GROUP_DIM]
        ys.append(jnp.dot(p.astype(BF16), pw_ref[g], preferred_element_type=F32))
    y = jnp.concatenate(ys, axis=1) * ps_ref[...]
    x1 = x + g1_ref[0] * y
    x1_ref[0] = x1
    h2, aff = _route(x1, n2g_ref[...], sh2_ref[0], sc2_ref[0], wrh_ref[...], wrl_ref[...])
    h2_ref[0] = h2
    aff_ref[0] = aff


def _row_spec(d):
    return pl.BlockSpec((1, d), lambda b, t: (0, 0))


def _mod_spec(d):
    return pl.BlockSpec((1, 1, d), lambda b, t: (b, 0, 0))


def _pool_layer(x, n1g, sh1, sc1, g1, pw, ps, n2g, sh2, sc2, wrh, wrl, tt):
    bsz, n, d = x.shape
    nb8 = n // HALO
    r = tt // HALO
    kern = functools.partial(_pool_kernel, n_tokens=n)
    return pl.pallas_call(
        kern,
        grid=(bsz, n // tt),
        in_specs=[
            pl.BlockSpec((1, tt, d), lambda b, t: (b, t, 0)),
            pl.BlockSpec((1, HALO, d), lambda b, t: (b, jnp.maximum(t * r - 1, 0), 0)),
            pl.BlockSpec((1, HALO, d), lambda b, t: (b, jnp.minimum((t + 1) * r, nb8 - 1), 0)),
            _row_spec(d), _mod_spec(d), _mod_spec(d), _mod_spec(d),
            pl.BlockSpec((4, POOL_GROUP_DIM, POOL_GROUP_DIM), lambda b, t: (0, 0, 0)),
            _row_spec(d),
            _row_spec(d), _mod_spec(d), _mod_spec(d),
            pl.BlockSpec((N_EXPERTS, d), lambda b, t: (0, 0)),
            pl.BlockSpec((N_EXPERTS, d), lambda b, t: (0, 0)),
        ],
        out_specs=[
            pl.BlockSpec((1, tt, d), lambda b, t: (b, t, 0)),
            pl.BlockSpec((1, tt, d), lambda b, t: (b, t, 0)),
            pl.BlockSpec((1, N_EXPERTS, tt), lambda b, t: (b, 0, t)),
        ],
        out_shape=[
            jax.ShapeDtypeStruct((bsz, n, d), F32),
            jax.ShapeDtypeStruct((bsz, n, d), BF16),
            jax.ShapeDtypeStruct((bsz, N_EXPERTS, n), F32),
        ],
        scratch_shapes=[pltpu.VMEM((tt + 2 * HALO, d), F32)],
        compiler_params=_cparams(("arbitrary", "arbitrary")),
        name="pool_route",
    )(x, x, x, n1g, sh1, sc1, g1, pw, ps, n2g, sh2, sc2, wrh, wrl)


def _head_rms(v, bavg, gain):
    sq = v * v
    sq_hi = sq.astype(BF16)
    sq_lo = (sq - sq_hi.astype(F32)).astype(BF16)
    ms = (jnp.dot(sq_hi, bavg, preferred_element_type=F32)
          + jnp.dot(sq_lo, bavg, preferred_element_type=F32))
    return v * lax.rsqrt(ms + NORM_EPS) * gain


def _rope(v, cos, sin_signed):
    lane = lax.broadcasted_iota(I32, v.shape, 1)
    fwd = pltpu.roll(v, 32, 1)
    bwd = pltpu.roll(v, 96, 1)
    partner = jnp.where((lane % HEAD_DIM) < HEAD_DIM // 2, bwd, fwd)
    return v * cos + partner * sin_signed


def _qkv_kernel(x_ref, n1g_ref, sh1_ref, sc1_ref, w_ref, bavg_ref, qg_ref, kg_ref,
                cos_ref, sin_ref, q_ref, k_ref, v_ref):
    h = _norm_mod(x_ref[0], n1g_ref[...], sh1_ref[0], sc1_ref[0]).astype(BF16)
    qkv = jnp.dot(h, w_ref[...], preferred_element_type=F32)
    nq = N_Q_HEADS * HEAD_DIM
    nk = N_KV_HEADS * HEAD_DIM
    bavg = bavg_ref[...]
    cos, sin = cos_ref[...], sin_ref[...]
    scale = HEAD_DIM ** -0.5
    for c in range(nq // 256):
        qn = _head_rms(qkv[:, c * 256:(c + 1) * 256], bavg, qg_ref[...])
        for s in range(2):
            qr = _rope(qn[:, s * LANES:(s + 1) * LANES], cos, sin) * scale
            for hh in range(2):
                head = c * 4 + s * 2 + hh
                q_ref[0, head] = qr[:, hh * HEAD_DIM:(hh + 1) * HEAD_DIM].astype(BF16)
    kn = _head_rms(qkv[:, nq:nq + nk], bavg, kg_ref[...])
    v = qkv[:, nq + nk:]
    for s in range(2):
        kr = _rope(kn[:, s * LANES:(s + 1) * LANES], cos, sin)
        for hh in range(2):
            head = s * 2 + hh
            k_ref[0, head] = kr[:, hh * HEAD_DIM:(hh + 1) * HEAD_DIM].astype(BF16)
            v_ref[0, head] = v[:, head * HEAD_DIM:(head + 1) * HEAD_DIM].astype(BF16)


def _qkv_layer(x, n1g, sh1, sc1, w, bavg, qg, kg, cos, sin, tt):
    bsz, n, d = x.shape
    return pl.pallas_call(
        _qkv_kernel,
        grid=(bsz, n // tt),
        in_specs=[
            pl.BlockSpec((1, tt, d), lambda b, t: (b, t, 0)),
            _row_spec(d), _mod_spec(d), _mod_spec(d),
            pl.BlockSpec(w.shape, lambda b, t: (0, 0)),
            pl.BlockSpec((256, 256), lambda b, t: (0, 0)),
            _row_spec(256), _row_spec(256),
            pl.BlockSpec((tt, LANES), lambda b, t: (t, 0)),
            pl.BlockSpec((tt, LANES), lambda b, t: (t, 0)),
        ],
        out_specs=[
            pl.BlockSpec((1, N_Q_HEADS, tt, HEAD_DIM), lambda b, t: (b, 0, t, 0)),
            pl.BlockSpec((1, N_KV_HEADS, tt, HEAD_DIM), lambda b, t: (b, 0, t, 0)),
            pl.BlockSpec((1, N_KV_HEADS, tt, HEAD_DIM), lambda b, t: (b, 0, t, 0)),
        ],
        out_shape=[
            jax.ShapeDtypeStruct((bsz, N_Q_HEADS, n, HEAD_DIM), BF16),
            jax.ShapeDtypeStruct((bsz, N_KV_HEADS, n, HEAD_DIM), BF16),
            jax.ShapeDtypeStruct((bsz, N_KV_HEADS, n, HEAD_DIM), BF16),
        ],
        compiler_params=_cparams(("arbitrary", "arbitrary")),
        name="qkv",
    )(x, n1g, sh1, sc1, w, bavg, qg, kg, cos, sin)


def _attn_kernel(sink_ref, q_ref, *refs, band):
    if band:
        kp_ref, kc_ref, kn_ref, vp_ref, vc_ref, vn_ref, kx_ref, vx_ref, bias_ref, o_ref, s_ref, p_ref = refs
    else:
        kx_ref, vx_ref, o_ref, s_ref, p_ref = refs
    tq = q_ref.shape[2]
    n_chunks = Q_PER_KV * tq // SOFTMAX_ROWS
    for g in range(N_KV_HEADS):
        if band:
            kg = jnp.concatenate([kp_ref[0, g], kc_ref[0, g], kn_ref[0, g], kx_ref[0, g]], axis=0)
            vg = jnp.concatenate([vp_ref[0, g], vc_ref[0, g], vn_ref[0, g], vx_ref[0, g]], axis=0)
        else:
            kg = kx_ref[0, g]
            vg = vx_ref[0, g]
        vext = jnp.concatenate([vg, jnp.ones_like(vg)], axis=1)
        qg = q_ref[0, g * Q_PER_KV:(g + 1) * Q_PER_KV].reshape(Q_PER_KV * tq, HEAD_DIM)
        s_ref[...] = lax.dot_general(qg, kg, NT_DIMS, preferred_element_type=F32)
        sink_terms = []
        for r in range(n_chunks):
            rs = slice(r * SOFTMAX_ROWS, (r + 1) * SOFTMAX_ROWS)
            sc = s_ref[rs, :]
            if band:
                sc = sc + bias_ref[0, rs, :]
            sk = sink_ref[g * Q_PER_KV + (r * SOFTMAX_ROWS) // tq]
            m = jnp.maximum(jnp.max(sc, axis=-1, keepdims=True), sk)
            p_ref[rs, :] = jnp.exp(sc - m).astype(BF16)
            sink_terms.append(jnp.exp(sk - m))
        oe = jnp.dot(p_ref[...], vext, preferred_element_type=F32)
        den = oe[:, HEAD_DIM:HEAD_DIM + 1] + jnp.concatenate(sink_terms, axis=0)
        o = (oe[:, :HEAD_DIM] / den).astype(o_ref.dtype)
        for hh in range(Q_PER_KV):
            head = g * Q_PER_KV + hh
            o_ref[0, :, head * HEAD_DIM:(head + 1) * HEAD_DIM] = o[hh * tq:(hh + 1) * tq]


def _band_bias(lc):
    nband = BLOCK_Q + 2 * WINDOW
    row = jnp.arange(Q_PER_KV * BLOCK_Q)[:, None] % BLOCK_Q
    col = jnp.arange(nband + lc)[None, :]
    dlt = col - row
    inband = (dlt >= 0) & (dlt <= 2 * WINDOW)
    variants = [inband & (col >= WINDOW), inband, inband & (col < WINDOW + BLOCK_Q)]
    return jnp.stack([jnp.where(v | (col >= nband), 0.0, NEG_INF) for v in variants]).astype(F32)


def _attn_layer(sink, q, k, v, kx, vx, band):
    bsz, _, n, _ = q.shape
    lc = kx.shape[2]
    tq = BLOCK_Q if band else n
    nb = n // tq
    n_keys = (BLOCK_Q + 2 * WINDOW + lc) if band else lc
    qspec = pl.BlockSpec((1, N_Q_HEADS, tq, HEAD_DIM), lambda b, t: (b, 0, t, 0))
    xspec = pl.BlockSpec((1, N_KV_HEADS, lc, HEAD_DIM), lambda b, t: (b, 0, 0, 0))
    if band:
        prev = pl.BlockSpec((1, N_KV_HEADS, tq, HEAD_DIM), lambda b, t: (b, 0, jnp.maximum(t - 1, 0), 0))
        cur = pl.BlockSpec((1, N_KV_HEADS, tq, HEAD_DIM), lambda b, t: (b, 0, t, 0))
        nxt = pl.BlockSpec((1, N_KV_HEADS, tq, HEAD_DIM), lambda b, t: (b, 0, jnp.minimum(t + 1, nb - 1), 0))
        bias = _band_bias(lc)
        bspec = pl.BlockSpec((1,) + bias.shape[1:],
                             lambda b, t: (jnp.where(t == 0, 0, jnp.where(t == nb - 1, 2, 1)), 0, 0))
        in_specs = [qspec, prev, cur, nxt, prev, cur, nxt, xspec, xspec, bspec]
        args = (q, k, k, k, v, v, v, kx, vx, bias)
    else:
        in_specs = [qspec, xspec, xspec]
        args = (q, kx, vx)
    return pl.pallas_call(
        functools.partial(_attn_kernel, band=band),
        grid=(bsz, nb),
        in_specs=[pl.BlockSpec(memory_space=pltpu.SMEM)] + in_specs,
        out_specs=pl.BlockSpec((1, tq, N_Q_HEADS * HEAD_DIM), lambda b, t: (b, t, 0)),
        out_shape=jax.ShapeDtypeStruct((bsz, n, N_Q_HEADS * HEAD_DIM), BF16),
        scratch_shapes=[pltpu.VMEM((Q_PER_KV * tq, n_keys), F32), pltpu.VMEM((Q_PER_KV * tq, n_keys), BF16)],
        compiler_params=_cparams(("arbitrary", "arbitrary")),
        name="attn_band" if band else "attn_ctx",
    )(sink, *args)


def _oproj_kernel(o_ref, x_ref, wo_ref, g1_ref, n2g_ref, sh2_ref, sc2_ref, wrh_ref, wrl_ref,
                  x1_ref, h2_ref, aff_ref):
    y = jnp.dot(o_ref[0], wo_ref[...], preferred_element_type=F32)
    x1 = x_ref[0] + g1_ref[0] * y
    x1_ref[0] = x1
    h2, aff = _route(x1, n2g_ref[...], sh2_ref[0], sc2_ref[0], wrh_ref[...], wrl_ref[...])
    h2_ref[0] = h2
    aff_ref[0] = aff


def _oproj_layer(o, x, wo, g1, n2g, sh2, sc2, wrh, wrl, tt):
    bsz, n, d = x.shape
    return pl.pallas_call(
        _oproj_kernel,
        grid=(bsz, n // tt),
        in_specs=[
            pl.BlockSpec((1, tt, d), lambda b, t: (b, t, 0)),
            pl.BlockSpec((1, tt, d), lambda b, t: (b, t, 0)),
            pl.BlockSpec((d, d), lambda b, t: (0, 0)),
            _mod_spec(d), _row_spec(d), _mod_spec(d), _mod_spec(d),
            pl.BlockSpec((N_EXPERTS, d), lambda b, t: (0, 0)),
            pl.BlockSpec((N_EXPERTS, d), lambda b, t: (0, 0)),
        ],
        out_specs=[
            pl.BlockSpec((1, tt, d), lambda b, t: (b, t, 0)),
            pl.BlockSpec((1, tt, d), lambda b, t: (b, t, 0)),
            pl.BlockSpec((1, N_EXPERTS, tt), lambda b, t: (b, 0, t)),
        ],
        out_shape=[
            jax.ShapeDtypeStruct((bsz, n, d), F32),
            jax.ShapeDtypeStruct((bsz, n, d), BF16),
            jax.ShapeDtypeStruct((bsz, N_EXPERTS, n), F32),
        ],
        compiler_params=_cparams(("arbitrary", "arbitrary")),
        name="oproj_route",
    )(o, x, wo, g1, n2g, sh2, sc2, wrh, wrl)


def _prefix_counts(mask, n_chunks):
    ne = mask.shape[0]
    m2 = mask.reshape(ne * n_chunks, LANES).astype(BF16)
    ri = lax.broadcasted_iota(I32, (LANES, LANES), 0)
    ci = lax.broadcasted_iota(I32, (LANES, LANES), 1)
    upper = (ri <= ci).astype(BF16)
    within = jnp.dot(m2, upper, preferred_element_type=F32).reshape(ne, n_chunks, LANES)
    tot = jnp.dot(m2, jnp.ones((LANES, LANES), BF16), preferred_element_type=F32).reshape(ne, n_chunks, LANES)
    rc = lax.broadcasted_iota(I32, (n_chunks, n_chunks), 0)
    cc = lax.broadcasted_iota(I32, (n_chunks, n_chunks), 1)
    strict_lower = (cc < rc).astype(F32)
    off = jnp.stack([jnp.dot(strict_lower, tot[e], preferred_element_type=F32) for e in range(ne)])
    return within + off, off


def _select_kernel(a_ref, slot_ref, offc_ref, *, cap):
    a = a_ref[0]
    ne, n_chunks, _ = a.shape
    bits = lax.bitcast_convert_type(a, I32)

    def count(pred):
        c = jnp.sum(pred.astype(F32), axis=1, keepdims=True)
        return jnp.sum(c, axis=2, keepdims=True)

    def body(i, thr):
        cand = thr | lax.shift_left(jnp.int32(1), 30 - i)
        return jnp.where(count(bits >= cand) >= cap, cand, thr)

    thr = lax.fori_loop(0, 31, body, jnp.zeros((ne, 1, 1), I32))
    gt = bits > thr
    eq = bits == thr
    need = cap - count(gt)
    eq_rank, _ = _prefix_counts(eq.astype(F32), n_chunks)
    sel = gt | (eq & (eq_rank <= need))
    sel_f = sel.astype(F32)
    incl, off = _prefix_counts(sel_f, n_chunks)
    slot_ref[0] = jnp.where(sel, (incl - 1.0).astype(I32), -1)
    offc_ref[0] = off.astype(I32)


def _select(aff4, cap):
    bsz, ne, n_chunks, cw = aff4.shape
    spec = pl.BlockSpec((1, ne, n_chunks, cw), lambda b: (b, 0, 0, 0))
    return pl.pallas_call(
        functools.partial(_select_kernel, cap=cap),
        grid=(bsz,),
        in_specs=[spec],
        out_specs=[spec, spec],
        out_shape=[jax.ShapeDtypeStruct(aff4.shape, I32), jax.ShapeDtypeStruct(aff4.shape, I32)],
        compiler_params=_cparams(("arbitrary",)),
        name="select",
    )(aff4)


def _window(cap, tile):
    return min(tile + BF16_ROWS, cap)


def _window_start(off, cap, win):
    start = lax.shift_left(lax.shift_right_logical(off, 4), 4)
    return pl.multiple_of(jnp.minimum(start, cap - win), BF16_ROWS)


def _gather_kernel(offs_ref, small_ref, h2_ref, sl_ref, af_ref, xs_ref, val_ref, *, cap, tile, n_tiles):
    b, eg, s = pl.program_id(0), pl.program_id(1), pl.program_id(2)
    ge = sl_ref.shape[1]
    sub = h2_ref.shape[1] // tile
    full = _window(cap, tile)
    small = min(SMALL_WINDOW, full)

    @pl.when(s == 0)
    def _():
        xs_ref[...] = jnp.zeros_like(xs_ref)
        val_ref[...] = jnp.zeros_like(val_ref)

    def tile_rows(u, win):
        ts = slice(u * tile, (u + 1) * tile)
        starts, hits = [], []
        for j in range(ge):
            off = offs_ref[(b * N_EXPERTS + eg * ge + j) * n_tiles + s * sub + u]
            start = _window_start(off, cap, win)
            rows = start + lax.broadcasted_iota(I32, (win, tile), 0)
            starts.append(start)
            hits.append(rows == sl_ref[0, j, :, ts])
        stacked = jnp.concatenate([h.astype(BF16) for h in hits], axis=0)
        res = jnp.dot(stacked, h2_ref[0, ts, :], preferred_element_type=F32)
        for j in range(ge):
            dst = pl.ds(starts[j], win)
            xs_ref[0, j, dst, :] = (xs_ref[0, j, dst, :].astype(F32) + res[j * win:(j + 1) * win]).astype(BF16)
            val_ref[0, j, dst, :] += jnp.sum(jnp.where(hits[j], af_ref[0, j, :, ts], 0.0), axis=1, keepdims=True)

    for u in range(sub):
        if small == full:
            tile_rows(u, full)
        else:
            fits = small_ref[(b * pl.num_programs(1) + eg) * n_tiles + s * sub + u] > 0
            pl.when(fits)(functools.partial(tile_rows, u, small))
            pl.when(jnp.logical_not(fits))(functools.partial(tile_rows, u, full))


def _gather(offs, small, h2, slot, aff, cap, tile):
    bsz, n, d = h2.shape
    tg = min(n, 1024)
    n_tiles = n // tile
    ge = GATHER_EXPERTS
    kern = functools.partial(_gather_kernel, cap=cap, tile=tile, n_tiles=n_tiles)
    return pl.pallas_call(
        kern,
        grid_spec=pltpu.PrefetchScalarGridSpec(
            num_scalar_prefetch=2,
            grid=(bsz, N_EXPERTS // ge, n // tg),
            in_specs=[
                pl.BlockSpec((1, tg, d), lambda b, e, s, offs, small: (b, s, 0)),
                pl.BlockSpec((1, ge, 1, tg), lambda b, e, s, offs, small: (b, e, 0, s)),
                pl.BlockSpec((1, ge, 1, tg), lambda b, e, s, offs, small: (b, e, 0, s)),
            ],
            out_specs=[
                pl.BlockSpec((1, ge, cap, d), lambda b, e, s, offs, small: (b, e, 0, 0)),
                pl.BlockSpec((1, ge, cap, 1), lambda b, e, s, offs, small: (b, e, 0, 0)),
            ],
        ),
        out_shape=[
            jax.ShapeDtypeStruct((bsz, N_EXPERTS, cap, d), BF16),
            jax.ShapeDtypeStruct((bsz, N_EXPERTS, cap, 1), F32),
        ],
        compiler_params=_cparams(("arbitrary", "arbitrary", "arbitrary")),
        name="gather",
    )(offs, small, h2, slot.reshape(bsz, N_EXPERTS, 1, n), aff.reshape(bsz, N_EXPERTS, 1, n))


def _up_kernel(*refs, has_ctx):
    if has_ctx:
        xl_ref, xc_ref, wg_ref, wu_ref, h_ref, xs_ref = refs
        cl = xl_ref.shape[2]

        @pl.when(pl.program_id(2) == 0)
        def _():
            xs_ref[0:cl, :] = xl_ref[0, 0]
            xs_ref[cl:, :] = xc_ref[0, 0]

        xs = xs_ref[...]
    else:
        xl_ref, wg_ref, wu_ref, h_ref = refs
        xs = xl_ref[0, 0]
    a = jnp.dot(xs, wg_ref[0, 0].astype(BF16), preferred_element_type=F32)
    u = jnp.dot(xs, wu_ref[0, 0].astype(BF16), preferred_element_type=F32)
    h_ref[0, 0] = (a * jax.nn.sigmoid(a) * u).astype(BF16)


def _down_kernel(*refs, has_ctx):
    if has_ctx:
        h_ref, vl_ref, vc_ref, wd_ref, y_ref = refs
    else:
        h_ref, vl_ref, wd_ref, y_ref = refs
    cl = vl_ref.shape[2]
    res = jnp.dot(h_ref[0, 0], wd_ref[0, 0].astype(BF16), preferred_element_type=F32)
    y_ref[0, 0, 0:cl, :] = (res[0:cl] * vl_ref[0, 0]).astype(BF16)
    if has_ctx:
        y_ref[0, 0, cl:, :] = (res[cl:] * vc_ref[0, 0]).astype(BF16)


def _mlp(layer, xl, vl, xc, vc, wg, wu, wd):
    bsz, ne, cl, d = xl.shape
    fdim = wg.shape[-1]
    has_ctx = xc is not None
    cc = xc.shape[2] if has_ctx else 0
    rows = cl + cc
    fc = 512
    dn = 512

    xspecs = [pl.BlockSpec((1, 1, cl, d), lambda e, b, f: (b, e, 0, 0))]
    if has_ctx:
        xspecs.append(pl.BlockSpec((1, 1, cc, d), lambda e, b, f: (b, e, 0, 0)))
    hid = pl.pallas_call(
        functools.partial(_up_kernel, has_ctx=has_ctx),
        grid=(ne, bsz, fdim // fc),
        in_specs=xspecs + [
            pl.BlockSpec((1, 1, d, fc), lambda e, b, f: (layer, e, 0, f)),
            pl.BlockSpec((1, 1, d, fc), lambda e, b, f: (layer, e, 0, f)),
        ],
        out_specs=pl.BlockSpec((1, 1, rows, fc), lambda e, b, f: (b, e, 0, f)),
        out_shape=jax.ShapeDtypeStruct((bsz, ne, rows, fdim), BF16),
        scratch_shapes=[pltpu.VMEM((rows, d), BF16)] if has_ctx else [],
        compiler_params=_cparams(("arbitrary", "arbitrary", "arbitrary")),
        name="experts_up",
    )(*((xl, xc) if has_ctx else (xl,)), wg, wu)

    vspecs = [pl.BlockSpec((1, 1, cl, 1), lambda e, b, j: (b, e, 0, 0))]
    if has_ctx:
        vspecs.append(pl.BlockSpec((1, 1, cc, 1), lambda e, b, j: (b, e, 0, 0)))
    return pl.pallas_call(
        functools.partial(_down_kernel, has_ctx=has_ctx),
        grid=(ne, bsz, d // dn),
        in_specs=[pl.BlockSpec((1, 1, rows, fdim), lambda e, b, j: (b, e, 0, 0))] + vspecs + [
            pl.BlockSpec((1, 1, fdim, dn), lambda e, b, j: (layer, e, 0, j)),
        ],
        out_specs=pl.BlockSpec((1, 1, rows, dn), lambda e, b, j: (b, e, 0, j)),
        out_shape=jax.ShapeDtypeStruct((bsz, ne, rows, d), BF16),
        compiler_params=_cparams(("arbitrary", "arbitrary", "arbitrary")),
        name="experts_down",
    )(hid, *((vl, vc) if has_ctx else (vl,)), wd)


def _combine_window(cap, tile):
    return min(tile, cap)


def _combine_kernel(offs_ref, ovf_ref, small_ref, x1_ref, g2_ref, sl_ref, y_hbm, o_ref, ywin_ref, yext_ref,
                    sems, semx, *, cap, tile, n_tiles, base):
    b, t = pl.program_id(0), pl.program_id(1)
    n_steps = pl.num_programs(0) * n_tiles
    win = _combine_window(cap, tile)
    small = min(SMALL_WINDOW, win)
    step = b * n_tiles + t
    buf = lax.rem(step, 2)

    def start_of(bb, tt, e, w):
        return _window_start(offs_ref[(bb * N_EXPERTS + e) * n_tiles + tt], cap, w)

    def window_copy(bb, tt, bf, e, w):
        src = pl.multiple_of(base + start_of(bb, tt, e, w), BF16_ROWS)
        return pltpu.make_async_copy(y_hbm.at[bb, e, pl.ds(src, w), :],
                                     ywin_ref.at[bf, pl.ds(e * w, w), :], sems.at[bf, e])

    def by_window(stp, fn):
        if small == win:
            fn(win)
        else:
            fits = small_ref[stp] > 0
            pl.when(fits)(functools.partial(fn, small))
            pl.when(jnp.logical_not(fits))(functools.partial(fn, win))

    def start_windows(bb, tt, bf, w):
        for e in range(N_EXPERTS):
            window_copy(bb, tt, bf, e, w).start()

    @pl.when(step == 0)
    def _():
        by_window(step, functools.partial(start_windows, b, t, buf))

    @pl.when(step < n_steps - 1)
    def _():
        wrap = t == n_tiles - 1
        b_next = jnp.where(wrap, b + 1, b)
        t_next = jnp.where(wrap, 0, t + 1)
        by_window(step + 1, functools.partial(start_windows, b_next, t_next, 1 - buf))

    sl = sl_ref[0] + 1
    eye = (lax.broadcasted_iota(I32, (tile, tile), 0)
           == lax.broadcasted_iota(I32, (tile, tile), 1)).astype(BF16)
    d_hi = lax.shift_right_logical(sl, 6).astype(F32).astype(BF16)
    d_lo = (sl & 63).astype(F32).astype(BF16)
    slot_t = (lax.dot_general(eye, d_hi, NT_DIMS, preferred_element_type=F32) * 64.0
              + lax.dot_general(eye, d_lo, NT_DIMS, preferred_element_type=F32)).astype(I32) - 1

    def accumulate(w):
        lane = lax.broadcasted_iota(I32, (tile, w), 1)
        hits = [(lane == slot_t[:, e:e + 1] - start_of(b, t, e, w)).astype(BF16) for e in range(N_EXPERTS)]
        for e in range(N_EXPERTS):
            window_copy(b, t, buf, e, w).wait()
        acc = None
        for g in range(0, N_EXPERTS, COMBINE_GROUP):
            part = jnp.dot(jnp.concatenate(hits[g:g + COMBINE_GROUP], axis=1),
                           ywin_ref[buf, g * w:(g + COMBINE_GROUP) * w, :], preferred_element_type=F32)
            acc = part if acc is None else acc + part
        o_ref[0] = x1_ref[0] + g2_ref[0] * acc

    by_window(step, accumulate)

    if win < cap:
        @pl.when(ovf_ref[step] > 0)
        def _():
            def ext_start(e):
                s0 = start_of(b, t, e, win)
                return s0, pl.multiple_of(jnp.minimum(s0 + win, cap - BF16_ROWS), BF16_ROWS)

            def ext_copy(e):
                src = pl.multiple_of(base + ext_start(e)[1], BF16_ROWS)
                return pltpu.make_async_copy(y_hbm.at[b, e, pl.ds(src, BF16_ROWS), :],
                                             yext_ref.at[pl.ds(e * BF16_ROWS, BF16_ROWS), :], semx.at[e])

            for e in range(N_EXPERTS):
                ext_copy(e).start()
            lane_x = lax.broadcasted_iota(I32, (tile, BF16_ROWS), 1)
            hits_x = []
            for e in range(N_EXPERTS):
                ext_copy(e).wait()
                s0, s1 = ext_start(e)
                row = s1 + lane_x
                hits_x.append(((row == slot_t[:, e:e + 1]) & (row >= s0 + win)).astype(BF16))
            extra = jnp.dot(jnp.concatenate(hits_x, axis=1), yext_ref[...], preferred_element_type=F32)
            o_ref[0] += g2_ref[0] * extra


def _combine(offs, ovf, small, x1, g2, slot, y, cap, tile, base):
    bsz, n, d = x1.shape
    n_tiles = n // tile
    win = _combine_window(cap, tile)
    kern = functools.partial(_combine_kernel, cap=cap, tile=tile, n_tiles=n_tiles, base=base)
    return pl.pallas_call(
        kern,
        grid_spec=pltpu.PrefetchScalarGridSpec(
            num_scalar_prefetch=3,
            grid=(bsz, n_tiles),
            in_specs=[
                pl.BlockSpec((1, tile, d), lambda b, t, *_: (b, t, 0)),
                pl.BlockSpec((1, 1, d), lambda b, t, *_: (b, 0, 0)),
                pl.BlockSpec((1, N_EXPERTS, tile), lambda b, t, *_: (b, 0, t)),
                pl.BlockSpec(memory_space=pl.ANY),
            ],
            out_specs=pl.BlockSpec((1, tile, d), lambda b, t, *_: (b, t, 0)),
            scratch_shapes=[pltpu.VMEM((2, N_EXPERTS * win, d), BF16),
                            pltpu.VMEM((N_EXPERTS * BF16_ROWS, d), BF16),
                            pltpu.SemaphoreType.DMA((2, N_EXPERTS)),
                            pltpu.SemaphoreType.DMA((N_EXPERTS,))],
        ),
        out_shape=jax.ShapeDtypeStruct((bsz, n, d), F32),
        compiler_params=_cparams(("arbitrary", "arbitrary")),
        name="combine",
    )(offs, ovf, small, x1, g2, slot, y)


def _deinterleave(w, n_heads):
    lead = w.shape[:-1]
    w = w.reshape(lead + (n_heads, HEAD_DIM // 2, 2))
    return jnp.swapaxes(w, -1, -2).reshape(lead + (n_heads * HEAD_DIM,))


def _rope_tables(n):
    rows = n // GRID_W
    row = jnp.repeat(jnp.arange(rows), GRID_W).astype(F32)
    col = jnp.tile(jnp.arange(GRID_W), rows).astype(F32)
    n_freq = HEAD_DIM // 4
    inv = ROPE_THETA ** (-jnp.arange(n_freq, dtype=F32) / n_freq)
    ang = jnp.concatenate([row[:, None] * inv, col[:, None] * inv], axis=-1)
    cos = jnp.tile(jnp.cos(ang), (1, 4))
    sin = jnp.sin(ang)
    sin_signed = jnp.tile(jnp.concatenate([-sin, sin], axis=-1), (1, 2))
    return cos, sin_signed


def _split_bf16(w):
    hi = w.astype(BF16)
    return hi, (w - hi.astype(F32)).astype(BF16)


def _route_select(aff, cap, tile):
    bsz, ne, n = aff.shape
    n_pad = max(n, HALO * LANES)
    a = aff if n_pad == n else jnp.pad(aff, ((0, 0), (0, 0), (0, n_pad - n)))
    slot4, offc4 = _select(a.reshape(bsz, ne, n_pad // LANES, LANES), cap)
    n_tiles = n // tile
    offs = offc4[:, :, ::tile // LANES, 0][:, :, :n_tiles]
    ends = jnp.concatenate([offs[:, :, 1:], jnp.full((bsz, ne, 1), cap, I32)], axis=2)

    def fits(win):
        return ends <= jnp.minimum((offs >> 4) << 4, cap - win) + win

    win_c = _combine_window(cap, tile)
    tables = {
        "offs": offs.reshape(-1),
        "combine_overflow": jnp.any(~fits(win_c), axis=1).astype(I32).reshape(-1),
        "combine_small": jnp.all(fits(min(SMALL_WINDOW, win_c)), axis=1).astype(I32).reshape(-1),
        "gather_small": jnp.all(
            fits(min(SMALL_WINDOW, _window(cap, tile))).reshape(bsz, ne // GATHER_EXPERTS, GATHER_EXPERTS, n_tiles),
            axis=2).astype(I32).reshape(-1),
    }
    return slot4.reshape(bsz, ne, n_pad)[:, :, :n], tables


def kernel(x, c, ctx, c_ctx, w_ada, b_ada, norm1_g, norm2_g, pool_w, pool_scale, attn_w_qkv, attn_w_o,
           attn_q_norm, attn_k_norm, attn_sink, router_w, exp_w_gate, exp_w_up, exp_w_down):
    bsz, n, d = x.shape
    lc = ctx.shape[1]
    cap = 2 * n // N_EXPERTS
    cap_c = 2 * lc // N_EXPERTS
    tile_c = min(TOK_TILE, lc)

    cc = jnp.zeros((8, d), F32).at[:bsz].set(c).at[bsz].set(c_ctx)
    mods = _ada(cc, w_ada, b_ada)

    cos, sin = _rope_tables(n)
    cos_c = jnp.ones((lc, LANES), F32)
    sin_c = jnp.zeros((lc, LANES), F32)
    blk = jnp.arange(256) // HEAD_DIM
    bavg = ((blk[:, None] == blk[None, :]).astype(F32) / HEAD_DIM).astype(BF16)

    for i in range(DEPTH):
        update_ctx = i < DEPTH - 1
        j = i // 2
        is_pool = (i % 2) == 0
        m = [mods[i, :, k * d:(k + 1) * d] for k in range(6)]
        sh1, sc1, g1, sh2, sc2, g2 = [v[:bsz, None, :] for v in m]
        csh1, csc1, cg1, csh2, csc2, cg2 = [jnp.broadcast_to(v[bsz][None, None, :], (bsz, 1, d)) for v in m]
        n1g = norm1_g[i][None, :]
        n2g = norm2_g[i][None, :]
        wrh, wrl = _split_bf16(router_w[i].T)

        if is_pool:
            pw = pool_w[j].astype(BF16)
            ps = pool_scale[j][None, :]
            x1, h2, aff = _pool_layer(x, n1g, sh1, sc1, g1, pw, ps, n2g, sh2, sc2, wrh, wrl, 512)
            if update_ctx:
                c1, hc2, affc = _pool_layer(ctx, n1g, csh1, csc1, cg1, pw, ps, n2g, csh2, csc2, wrh, wrl, lc)
        else:
            nq = N_Q_HEADS * HEAD_DIM
            nk = N_KV_HEADS * HEAD_DIM
            wqkv = attn_w_qkv[j]
            wperm = jnp.concatenate([_deinterleave(wqkv[:, :nq], N_Q_HEADS),
                                     _deinterleave(wqkv[:, nq:nq + nk], N_KV_HEADS),
                                     wqkv[:, nq + nk:]], axis=1).astype(BF16)
            qg = jnp.tile(_deinterleave(attn_q_norm[j], 1), 4)[None, :]
            kg = jnp.tile(_deinterleave(attn_k_norm[j], 1), 4)[None, :]
            wo = attn_w_o[j].astype(BF16)
            sink = attn_sink[j]
            q, k, v = _qkv_layer(x, n1g, sh1, sc1, wperm, bavg, qg, kg, cos, sin, 512)
            qc, kc, vc = _qkv_layer(ctx, n1g, csh1, csc1, wperm, bavg, qg, kg, cos_c, sin_c, lc)
            o = _attn_layer(sink, q, k, v, kc, vc, True)
            x1, h2, aff = _oproj_layer(o, x, wo, g1, n2g, sh2, sc2, wrh, wrl, 512)
            if update_ctx:
                oc = _attn_layer(sink, qc, None, None, kc, vc, False)
                c1, hc2, affc = _oproj_layer(oc, ctx, wo, cg1, n2g, csh2, csc2, wrh, wrl, lc)

        slot, tb = _route_select(aff, cap, TOK_TILE)
        xs, vals = _gather(tb["offs"], tb["gather_small"], h2, slot, aff, cap, TOK_TILE)
        if update_ctx:
            slot_c, tc = _route_select(affc, cap_c, tile_c)
            xs_c, vals_c = _gather(tc["offs"], tc["gather_small"], hc2, slot_c, affc, cap_c, tile_c)
            y = _mlp(i, xs, vals, xs_c, vals_c, exp_w_gate, exp_w_up, exp_w_down)
            ctx = _combine(tc["offs"], tc["combine_overflow"], tc["combine_small"], c1, cg2, slot_c, y,
                           cap_c, tile_c, cap)
        else:
            y = _mlp(i, xs, vals, None, None, exp_w_gate, exp_w_up, exp_w_down)
        x = _combine(tb["offs"], tb["combine_overflow"], tb["combine_small"], x1, g2, slot, y, cap, TOK_TILE, 0)
    return x
```

```python
import functools

import jax
import jax.numpy as jnp
from jax import lax
from jax.experimental import pallas as pl
from jax.experimental.pallas import tpu as pltpu

F32 = jnp.float32
BF16 = jnp.bfloat16
I32 = jnp.int32

D_MODEL = 1024
DEPTH = 4
GRID_W = 64
POOL_WINDOWS = (2, 4, 8, 16)
POOL_GROUP_DIM = 256
HEAD_DIM = 64
N_Q_HEADS = 16
N_KV_HEADS = 4
Q_PER_KV = 4
WINDOW = 128
BLOCK_Q = 128
ROPE_THETA = 10000.0
N_EXPERTS = 16
D_EXPERT = 2048
NORM_EPS = 1e-6
NEG_INF = -1e30

LANES = 128
BF16_ROWS = 16
HALO = 8
TOK_TILE = 256
SOFTMAX_ROWS = 32
GATHER_EXPERTS = 4
COMBINE_GROUP = 4
UP_ROW_SPLIT = 2
SMALL_WINDOW = 64
VMEM_LIMIT = 56 * 1024 * 1024

NT_DIMS = (((1,), (1,)), ((), ()))


def _cparams(sem, vmem=None):
    return pltpu.CompilerParams(dimension_semantics=sem, vmem_limit_bytes=vmem or VMEM_LIMIT)


def _norm_mod(x, g, shift, scale):
    ms = jnp.mean(x * x, axis=-1, keepdims=True)
    return (x * lax.rsqrt(ms + NORM_EPS)) * g * (1.0 + scale) + shift


def _ada_kernel(cc_ref, w_ref, b_ref, o_ref):
    cc = cc_ref[...]
    s = cc * jax.nn.sigmoid(cc)
    o_ref[0] = jnp.dot(s, w_ref[0], preferred_element_type=F32,
                       precision=lax.Precision.HIGHEST) + b_ref[0]


def _ada(cc, w_ada, b_ada):
    depth, d, six_d = w_ada.shape
    tn = 1536
    return pl.pallas_call(
        _ada_kernel,
        grid=(depth, six_d // tn),
        in_specs=[
            pl.BlockSpec((8, d), lambda i, j: (0, 0)),
            pl.BlockSpec((1, d, tn), lambda i, j: (i, 0, j)),
            pl.BlockSpec((1, 1, tn), lambda i, j: (i, 0, j)),
        ],
        out_specs=pl.BlockSpec((1, 8, tn), lambda i, j: (i, 0, j)),
        out_shape=jax.ShapeDtypeStruct((depth, 8, six_d), F32),
        compiler_params=_cparams(("arbitrary", "arbitrary")),
        name="ada",
    )(cc, w_ada, b_ada.reshape(depth, 1, six_d))


def _route(x1, n2g, sh2, sc2, wr_hi, wr_lo):
    h2 = _norm_mod(x1, n2g, sh2, sc2)
    h_hi = h2.astype(BF16)
    h_lo = (h2 - h_hi.astype(F32)).astype(BF16)
    lg = (lax.dot_general(wr_hi, h_hi, NT_DIMS, preferred_element_type=F32)
          + lax.dot_general(wr_hi, h_lo, NT_DIMS, preferred_element_type=F32)
          + lax.dot_general(wr_lo, h_hi, NT_DIMS, preferred_element_type=F32))
    m = jnp.max(lg, axis=0, keepdims=True)
    ex = jnp.exp(lg - m)
    aff = ex / jnp.sum(ex, axis=0, keepdims=True)
    return h_hi, aff


def _pool_kernel(x_ref, xp_ref, xn_ref, n1g_ref, sh1_ref, sc1_ref, g1_ref, pw_ref, ps_ref,
                 n2g_ref, sh2_ref, sc2_ref, wrh_ref, wrl_ref,
                 x1_ref, h2_ref, aff_ref, hs_ref, *, n_tokens):
    t = pl.program_id(1)
    nt = pl.num_programs(1)
    tt = x_ref.shape[1]
    n1g, sh1, sc1 = n1g_ref[...], sh1_ref[0], sc1_ref[0]
    x = x_ref[0]
    h = _norm_mod(x, n1g, sh1, sc1)
    hp = _norm_mod(xp_ref[0], n1g, sh1, sc1) * (t > 0).astype(F32)
    hn = _norm_mod(xn_ref[0], n1g, sh1, sc1) * (t < nt - 1).astype(F32)
    hs_ref[0:HALO, :] = hp
    hs_ref[HALO:HALO + tt, :] = h
    hs_ref[HALO + tt:HALO + tt + HALO, :] = hn
    pos = t * tt + lax.broadcasted_iota(I32, (tt, 1), 0)
    ys = []
    for g, w in enumerate(POOL_WINDOWS):
        half = w // 2
        c0 = g * POOL_GROUP_DIM
        s = hs_ref[HALO - half:HALO - half + tt, c0:c0 + POOL_GROUP_DIM]
        for k in range(1, w):
            s = s + hs_ref[HALO - half + k:HALO - half + k + tt, c0:c0 + POOL_GROUP_DIM]
        lo = jnp.clip(pos - half, 0, n_tokens)
        hi = jnp.clip(pos + half, 0, n_tokens)
        p = s / (hi - lo).astype(F32) - h[:, c0:c0 + POOL_GROUP_DIM]
        ys.append(jnp.dot(p.astype(BF16), pw_ref[g], preferred_element_type=F32))
    y = jnp.concatenate(ys, axis=1) * ps_ref[...]
    x1 = x + g1_ref[0] * y
    x1_ref[0] = x1
    h2, aff = _route(x1, n2g_ref[...], sh2_ref[0], sc2_ref[0], wrh_ref[...], wrl_ref[...])
    h2_ref[0] = h2
    aff_ref[0] = aff


def _row_spec(d):
    return pl.BlockSpec((1, d), lambda b, t: (0, 0))


def _mod_spec(d):
    return pl.BlockSpec((1, 1, d), lambda b, t: (b, 0, 0))


def _pool_layer(x, n1g, sh1, sc1, g1, pw, ps, n2g, sh2, sc2, wrh, wrl, tt):
    bsz, n, d = x.shape
    nb8 = n // HALO
    r = tt // HALO
    kern = functools.partial(_pool_kernel, n_tokens=n)
    return pl.pallas_call(
        kern,
        grid=(bsz, n // tt),
        in_specs=[
            pl.BlockSpec((1, tt, d), lambda b, t: (b, t, 0)),
            pl.BlockSpec((1, HALO, d), lambda b, t: (b, jnp.maximum(t * r - 1, 0), 0)),
            pl.BlockSpec((1, HALO, d), lambda b, t: (b, jnp.minimum((t + 1) * r, nb8 - 1), 0)),
            _row_spec(d), _mod_spec(d), _mod_spec(d), _mod_spec(d),
            pl.BlockSpec((4, POOL_GROUP_DIM, POOL_GROUP_DIM), lambda b, t: (0, 0, 0)),
            _row_spec(d),
            _row_spec(d), _mod_spec(d), _mod_spec(d),
            pl.BlockSpec((N_EXPERTS, d), lambda b, t: (0, 0)),
            pl.BlockSpec((N_EXPERTS, d), lambda b, t: (0, 0)),
        ],
        out_specs=[
            pl.BlockSpec((1, tt, d), lambda b, t: (b, t, 0)),
            pl.BlockSpec((1, tt, d), lambda b, t: (b, t, 0)),
            pl.BlockSpec((1, N_EXPERTS, tt), lambda b, t: (b, 0, t)),
        ],
        out_shape=[
            jax.ShapeDtypeStruct((bsz, n, d), F32),
            jax.ShapeDtypeStruct((bsz, n, d), BF16),
            jax.ShapeDtypeStruct((bsz, N_EXPERTS, n), F32),
        ],
        scratch_shapes=[pltpu.VMEM((tt + 2 * HALO, d), F32)],
        compiler_params=_cparams(("arbitrary", "arbitrary")),
        name="pool_route",
    )(x, x, x, n1g, sh1, sc1, g1, pw, ps, n2g, sh2, sc2, wrh, wrl)


def _head_rms(v, bavg, gain):
    sq = v * v
    sq_hi = sq.astype(BF16)
    sq_lo = (sq - sq_hi.astype(F32)).astype(BF16)
    ms = (jnp.dot(sq_hi, bavg, preferred_element_type=F32)
          + jnp.dot(sq_lo, bavg, preferred_element_type=F32))
    return v * lax.rsqrt(ms + NORM_EPS) * gain


def _rope(v, cos, sin_signed):
    lane = lax.broadcasted_iota(I32, v.shape, 1)
    fwd = pltpu.roll(v, 32, 1)
    bwd = pltpu.roll(v, 96, 1)
    partner = jnp.where((lane % HEAD_DIM) < HEAD_DIM // 2, bwd, fwd)
    return v * cos + partner * sin_signed


def _qkv_kernel(x_ref, n1g_ref, sh1_ref, sc1_ref, w_ref, bavg_ref, qg_ref, kg_ref,
                cos_ref, sin_ref, q_ref, k_ref, v_ref):
    _qkv_body(x_ref[0], n1g_ref, sh1_ref, sc1_ref, w_ref, bavg_ref, qg_ref, kg_ref,
              cos_ref, sin_ref, q_ref, k_ref, v_ref)


def _qkv_body(x, n1g_ref, sh1_ref, sc1_ref, w_ref, bavg_ref, qg_ref, kg_ref,
              cos_ref, sin_ref, q_ref, k_ref, v_ref):
    h = _norm_mod(x, n1g_ref[...], sh1_ref[0], sc1_ref[0]).astype(BF16)
    qkv = jnp.dot(h, w_ref[...], preferred_element_type=F32)
    nq = N_Q_HEADS * HEAD_DIM
    nk = N_KV_HEADS * HEAD_DIM
    bavg = bavg_ref[...]
    cos, sin = cos_ref[...], sin_ref[...]
    scale = HEAD_DIM ** -0.5
    for c in range(nq // 256):
        qn = _head_rms(qkv[:, c * 256:(c + 1) * 256], bavg, qg_ref[...])
        for s in range(2):
            qr = _rope(qn[:, s * LANES:(s + 1) * LANES], cos, sin) * scale
            for hh in range(2):
                head = c * 4 + s * 2 + hh
                q_ref[0, head] = qr[:, hh * HEAD_DIM:(hh + 1) * HEAD_DIM].astype(BF16)
    kn = _head_rms(qkv[:, nq:nq + nk], bavg, kg_ref[...])
    v = qkv[:, nq + nk:]
    for s in range(2):
        kr = _rope(kn[:, s * LANES:(s + 1) * LANES], cos, sin)
        for hh in range(2):
            head = s * 2 + hh
            k_ref[0, head] = kr[:, hh * HEAD_DIM:(hh + 1) * HEAD_DIM].astype(BF16)
            v_ref[0, head] = v[:, head * HEAD_DIM:(head + 1) * HEAD_DIM].astype(BF16)


def _qkv_layer(x, n1g, sh1, sc1, w, bavg, qg, kg, cos, sin, tt):
    bsz, n, d = x.shape
    return pl.pallas_call(
        _qkv_kernel,
        grid=(bsz, n // tt),
        in_specs=[
            pl.BlockSpec((1, tt, d), lambda b, t: (b, t, 0)),
            _row_spec(d), _mod_spec(d), _mod_spec(d),
            pl.BlockSpec(w.shape, lambda b, t: (0, 0)),
            pl.BlockSpec((256, 256), lambda b, t: (0, 0)),
            _row_spec(256), _row_spec(256),
            pl.BlockSpec((tt, LANES), lambda b, t: (t, 0)),
            pl.BlockSpec((tt, LANES), lambda b, t: (t, 0)),
        ],
        out_specs=[
            pl.BlockSpec((1, N_Q_HEADS, tt, HEAD_DIM), lambda b, t: (b, 0, t, 0)),
            pl.BlockSpec((1, N_KV_HEADS, tt, HEAD_DIM), lambda b, t: (b, 0, t, 0)),
            pl.BlockSpec((1, N_KV_HEADS, tt, HEAD_DIM), lambda b, t: (b, 0, t, 0)),
        ],
        out_shape=[
            jax.ShapeDtypeStruct((bsz, N_Q_HEADS, n, HEAD_DIM), BF16),
            jax.ShapeDtypeStruct((bsz, N_KV_HEADS, n, HEAD_DIM), BF16),
            jax.ShapeDtypeStruct((bsz, N_KV_HEADS, n, HEAD_DIM), BF16),
        ],
        compiler_params=_cparams(("arbitrary", "arbitrary")),
        name="qkv",
    )(x, n1g, sh1, sc1, w, bavg, qg, kg, cos, sin)


def _attn_kernel(sink_ref, q_ref, *refs, band):
    if band:
        kp_ref, kc_ref, kn_ref, vp_ref, vc_ref, vn_ref, kx_ref, vx_ref, bias_ref, o_ref, s_all, p_all = refs
    else:
        kx_ref, vx_ref, o_ref, s_all, p_all = refs
    tq = q_ref.shape[2]
    n_chunks = Q_PER_KV * tq // SOFTMAX_ROWS
    for g in range(N_KV_HEADS):
        if band:
            kg = jnp.concatenate([kp_ref[0, g], kc_ref[0, g], kn_ref[0, g], kx_ref[0, g]], axis=0)
            vg = jnp.concatenate([vp_ref[0, g], vc_ref[0, g], vn_ref[0, g], vx_ref[0, g]], axis=0)
        else:
            kg = kx_ref[0, g]
            vg = vx_ref[0, g]
        vext = jnp.concatenate([vg, jnp.ones_like(vg)], axis=1)
        qg = q_ref[0, g * Q_PER_KV:(g + 1) * Q_PER_KV].reshape(Q_PER_KV * tq, HEAD_DIM)
        s_ref, p_ref = s_all.at[g], p_all.at[g]
        s_ref[...] = lax.dot_general(qg, kg, NT_DIMS, preferred_element_type=F32)
        sink_terms = []
        for r in range(n_chunks):
            rs = slice(r * SOFTMAX_ROWS, (r + 1) * SOFTMAX_ROWS)
            sc = s_ref[rs, :]
            if band:
                sc = sc + bias_ref[0, rs, :]
            sk = sink_ref[g * Q_PER_KV + (r * SOFTMAX_ROWS) // tq]
            m = jnp.maximum(jnp.max(sc, axis=-1, keepdims=True), sk)
            p_ref[rs, :] = jnp.exp(sc - m).astype(BF16)
            sink_terms.append(jnp.exp(sk - m))
        oe = jnp.dot(p_ref[...], vext, preferred_element_type=F32)
        den = oe[:, HEAD_DIM:HEAD_DIM + 1] + jnp.concatenate(sink_terms, axis=0)
        o = (oe[:, :HEAD_DIM] / den).astype(o_ref.dtype)
        for hh in range(Q_PER_KV):
            head = g * Q_PER_KV + hh
            o_ref[0, :, head * HEAD_DIM:(head + 1) * HEAD_DIM] = o[hh * tq:(hh + 1) * tq]


def _band_bias(lc):
    nband = BLOCK_Q + 2 * WINDOW
    row = jnp.arange(Q_PER_KV * BLOCK_Q)[:, None] % BLOCK_Q
    col = jnp.arange(nband + lc)[None, :]
    dlt = col - row
    inband = (dlt >= 0) & (dlt <= 2 * WINDOW)
    variants = [inband & (col >= WINDOW), inband, inband & (col < WINDOW + BLOCK_Q)]
    return jnp.stack([jnp.where(v | (col >= nband), 0.0, NEG_INF) for v in variants]).astype(F32)


def _attn_layer(sink, q, k, v, kx, vx, band):
    bsz, _, n, _ = q.shape
    lc = kx.shape[2]
    tq = BLOCK_Q if band else n
    nb = n // tq
    n_keys = (BLOCK_Q + 2 * WINDOW + lc) if band else lc
    qspec = pl.BlockSpec((1, N_Q_HEADS, tq, HEAD_DIM), lambda b, t: (b, 0, t, 0))
    xspec = pl.BlockSpec((1, N_KV_HEADS, lc, HEAD_DIM), lambda b, t: (b, 0, 0, 0))
    if band:
        prev = pl.BlockSpec((1, N_KV_HEADS, tq, HEAD_DIM), lambda b, t: (b, 0, jnp.maximum(t - 1, 0), 0))
        cur = pl.BlockSpec((1, N_KV_HEADS, tq, HEAD_DIM), lambda b, t: (b, 0, t, 0))
        nxt = pl.BlockSpec((1, N_KV_HEADS, tq, HEAD_DIM), lambda b, t: (b, 0, jnp.minimum(t + 1, nb - 1), 0))
        bias = _band_bias(lc)
        bspec = pl.BlockSpec((1,) + bias.shape[1:],
                             lambda b, t: (jnp.where(t == 0, 0, jnp.where(t == nb - 1, 2, 1)), 0, 0))
        in_specs = [qspec, prev, cur, nxt, prev, cur, nxt, xspec, xspec, bspec]
        args = (q, k, k, k, v, v, v, kx, vx, bias)
    else:
        in_specs = [qspec, xspec, xspec]
        args = (q, kx, vx)
    return pl.pallas_call(
        functools.partial(_attn_kernel, band=band),
        grid=(bsz, nb),
        in_specs=[pl.BlockSpec(memory_space=pltpu.SMEM)] + in_specs,
        out_specs=pl.BlockSpec((1, tq, N_Q_HEADS * HEAD_DIM), lambda b, t: (b, t, 0)),
        out_shape=jax.ShapeDtypeStruct((bsz, n, N_Q_HEADS * HEAD_DIM), BF16),
        scratch_shapes=[pltpu.VMEM((N_KV_HEADS, Q_PER_KV * tq, n_keys), F32),
                        pltpu.VMEM((N_KV_HEADS, Q_PER_KV * tq, n_keys), BF16)],
        compiler_params=_cparams(("arbitrary", "arbitrary")),
        name="attn_band" if band else "attn_ctx",
    )(sink, *args)


def _oproj_kernel(o_ref, x_ref, wo_ref, g1_ref, n2g_ref, sh2_ref, sc2_ref, wrh_ref, wrl_ref,
                  x1_ref, h2_ref, aff_ref):
    y = jnp.dot(o_ref[0], wo_ref[...], preferred_element_type=F32)
    x1 = x_ref[0] + g1_ref[0] * y
    x1_ref[0] = x1
    h2, aff = _route(x1, n2g_ref[...], sh2_ref[0], sc2_ref[0], wrh_ref[...], wrl_ref[...])
    h2_ref[0] = h2
    aff_ref[0] = aff


def _oproj_layer(o, x, wo, g1, n2g, sh2, sc2, wrh, wrl, tt):
    bsz, n, d = x.shape
    return pl.pallas_call(
        _oproj_kernel,
        grid=(bsz, n // tt),
        in_specs=[
            pl.BlockSpec((1, tt, d), lambda b, t: (b, t, 0)),
            pl.BlockSpec((1, tt, d), lambda b, t: (b, t, 0)),
            pl.BlockSpec((d, d), lambda b, t: (0, 0)),
            _mod_spec(d), _row_spec(d), _mod_spec(d), _mod_spec(d),
            pl.BlockSpec((N_EXPERTS, d), lambda b, t: (0, 0)),
            pl.BlockSpec((N_EXPERTS, d), lambda b, t: (0, 0)),
        ],
        out_specs=[
            pl.BlockSpec((1, tt, d), lambda b, t: (b, t, 0)),
            pl.BlockSpec((1, tt, d), lambda b, t: (b, t, 0)),
            pl.BlockSpec((1, N_EXPERTS, tt), lambda b, t: (b, 0, t)),
        ],
        out_shape=[
            jax.ShapeDtypeStruct((bsz, n, d), F32),
            jax.ShapeDtypeStruct((bsz, n, d), BF16),
            jax.ShapeDtypeStruct((bsz, N_EXPERTS, n), F32),
        ],
        compiler_params=_cparams(("arbitrary", "arbitrary")),
        name="oproj_route",
    )(o, x, wo, g1, n2g, sh2, sc2, wrh, wrl)


def _prefix_counts(mask, n_chunks):
    ne = mask.shape[0]
    m2 = mask.reshape(ne * n_chunks, LANES).astype(BF16)
    ri = lax.broadcasted_iota(I32, (LANES, LANES), 0)
    ci = lax.broadcasted_iota(I32, (LANES, LANES), 1)
    upper = (ri <= ci).astype(BF16)
    within = jnp.dot(m2, upper, preferred_element_type=F32).reshape(ne, n_chunks, LANES)
    tot = jnp.dot(m2, jnp.ones((LANES, LANES), BF16), preferred_element_type=F32).reshape(ne, n_chunks, LANES)
    rc = lax.broadcasted_iota(I32, (n_chunks, n_chunks), 0)
    cc = lax.broadcasted_iota(I32, (n_chunks, n_chunks), 1)
    strict_lower = (cc < rc).astype(F32)
    off = jnp.stack([jnp.dot(strict_lower, tot[e], preferred_element_type=F32) for e in range(ne)])
    return within + off, off


def _select_kernel(a_ref, slot_ref, offc_ref, *, cap):
    a = a_ref[0]
    ne, n_chunks, _ = a.shape
    bits = lax.bitcast_convert_type(a, I32)

    def count(pred):
        c = jnp.sum(pred.astype(F32), axis=1, keepdims=True)
        return jnp.sum(c, axis=2, keepdims=True)

    def body(i, thr):
        cand = thr | lax.shift_left(jnp.int32(1), 30 - i)
        return jnp.where(count(bits >= cand) >= cap, cand, thr)

    thr = lax.fori_loop(0, 31, body, jnp.zeros((ne, 1, 1), I32))
    gt = bits > thr
    eq = bits == thr
    need = cap - count(gt)
    eq_rank, _ = _prefix_counts(eq.astype(F32), n_chunks)
    sel = gt | (eq & (eq_rank <= need))
    sel_f = sel.astype(F32)
    incl, off = _prefix_counts(sel_f, n_chunks)
    slot_ref[0] = jnp.where(sel, (incl - 1.0).astype(I32), -1)
    offc_ref[0] = off.astype(I32)


def _select(aff4, cap):
    bsz, ne, n_chunks, cw = aff4.shape
    spec = pl.BlockSpec((1, ne, n_chunks, cw), lambda b: (b, 0, 0, 0))
    return pl.pallas_call(
        functools.partial(_select_kernel, cap=cap),
        grid=(bsz,),
        in_specs=[spec],
        out_specs=[spec, spec],
        out_shape=[jax.ShapeDtypeStruct(aff4.shape, I32), jax.ShapeDtypeStruct(aff4.shape, I32)],
        compiler_params=_cparams(("arbitrary",)),
        name="select",
    )(aff4)


def _window(cap, tile):
    return min(tile + BF16_ROWS, cap)


def _window_start(off, cap, win):
    start = lax.shift_left(lax.shift_right_logical(off, 4), 4)
    return pl.multiple_of(jnp.minimum(start, cap - win), BF16_ROWS)


def _gather_kernel(offs_ref, small_ref, h2_ref, sl_ref, af_ref, xs_ref, val_ref, *, cap, tile, n_tiles):
    b, eg, s = pl.program_id(0), pl.program_id(1), pl.program_id(2)
    ge = sl_ref.shape[1]
    sub = h2_ref.shape[1] // tile
    full = _window(cap, tile)
    small = min(SMALL_WINDOW, full)

    @pl.when(s == 0)
    def _():
        xs_ref[...] = jnp.zeros_like(xs_ref)
        val_ref[...] = jnp.zeros_like(val_ref)

    def tile_rows(u, win):
        ts = slice(u * tile, (u + 1) * tile)
        starts, hits = [], []
        for j in range(ge):
            off = offs_ref[(b * N_EXPERTS + eg * ge + j) * n_tiles + s * sub + u]
            start = _window_start(off, cap, win)
            rows = start + lax.broadcasted_iota(I32, (win, tile), 0)
            starts.append(start)
            hits.append(rows == sl_ref[0, j, :, ts])
        stacked = jnp.concatenate([h.astype(BF16) for h in hits], axis=0)
        res = jnp.dot(stacked, h2_ref[0, ts, :], preferred_element_type=F32)
        for j in range(ge):
            dst = pl.ds(starts[j], win)
            xs_ref[0, j, dst, :] = (xs_ref[0, j, dst, :].astype(F32) + res[j * win:(j + 1) * win]).astype(BF16)
            val_ref[0, j, dst, :] += jnp.sum(jnp.where(hits[j], af_ref[0, j, :, ts], 0.0), axis=1, keepdims=True)

    for u in range(sub):
        if small == full:
            tile_rows(u, full)
        else:
            fits = small_ref[(b * pl.num_programs(1) + eg) * n_tiles + s * sub + u] > 0
            pl.when(fits)(functools.partial(tile_rows, u, small))
            pl.when(jnp.logical_not(fits))(functools.partial(tile_rows, u, full))


def _gather(offs, small, h2, slot, aff, cap, tile):
    bsz, n, d = h2.shape
    tg = min(n, 1024)
    n_tiles = n // tile
    ge = GATHER_EXPERTS
    kern = functools.partial(_gather_kernel, cap=cap, tile=tile, n_tiles=n_tiles)
    return pl.pallas_call(
        kern,
        grid_spec=pltpu.PrefetchScalarGridSpec(
            num_scalar_prefetch=2,
            grid=(bsz, N_EXPERTS // ge, n // tg),
            in_specs=[
                pl.BlockSpec((1, tg, d), lambda b, e, s, offs, small: (b, s, 0)),
                pl.BlockSpec((1, ge, 1, tg), lambda b, e, s, offs, small: (b, e, 0, s)),
                pl.BlockSpec((1, ge, 1, tg), lambda b, e, s, offs, small: (b, e, 0, s)),
            ],
            out_specs=[
                pl.BlockSpec((1, ge, cap, d), lambda b, e, s, offs, small: (b, e, 0, 0)),
                pl.BlockSpec((1, ge, cap, 1), lambda b, e, s, offs, small: (b, e, 0, 0)),
            ],
        ),
        out_shape=[
            jax.ShapeDtypeStruct((bsz, N_EXPERTS, cap, d), BF16),
            jax.ShapeDtypeStruct((bsz, N_EXPERTS, cap, 1), F32),
        ],
        compiler_params=_cparams(("arbitrary", "arbitrary", "arbitrary")),
        name="gather",
    )(offs, small, h2, slot.reshape(bsz, N_EXPERTS, 1, n), aff.reshape(bsz, N_EXPERTS, 1, n))


def _up_kernel(*refs, has_ctx):
    if has_ctx:
        xl_ref, xc_ref, wg_ref, wu_ref, h_ref, xs_ref = refs
        cl = xl_ref.shape[2]

        @pl.when(pl.program_id(2) == 0)
        def _():
            xs_ref[0:cl, :] = xl_ref[0, 0]
            xs_ref[cl:, :] = xc_ref[0, 0]

        xs_of = lambda rs: xs_ref[rs, :]
    else:
        xl_ref, wg_ref, wu_ref, h_ref = refs
        xs_of = lambda rs: xl_ref[0, 0, rs, :]
    wg = wg_ref[0, 0].astype(BF16)
    wu = wu_ref[0, 0].astype(BF16)
    rows = h_ref.shape[2]
    for r in range(UP_ROW_SPLIT):
        rs = slice(r * rows // UP_ROW_SPLIT, (r + 1) * rows // UP_ROW_SPLIT)
        xs = xs_of(rs)
        a = jnp.dot(xs, wg, preferred_element_type=F32)
        u = jnp.dot(xs, wu, preferred_element_type=F32)
        h_ref[0, 0, rs, :] = (a * jax.nn.sigmoid(a) * u).astype(BF16)


def _down_kernel(*refs, has_ctx):
    if has_ctx:
        h_ref, vl_ref, vc_ref, wd_ref, y_ref = refs
    else:
        h_ref, vl_ref, wd_ref, y_ref = refs
    cl = vl_ref.shape[2]
    res = jnp.dot(h_ref[0, 0], wd_ref[0, 0].astype(BF16), preferred_element_type=F32)
    y_ref[0, 0, 0:cl, :] = (res[0:cl] * vl_ref[0, 0]).astype(BF16)
    if has_ctx:
        y_ref[0, 0, cl:, :] = (res[cl:] * vc_ref[0, 0]).astype(BF16)


def _mlp(layer, xl, vl, xc, vc, wg, wu, wd):
    bsz, ne, cl, d = xl.shape
    fdim = wg.shape[-1]
    has_ctx = xc is not None
    cc = xc.shape[2] if has_ctx else 0
    rows = cl + cc
    fc = 1024
    dn = 512

    xspecs = [pl.BlockSpec((1, 1, cl, d), lambda e, b, f: (b, e, 0, 0))]
    if has_ctx:
        xspecs.append(pl.BlockSpec((1, 1, cc, d), lambda e, b, f: (b, e, 0, 0)))
    hid = pl.pallas_call(
        functools.partial(_up_kernel, has_ctx=has_ctx),
        grid=(ne, bsz, fdim // fc),
        in_specs=xspecs + [
            pl.BlockSpec((1, 1, d, fc), lambda e, b, f: (layer, e, 0, f)),
            pl.BlockSpec((1, 1, d, fc), lambda e, b, f: (layer, e, 0, f)),
        ],
        out_specs=pl.BlockSpec((1, 1, rows, fc), lambda e, b, f: (b, e, 0, f)),
        out_shape=jax.ShapeDtypeStruct((bsz, ne, rows, fdim), BF16),
        scratch_shapes=[pltpu.VMEM((rows, d), BF16)] if has_ctx else [],
        compiler_params=_cparams(("arbitrary", "arbitrary", "arbitrary")),
        name="experts_up",
    )(*((xl, xc) if has_ctx else (xl,)), wg, wu)

    vspecs = [pl.BlockSpec((1, 1, cl, 1), lambda e, b, j: (b, e, 0, 0))]
    if has_ctx:
        vspecs.append(pl.BlockSpec((1, 1, cc, 1), lambda e, b, j: (b, e, 0, 0)))
    return pl.pallas_call(
        functools.partial(_down_kernel, has_ctx=has_ctx),
        grid=(ne, bsz, d // dn),
        in_specs=[pl.BlockSpec((1, 1, rows, fdim), lambda e, b, j: (b, e, 0, 0))] + vspecs + [
            pl.BlockSpec((1, 1, fdim, dn), lambda e, b, j: (layer, e, 0, j)),
        ],
        out_specs=pl.BlockSpec((1, 1, rows, dn), lambda e, b, j: (b, e, 0, j)),
        out_shape=jax.ShapeDtypeStruct((bsz, ne, rows, d), BF16),
        compiler_params=_cparams(("arbitrary", "arbitrary", "arbitrary")),
        name="experts_down",
    )(hid, *((vl, vc) if has_ctx else (vl,)), wd)


def _combine_window(cap, tile):
    return min(tile, cap)


def _combine_kernel(offs_ref, ovf_ref, small_ref, x1_ref, g2_ref, sl_ref, y_hbm, *rest,
                    cap, tile, n_tiles, base, fuse_qkv):
    if fuse_qkv:
        qkv_in, (o_ref, q_ref, k_ref, v_ref, ywin_ref, yext_ref, sems, semx) = rest[:9], rest[9:]
    else:
        o_ref, ywin_ref, yext_ref, sems, semx = rest
    b, t = pl.program_id(0), pl.program_id(1)
    n_steps = pl.num_programs(0) * n_tiles
    win = _combine_window(cap, tile)
    small = min(SMALL_WINDOW, win)
    step = b * n_tiles + t
    buf = lax.rem(step, 2)

    def start_of(bb, tt, e, w):
        return _window_start(offs_ref[(bb * N_EXPERTS + e) * n_tiles + tt], cap, w)

    def window_copy(bb, tt, bf, e, w):
        src = pl.multiple_of(base + start_of(bb, tt, e, w), BF16_ROWS)
        return pltpu.make_async_copy(y_hbm.at[bb, e, pl.ds(src, w), :],
                                     ywin_ref.at[bf, pl.ds(e * w, w), :], sems.at[bf, e])

    def by_window(stp, fn):
        if small == win:
            fn(win)
        else:
            fits = small_ref[stp] > 0
            pl.when(fits)(functools.partial(fn, small))
            pl.when(jnp.logical_not(fits))(functools.partial(fn, win))

    def start_windows(bb, tt, bf, w):
        for e in range(N_EXPERTS):
            window_copy(bb, tt, bf, e, w).start()

    @pl.when(step == 0)
    def _():
        by_window(step, functools.partial(start_windows, b, t, buf))

    @pl.when(step < n_steps - 1)
    def _():
        wrap = t == n_tiles - 1
        b_next = jnp.where(wrap, b + 1, b)
        t_next = jnp.where(wrap, 0, t + 1)
        by_window(step + 1, functools.partial(start_windows, b_next, t_next, 1 - buf))

    sl = sl_ref[0] + 1
    eye = (lax.broadcasted_iota(I32, (tile, tile), 0)
           == lax.broadcasted_iota(I32, (tile, tile), 1)).astype(BF16)
    d_hi = lax.shift_right_logical(sl, 6).astype(F32).astype(BF16)
    d_lo = (sl & 63).astype(F32).astype(BF16)
    slot_t = (lax.dot_general(eye, d_hi, NT_DIMS, preferred_element_type=F32) * 64.0
              + lax.dot_general(eye, d_lo, NT_DIMS, preferred_element_type=F32)).astype(I32) - 1

    def accumulate(w):
        lane = lax.broadcasted_iota(I32, (tile, w), 1)
        hits = [(lane == slot_t[:, e:e + 1] - start_of(b, t, e, w)).astype(BF16) for e in range(N_EXPERTS)]
        for e in range(N_EXPERTS):
            window_copy(b, t, buf, e, w).wait()
        acc = None
        for g in range(0, N_EXPERTS, COMBINE_GROUP):
            part = jnp.dot(jnp.concatenate(hits[g:g + COMBINE_GROUP], axis=1),
                           ywin_ref[buf, g * w:(g + COMBINE_GROUP) * w, :], preferred_element_type=F32)
            acc = part if acc is None else acc + part
        o_ref[0] = x1_ref[0] + g2_ref[0] * acc

    by_window(step, accumulate)

    if win < cap:
        @pl.when(ovf_ref[step] > 0)
        def _():
            def ext_start(e):
                s0 = start_of(b, t, e, win)
                return s0, pl.multiple_of(jnp.minimum(s0 + win, cap - BF16_ROWS), BF16_ROWS)

            def ext_copy(e):
                src = pl.multiple_of(base + ext_start(e)[1], BF16_ROWS)
                return pltpu.make_async_copy(y_hbm.at[b, e, pl.ds(src, BF16_ROWS), :],
                                             yext_ref.at[pl.ds(e * BF16_ROWS, BF16_ROWS), :], semx.at[e])

            for e in range(N_EXPERTS):
                ext_copy(e).start()
            lane_x = lax.broadcasted_iota(I32, (tile, BF16_ROWS), 1)
            hits_x = []
            for e in range(N_EXPERTS):
                ext_copy(e).wait()
                s0, s1 = ext_start(e)
                row = s1 + lane_x
                hits_x.append(((row == slot_t[:, e:e + 1]) & (row >= s0 + win)).astype(BF16))
            extra = jnp.dot(jnp.concatenate(hits_x, axis=1), yext_ref[...], preferred_element_type=F32)
            o_ref[0] += g2_ref[0] * extra

    if fuse_qkv:
        _qkv_body(o_ref[0], *qkv_in, q_ref, k_ref, v_ref)


def _combine(offs, ovf, small, x1, g2, slot, y, cap, tile, base, qkv_args=None):
    bsz, n, d = x1.shape
    n_tiles = n // tile
    win = _combine_window(cap, tile)
    fuse = qkv_args is not None
    kern = functools.partial(_combine_kernel, cap=cap, tile=tile, n_tiles=n_tiles, base=base, fuse_qkv=fuse)
    in_specs = [
        pl.BlockSpec((1, tile, d), lambda b, t, *_: (b, t, 0)),
        pl.BlockSpec((1, 1, d), lambda b, t, *_: (b, 0, 0)),
        pl.BlockSpec((1, N_EXPERTS, tile), lambda b, t, *_: (b, 0, t)),
        pl.BlockSpec(memory_space=pl.ANY),
    ]
    out_specs = [pl.BlockSpec((1, tile, d), lambda b, t, *_: (b, t, 0))]
    out_shape = [jax.ShapeDtypeStruct((bsz, n, d), F32)]
    if fuse:
        w = qkv_args[3]
        in_specs += [
            pl.BlockSpec((1, d), lambda b, t, *_: (0, 0)),
            pl.BlockSpec((1, 1, d), lambda b, t, *_: (b, 0, 0)),
            pl.BlockSpec((1, 1, d), lambda b, t, *_: (b, 0, 0)),
            pl.BlockSpec(w.shape, lambda b, t, *_: (0, 0)),
            pl.BlockSpec((256, 256), lambda b, t, *_: (0, 0)),
            pl.BlockSpec((1, 256), lambda b, t, *_: (0, 0)),
            pl.BlockSpec((1, 256), lambda b, t, *_: (0, 0)),
            pl.BlockSpec((tile, LANES), lambda b, t, *_: (t, 0)),
            pl.BlockSpec((tile, LANES), lambda b, t, *_: (t, 0)),
        ]
        for heads in (N_Q_HEADS, N_KV_HEADS, N_KV_HEADS):
            out_specs.append(pl.BlockSpec((1, heads, tile, HEAD_DIM), lambda b, t, *_: (b, 0, t, 0)))
            out_shape.append(jax.ShapeDtypeStruct((bsz, heads, n, HEAD_DIM), BF16))
    out = pl.pallas_call(
        kern,
        grid_spec=pltpu.PrefetchScalarGridSpec(
            num_scalar_prefetch=3,
            grid=(bsz, n_tiles),
            in_specs=in_specs,
            out_specs=out_specs,
            scratch_shapes=[pltpu.VMEM((2, N_EXPERTS * win, d), BF16),
                            pltpu.VMEM((N_EXPERTS * BF16_ROWS, d), BF16),
                            pltpu.SemaphoreType.DMA((2, N_EXPERTS)),
                            pltpu.SemaphoreType.DMA((N_EXPERTS,))],
        ),
        out_shape=out_shape,
        compiler_params=_cparams(("arbitrary", "arbitrary")),
        name="combine_qkv" if fuse else "combine",
    )(offs, ovf, small, x1, g2, slot, y, *(qkv_args or ()))
    return out if fuse else out[0]


def _deinterleave(w, n_heads):
    lead = w.shape[:-1]
    w = w.reshape(lead + (n_heads, HEAD_DIM // 2, 2))
    return jnp.swapaxes(w, -1, -2).reshape(lead + (n_heads * HEAD_DIM,))


def _rope_tables(n):
    rows = n // GRID_W
    row = jnp.repeat(jnp.arange(rows), GRID_W).astype(F32)
    col = jnp.tile(jnp.arange(GRID_W), rows).astype(F32)
    n_freq = HEAD_DIM // 4
    inv = ROPE_THETA ** (-jnp.arange(n_freq, dtype=F32) / n_freq)
    ang = jnp.concatenate([row[:, None] * inv, col[:, None] * inv], axis=-1)
    cos = jnp.tile(jnp.cos(ang), (1, 4))
    sin = jnp.sin(ang)
    sin_signed = jnp.tile(jnp.concatenate([-sin, sin], axis=-1), (1, 2))
    return cos, sin_signed


def _split_bf16(w):
    hi = w.astype(BF16)
    return hi, (w - hi.astype(F32)).astype(BF16)


def _route_select(aff, cap, tile):
    bsz, ne, n = aff.shape
    n_pad = max(n, HALO * LANES)
    a = aff if n_pad == n else jnp.pad(aff, ((0, 0), (0, 0), (0, n_pad - n)))
    slot4, offc4 = _select(a.reshape(bsz, ne, n_pad // LANES, LANES), cap)
    n_tiles = n // tile
    offs = offc4[:, :, ::tile // LANES, 0][:, :, :n_tiles]
    ends = jnp.concatenate([offs[:, :, 1:], jnp.full((bsz, ne, 1), cap, I32)], axis=2)

    def fits(win):
        return ends <= jnp.minimum((offs >> 4) << 4, cap - win) + win

    win_c = _combine_window(cap, tile)
    tables = {
        "offs": offs.reshape(-1),
        "combine_overflow": jnp.any(~fits(win_c), axis=1).astype(I32).reshape(-1),
        "combine_small": jnp.all(fits(min(SMALL_WINDOW, win_c)), axis=1).astype(I32).reshape(-1),
        "gather_small": jnp.all(
            fits(min(SMALL_WINDOW, _window(cap, tile))).reshape(bsz, ne // GATHER_EXPERTS, GATHER_EXPERTS, n_tiles),
            axis=2).astype(I32).reshape(-1),
    }
    return slot4.reshape(bsz, ne, n_pad)[:, :, :n], tables


def kernel(x, c, ctx, c_ctx, w_ada, b_ada, norm1_g, norm2_g, pool_w, pool_scale, attn_w_qkv, attn_w_o,
           attn_q_norm, attn_k_norm, attn_sink, router_w, exp_w_gate, exp_w_up, exp_w_down):
    bsz, n, d = x.shape
    lc = ctx.shape[1]
    cap = 2 * n // N_EXPERTS
    cap_c = 2 * lc // N_EXPERTS
    tile_c = min(TOK_TILE, lc)

    cc = jnp.zeros((8, d), F32).at[:bsz].set(c).at[bsz].set(c_ctx)
    mods = _ada(cc, w_ada, b_ada)

    cos, sin = _rope_tables(n)
    cos_c = jnp.ones((lc, LANES), F32)
    sin_c = jnp.zeros((lc, LANES), F32)
    blk = jnp.arange(256) // HEAD_DIM
    bavg = ((blk[:, None] == blk[None, :]).astype(F32) / HEAD_DIM).astype(BF16)

    def layer_mods(i):
        m = [mods[i, :, k * d:(k + 1) * d] for k in range(6)]
        lat = [v[:bsz, None, :] for v in m]
        con = [jnp.broadcast_to(v[bsz][None, None, :], (bsz, 1, d)) for v in m]
        return lat, con

    def qkv_params(i):
        j = i // 2
        nq = N_Q_HEADS * HEAD_DIM
        nk = N_KV_HEADS * HEAD_DIM
        wqkv = attn_w_qkv[j]
        wperm = jnp.concatenate([_deinterleave(wqkv[:, :nq], N_Q_HEADS),
                                 _deinterleave(wqkv[:, nq:nq + nk], N_KV_HEADS),
                                 wqkv[:, nq + nk:]], axis=1).astype(BF16)
        qg = jnp.tile(_deinterleave(attn_q_norm[j], 1), 4)[None, :]
        kg = jnp.tile(_deinterleave(attn_k_norm[j], 1), 4)[None, :]
        return wperm, bavg, qg, kg

    qkv_latent = None
    for i in range(DEPTH):
        update_ctx = i < DEPTH - 1
        j = i // 2
        is_pool = (i % 2) == 0
        (sh1, sc1, g1, sh2, sc2, g2), (csh1, csc1, cg1, csh2, csc2, cg2) = layer_mods(i)
        n1g = norm1_g[i][None, :]
        n2g = norm2_g[i][None, :]
        wrh, wrl = _split_bf16(router_w[i].T)

        if is_pool:
            pw = pool_w[j].astype(BF16)
            ps = pool_scale[j][None, :]
            x1, h2, aff = _pool_layer(x, n1g, sh1, sc1, g1, pw, ps, n2g, sh2, sc2, wrh, wrl, 512)
            if update_ctx:
                c1, hc2, affc = _pool_layer(ctx, n1g, csh1, csc1, cg1, pw, ps, n2g, csh2, csc2, wrh, wrl, lc)
        else:
            wperm, _, qg, kg = qkv_params(i)
            wo = attn_w_o[j].astype(BF16)
            sink = attn_sink[j]
            if qkv_latent is None:
                qkv_latent = _qkv_layer(x, n1g, sh1, sc1, wperm, bavg, qg, kg, cos, sin, 512)
            q, k, v = qkv_latent
            qkv_latent = None
            qc, kc, vc = _qkv_layer(ctx, n1g, csh1, csc1, wperm, bavg, qg, kg, cos_c, sin_c, lc)
            o = _attn_layer(sink, q, k, v, kc, vc, True)
            x1, h2, aff = _oproj_layer(o, x, wo, g1, n2g, sh2, sc2, wrh, wrl, 512)
            if update_ctx:
                oc = _attn_layer(sink, qc, None, None, kc, vc, False)
                c1, hc2, affc = _oproj_layer(oc, ctx, wo, cg1, n2g, csh2, csc2, wrh, wrl, lc)

        slot, tb = _route_select(aff, cap, TOK_TILE)
        xs, vals = _gather(tb["offs"], tb["gather_small"], h2, slot, aff, cap, TOK_TILE)
        if update_ctx:
            slot_c, tc = _route_select(affc, cap_c, tile_c)
            xs_c, vals_c = _gather(tc["offs"], tc["gather_small"], hc2, slot_c, affc, cap_c, tile_c)
            y = _mlp(i, xs, vals, xs_c, vals_c, exp_w_gate, exp_w_up, exp_w_down)
            ctx = _combine(tc["offs"], tc["combine_overflow"], tc["combine_small"], c1, cg2, slot_c, y,
                           cap_c, tile_c, cap)
        else:
            y = _mlp(i, xs, vals, None, None, exp_w_gate, exp_w_up, exp_w_down)
        next_is_attn = i + 1 < DEPTH and (i + 1) % 2 == 1
        if next_is_attn:
            (nsh1, nsc1, *_), _ = layer_mods(i + 1)
            args = (norm1_g[i + 1][None, :], nsh1, nsc1) + qkv_params(i + 1) + (cos, sin)
            x, *qkv_latent = _combine(tb["offs"], tb["combine_overflow"], tb["combine_small"], x1, g2, slot, y,
                                      cap, TOK_TILE, 0, args)
        else:
            x = _combine(tb["offs"], tb["combine_overflow"], tb["combine_small"], x1, g2, slot, y, cap, TOK_TILE, 0)
    return x
```

```python
import functools

import jax
import jax.numpy as jnp
from jax import lax
from jax.experimental import pallas as pl
from jax.experimental.pallas import tpu as pltpu

F32 = jnp.float32
BF16 = jnp.bfloat16
I32 = jnp.int32

D_MODEL = 1024
DEPTH = 4
GRID_W = 64
POOL_WINDOWS = (2, 4, 8, 16)
POOL_GROUP_DIM = 256
HEAD_DIM = 64
N_Q_HEADS = 16
N_KV_HEADS = 4
Q_PER_KV = 4
WINDOW = 128
BLOCK_Q = 128
ROPE_THETA = 10000.0
N_EXPERTS = 16
D_EXPERT = 2048
NORM_EPS = 1e-6
NEG_INF = -1e30

LANES = 128
BF16_ROWS = 16
HALO = 8
SOFTMAX_ROWS = 32
GATHER_EXPERTS = 4
COMBINE_GROUP = 4
UP_ROW_SPLIT = 2
GATHER_TILE, GATHER_SMALL = 512, 112
COMBINE_TILE, COMBINE_SMALL = 256, 64
VMEM_LIMIT = 56 * 1024 * 1024

NT_DIMS = (((1,), (1,)), ((), ()))


def _cparams(sem, vmem=None):
    return pltpu.CompilerParams(dimension_semantics=sem, vmem_limit_bytes=vmem or VMEM_LIMIT)


def _norm_mod(x, g, shift, scale):
    ms = jnp.mean(x * x, axis=-1, keepdims=True)
    return (x * lax.rsqrt(ms + NORM_EPS)) * g * (1.0 + scale) + shift


def _ada_kernel(cc_ref, w_ref, b_ref, o_ref):
    cc = cc_ref[...]
    s = cc * jax.nn.sigmoid(cc)
    o_ref[0] = jnp.dot(s, w_ref[0], preferred_element_type=F32,
                       precision=lax.Precision.HIGHEST) + b_ref[0]


def _ada(cc, w_ada, b_ada):
    depth, d, six_d = w_ada.shape
    tn = 1536
    return pl.pallas_call(
        _ada_kernel,
        grid=(depth, six_d // tn),
        in_specs=[
            pl.BlockSpec((8, d), lambda i, j: (0, 0)),
            pl.BlockSpec((1, d, tn), lambda i, j: (i, 0, j)),
            pl.BlockSpec((1, 1, tn), lambda i, j: (i, 0, j)),
        ],
        out_specs=pl.BlockSpec((1, 8, tn), lambda i, j: (i, 0, j)),
        out_shape=jax.ShapeDtypeStruct((depth, 8, six_d), F32),
        compiler_params=_cparams(("arbitrary", "arbitrary")),
        name="ada",
    )(cc, w_ada, b_ada.reshape(depth, 1, six_d))


def _route(x1, n2g, sh2, sc2, wr_hi, wr_lo):
    h2 = _norm_mod(x1, n2g, sh2, sc2)
    h_hi = h2.astype(BF16)
    h_lo = (h2 - h_hi.astype(F32)).astype(BF16)
    lg = (lax.dot_general(wr_hi, h_hi, NT_DIMS, preferred_element_type=F32)
          + lax.dot_general(wr_hi, h_lo, NT_DIMS, preferred_element_type=F32)
          + lax.dot_general(wr_lo, h_hi, NT_DIMS, preferred_element_type=F32))
    m = jnp.max(lg, axis=0, keepdims=True)
    ex = jnp.exp(lg - m)
    aff = ex / jnp.sum(ex, axis=0, keepdims=True)
    return h_hi, aff


def _pool_kernel(x_ref, xp_ref, xn_ref, n1g_ref, sh1_ref, sc1_ref, g1_ref, pw_ref, ps_ref,
                 n2g_ref, sh2_ref, sc2_ref, wrh_ref, wrl_ref,
                 x1_ref, h2_ref, aff_ref, hs_ref, *, n_tokens):
    t = pl.program_id(1)
    nt = pl.num_programs(1)
    tt = x_ref.shape[1]
    n1g, sh1, sc1 = n1g_ref[...], sh1_ref[0], sc1_ref[0]
    x = x_ref[0]
    h = _norm_mod(x, n1g, sh1, sc1)
    hp = _norm_mod(xp_ref[0], n1g, sh1, sc1) * (t > 0).astype(F32)
    hn = _norm_mod(xn_ref[0], n1g, sh1, sc1) * (t < nt - 1).astype(F32)
    hs_ref[0:HALO, :] = hp
    hs_ref[HALO:HALO + tt, :] = h
    hs_ref[HALO + tt:HALO + tt + HALO, :] = hn
    pos = t * tt + lax.broadcasted_iota(I32, (tt, 1), 0)
    ys = []
    for g, w in enumerate(POOL_WINDOWS):
        half = w // 2
        c0 = g * POOL_GROUP_DIM
        s = hs_ref[HALO - half:HALO - half + tt, c0:c0 + POOL_GROUP_DIM]
        for k in range(1, w):
            s = s + hs_ref[HALO - half + k:HALO - half + k + tt, c0:c0 + POOL_GROUP_DIM]
        lo = jnp.clip(pos - half, 0, n_tokens)
        hi = jnp.clip(pos + half, 0, n_tokens)
        p = s / (hi - lo).astype(F32) - h[:, c0:c0 + POOL_GROUP_DIM]
        ys.append(jnp.dot(p.astype(BF16), pw_ref[g], preferred_element_type=F32))
    y = jnp.concatenate(ys, axis=1) * ps_ref[...]
    x1 = x + g1_ref[0] * y
    x1_ref[0] = x1
    h2, aff = _route(x1, n2g_ref[...], sh2_ref[0], sc2_ref[0], wrh_ref[...], wrl_ref[...])
    h2_ref[0] = h2
    aff_ref[0] = aff


def _row_spec(d):
    return pl.BlockSpec((1, d), lambda b, t: (0, 0))


def _mod_spec(d):
    return pl.BlockSpec((1, 1, d), lambda b, t: (b, 0, 0))


def _pool_layer(x, n1g, sh1, sc1, g1, pw, ps, n2g, sh2, sc2, wrh, wrl, tt):
    bsz, n, d = x.shape
    nb8 = n // HALO
    r = tt // HALO
    kern = functools.partial(_pool_kernel, n_tokens=n)
    return pl.pallas_call(
        kern,
        grid=(bsz, n // tt),
        in_specs=[
            pl.BlockSpec((1, tt, d), lambda b, t: (b, t, 0)),
            pl.BlockSpec((1, HALO, d), lambda b, t: (b, jnp.maximum(t * r - 1, 0), 0)),
            pl.BlockSpec((1, HALO, d), lambda b, t: (b, jnp.minimum((t + 1) * r, nb8 - 1), 0)),
            _row_spec(d), _mod_spec(d), _mod_spec(d), _mod_spec(d),
            pl.BlockSpec((4, POOL_GROUP_DIM, POOL_GROUP_DIM), lambda b, t: (0, 0, 0)),
            _row_spec(d),
            _row_spec(d), _mod_spec(d), _mod_spec(d),
            pl.BlockSpec((N_EXPERTS, d), lambda b, t: (0, 0)),
            pl.BlockSpec((N_EXPERTS, d), lambda b, t: (0, 0)),
        ],
        out_specs=[
            pl.BlockSpec((1, tt, d), lambda b, t: (b, t, 0)),
            pl.BlockSpec((1, tt, d), lambda b, t: (b, t, 0)),
            pl.BlockSpec((1, N_EXPERTS, tt), lambda b, t: (b, 0, t)),
        ],
        out_shape=[
            jax.ShapeDtypeStruct((bsz, n, d), F32),
            jax.ShapeDtypeStruct((bsz, n, d), BF16),
            jax.ShapeDtypeStruct((bsz, N_EXPERTS, n), F32),
        ],
        scratch_shapes=[pltpu.VMEM((tt + 2 * HALO, d), F32)],
        compiler_params=_cparams(("arbitrary", "arbitrary")),
        name="pool_route",
    )(x, x, x, n1g, sh1, sc1, g1, pw, ps, n2g, sh2, sc2, wrh, wrl)


def _head_rms(v, bavg, gain):
    sq = v * v
    sq_hi = sq.astype(BF16)
    sq_lo = (sq - sq_hi.astype(F32)).astype(BF16)
    ms = (jnp.dot(sq_hi, bavg, preferred_element_type=F32)
          + jnp.dot(sq_lo, bavg, preferred_element_type=F32))
    return v * lax.rsqrt(ms + NORM_EPS) * gain


def _rope(v, cos, sin_signed):
    lane = lax.broadcasted_iota(I32, v.shape, 1)
    fwd = pltpu.roll(v, 32, 1)
    bwd = pltpu.roll(v, 96, 1)
    partner = jnp.where((lane % HEAD_DIM) < HEAD_DIM // 2, bwd, fwd)
    return v * cos + partner * sin_signed


def _qkv_kernel(x_ref, n1g_ref, sh1_ref, sc1_ref, w_ref, bavg_ref, qg_ref, kg_ref,
                cos_ref, sin_ref, q_ref, k_ref, v_ref):
    _qkv_body(x_ref[0], n1g_ref, sh1_ref, sc1_ref, w_ref, bavg_ref, qg_ref, kg_ref,
              cos_ref, sin_ref, q_ref, k_ref, v_ref)


def _qkv_body(x, n1g_ref, sh1_ref, sc1_ref, w_ref, bavg_ref, qg_ref, kg_ref,
              cos_ref, sin_ref, q_ref, k_ref, v_ref):
    h = _norm_mod(x, n1g_ref[...], sh1_ref[0], sc1_ref[0]).astype(BF16)
    qkv = jnp.dot(h, w_ref[...], preferred_element_type=F32)
    nq = N_Q_HEADS * HEAD_DIM
    nk = N_KV_HEADS * HEAD_DIM
    bavg = bavg_ref[...]
    cos, sin = cos_ref[...], sin_ref[...]
    scale = HEAD_DIM ** -0.5
    for c in range(nq // 256):
        qn = _head_rms(qkv[:, c * 256:(c + 1) * 256], bavg, qg_ref[...])
        for s in range(2):
            qr = _rope(qn[:, s * LANES:(s + 1) * LANES], cos, sin) * scale
            for hh in range(2):
                head = c * 4 + s * 2 + hh
                q_ref[0, head] = qr[:, hh * HEAD_DIM:(hh + 1) * HEAD_DIM].astype(BF16)
    kn = _head_rms(qkv[:, nq:nq + nk], bavg, kg_ref[...])
    v = qkv[:, nq + nk:]
    for s in range(2):
        kr = _rope(kn[:, s * LANES:(s + 1) * LANES], cos, sin)
        for hh in range(2):
            head = s * 2 + hh
            k_ref[0, head] = kr[:, hh * HEAD_DIM:(hh + 1) * HEAD_DIM].astype(BF16)
            v_ref[0, head] = v[:, head * HEAD_DIM:(head + 1) * HEAD_DIM].astype(BF16)


def _qkv_layer(x, n1g, sh1, sc1, w, bavg, qg, kg, cos, sin, tt):
    bsz, n, d = x.shape
    return pl.pallas_call(
        _qkv_kernel,
        grid=(bsz, n // tt),
        in_specs=[
            pl.BlockSpec((1, tt, d), lambda b, t: (b, t, 0)),
            _row_spec(d), _mod_spec(d), _mod_spec(d),
            pl.BlockSpec(w.shape, lambda b, t: (0, 0)),
            pl.BlockSpec((256, 256), lambda b, t: (0, 0)),
            _row_spec(256), _row_spec(256),
            pl.BlockSpec((tt, LANES), lambda b, t: (t, 0)),
            pl.BlockSpec((tt, LANES), lambda b, t: (t, 0)),
        ],
        out_specs=[
            pl.BlockSpec((1, N_Q_HEADS, tt, HEAD_DIM), lambda b, t: (b, 0, t, 0)),
            pl.BlockSpec((1, N_KV_HEADS, tt, HEAD_DIM), lambda b, t: (b, 0, t, 0)),
            pl.BlockSpec((1, N_KV_HEADS, tt, HEAD_DIM), lambda b, t: (b, 0, t, 0)),
        ],
        out_shape=[
            jax.ShapeDtypeStruct((bsz, N_Q_HEADS, n, HEAD_DIM), BF16),
            jax.ShapeDtypeStruct((bsz, N_KV_HEADS, n, HEAD_DIM), BF16),
            jax.ShapeDtypeStruct((bsz, N_KV_HEADS, n, HEAD_DIM), BF16),
        ],
        compiler_params=_cparams(("arbitrary", "arbitrary")),
        name="qkv",
    )(x, n1g, sh1, sc1, w, bavg, qg, kg, cos, sin)


def _attn_kernel(sink_ref, q_ref, *refs, band):
    if band:
        kp_ref, kc_ref, kn_ref, vp_ref, vc_ref, vn_ref, kx_ref, vx_ref, bias_ref, o_ref, s_all, p_all = refs
    else:
        kx_ref, vx_ref, o_ref, s_all, p_all = refs
    tq = q_ref.shape[2]
    n_chunks = Q_PER_KV * tq // SOFTMAX_ROWS
    for g in range(N_KV_HEADS):
        if band:
            kg = jnp.concatenate([kp_ref[0, g], kc_ref[0, g], kn_ref[0, g], kx_ref[0, g]], axis=0)
            vg = jnp.concatenate([vp_ref[0, g], vc_ref[0, g], vn_ref[0, g], vx_ref[0, g]], axis=0)
        else:
            kg = kx_ref[0, g]
            vg = vx_ref[0, g]
        vext = jnp.concatenate([vg, jnp.ones_like(vg)], axis=1)
        qg = q_ref[0, g * Q_PER_KV:(g + 1) * Q_PER_KV].reshape(Q_PER_KV * tq, HEAD_DIM)
        s_ref, p_ref = s_all.at[g], p_all.at[g]
        s_ref[...] = lax.dot_general(qg, kg, NT_DIMS, preferred_element_type=F32)
        sink_terms = []
        for r in range(n_chunks):
            rs = slice(r * SOFTMAX_ROWS, (r + 1) * SOFTMAX_ROWS)
            sc = s_ref[rs, :]
            if band:
                sc = sc + bias_ref[0, rs, :]
            sk = sink_ref[g * Q_PER_KV + (r * SOFTMAX_ROWS) // tq]
            m = jnp.maximum(jnp.max(sc, axis=-1, keepdims=True), sk)
            p_ref[rs, :] = jnp.exp(sc - m).astype(BF16)
            sink_terms.append(jnp.exp(sk - m))
        oe = jnp.dot(p_ref[...], vext, preferred_element_type=F32)
        den = oe[:, HEAD_DIM:HEAD_DIM + 1] + jnp.concatenate(sink_terms, axis=0)
        o = (oe[:, :HEAD_DIM] / den).astype(o_ref.dtype)
        for hh in range(Q_PER_KV):
            head = g * Q_PER_KV + hh
            o_ref[0, :, head * HEAD_DIM:(head + 1) * HEAD_DIM] = o[hh * tq:(hh + 1) * tq]


def _band_bias(lc):
    nband = BLOCK_Q + 2 * WINDOW
    row = jnp.arange(Q_PER_KV * BLOCK_Q)[:, None] % BLOCK_Q
    col = jnp.arange(nband + lc)[None, :]
    dlt = col - row
    inband = (dlt >= 0) & (dlt <= 2 * WINDOW)
    variants = [inband & (col >= WINDOW), inband, inband & (col < WINDOW + BLOCK_Q)]
    return jnp.stack([jnp.where(v | (col >= nband), 0.0, NEG_INF) for v in variants]).astype(F32)


def _attn_layer(sink, q, k, v, kx, vx, band):
    bsz, _, n, _ = q.shape
    lc = kx.shape[2]
    tq = BLOCK_Q if band else n
    nb = n // tq
    n_keys = (BLOCK_Q + 2 * WINDOW + lc) if band else lc
    qspec = pl.BlockSpec((1, N_Q_HEADS, tq, HEAD_DIM), lambda b, t: (b, 0, t, 0))
    xspec = pl.BlockSpec((1, N_KV_HEADS, lc, HEAD_DIM), lambda b, t: (b, 0, 0, 0))
    if band:
        prev = pl.BlockSpec((1, N_KV_HEADS, tq, HEAD_DIM), lambda b, t: (b, 0, jnp.maximum(t - 1, 0), 0))
        cur = pl.BlockSpec((1, N_KV_HEADS, tq, HEAD_DIM), lambda b, t: (b, 0, t, 0))
        nxt = pl.BlockSpec((1, N_KV_HEADS, tq, HEAD_DIM), lambda b, t: (b, 0, jnp.minimum(t + 1, nb - 1), 0))
        bias = _band_bias(lc)
        bspec = pl.BlockSpec((1,) + bias.shape[1:],
                             lambda b, t: (jnp.where(t == 0, 0, jnp.where(t == nb - 1, 2, 1)), 0, 0))
        in_specs = [qspec, prev, cur, nxt, prev, cur, nxt, xspec, xspec, bspec]
        args = (q, k, k, k, v, v, v, kx, vx, bias)
    else:
        in_specs = [qspec, xspec, xspec]
        args = (q, kx, vx)
    return pl.pallas_call(
        functools.partial(_attn_kernel, band=band),
        grid=(bsz, nb),
        in_specs=[pl.BlockSpec(memory_space=pltpu.SMEM)] + in_specs,
        out_specs=pl.BlockSpec((1, tq, N_Q_HEADS * HEAD_DIM), lambda b, t: (b, t, 0)),
        out_shape=jax.ShapeDtypeStruct((bsz, n, N_Q_HEADS * HEAD_DIM), BF16),
        scratch_shapes=[pltpu.VMEM((N_KV_HEADS, Q_PER_KV * tq, n_keys), F32),
                        pltpu.VMEM((N_KV_HEADS, Q_PER_KV * tq, n_keys), BF16)],
        compiler_params=_cparams(("arbitrary", "arbitrary")),
        name="attn_band" if band else "attn_ctx",
    )(sink, *args)


def _oproj_kernel(o_ref, x_ref, wo_ref, g1_ref, n2g_ref, sh2_ref, sc2_ref, wrh_ref, wrl_ref,
                  x1_ref, h2_ref, aff_ref):
    y = jnp.dot(o_ref[0], wo_ref[...], preferred_element_type=F32)
    x1 = x_ref[0] + g1_ref[0] * y
    x1_ref[0] = x1
    h2, aff = _route(x1, n2g_ref[...], sh2_ref[0], sc2_ref[0], wrh_ref[...], wrl_ref[...])
    h2_ref[0] = h2
    aff_ref[0] = aff


def _oproj_layer(o, x, wo, g1, n2g, sh2, sc2, wrh, wrl, tt):
    bsz, n, d = x.shape
    return pl.pallas_call(
        _oproj_kernel,
        grid=(bsz, n // tt),
        in_specs=[
            pl.BlockSpec((1, tt, d), lambda b, t: (b, t, 0)),
            pl.BlockSpec((1, tt, d), lambda b, t: (b, t, 0)),
            pl.BlockSpec((d, d), lambda b, t: (0, 0)),
            _mod_spec(d), _row_spec(d), _mod_spec(d), _mod_spec(d),
            pl.BlockSpec((N_EXPERTS, d), lambda b, t: (0, 0)),
            pl.BlockSpec((N_EXPERTS, d), lambda b, t: (0, 0)),
        ],
        out_specs=[
            pl.BlockSpec((1, tt, d), lambda b, t: (b, t, 0)),
            pl.BlockSpec((1, tt, d), lambda b, t: (b, t, 0)),
            pl.BlockSpec((1, N_EXPERTS, tt), lambda b, t: (b, 0, t)),
        ],
        out_shape=[
            jax.ShapeDtypeStruct((bsz, n, d), F32),
            jax.ShapeDtypeStruct((bsz, n, d), BF16),
            jax.ShapeDtypeStruct((bsz, N_EXPERTS, n), F32),
        ],
        compiler_params=_cparams(("arbitrary", "arbitrary")),
        name="oproj_route",
    )(o, x, wo, g1, n2g, sh2, sc2, wrh, wrl)


def _prefix_counts(mask, n_chunks):
    ne = mask.shape[0]
    m2 = mask.reshape(ne * n_chunks, LANES).astype(BF16)
    ri = lax.broadcasted_iota(I32, (LANES, LANES), 0)
    ci = lax.broadcasted_iota(I32, (LANES, LANES), 1)
    upper = (ri <= ci).astype(BF16)
    within = jnp.dot(m2, upper, preferred_element_type=F32).reshape(ne, n_chunks, LANES)
    tot = jnp.dot(m2, jnp.ones((LANES, LANES), BF16), preferred_element_type=F32).reshape(ne, n_chunks, LANES)
    rc = lax.broadcasted_iota(I32, (n_chunks, n_chunks), 0)
    cc = lax.broadcasted_iota(I32, (n_chunks, n_chunks), 1)
    strict_lower = (cc < rc).astype(F32)
    off = jnp.stack([jnp.dot(strict_lower, tot[e], preferred_element_type=F32) for e in range(ne)])
    return within + off, off


def _select_kernel(a_ref, slot_ref, offc_ref, *, cap):
    a = a_ref[0]
    ne, n_chunks, _ = a.shape
    bits = lax.bitcast_convert_type(a, I32)

    def count(pred):
        c = jnp.sum(pred.astype(F32), axis=1, keepdims=True)
        return jnp.sum(c, axis=2, keepdims=True)

    def body(i, thr):
        cand = thr | lax.shift_left(jnp.int32(1), 30 - i)
        return jnp.where(count(bits >= cand) >= cap, cand, thr)

    thr = lax.fori_loop(0, 31, body, jnp.zeros((ne, 1, 1), I32))
    gt = bits > thr
    eq = bits == thr
    need = cap - count(gt)
    eq_rank, _ = _prefix_counts(eq.astype(F32), n_chunks)
    sel = gt | (eq & (eq_rank <= need))
    sel_f = sel.astype(F32)
    incl, off = _prefix_counts(sel_f, n_chunks)
    slot_ref[0] = jnp.where(sel, (incl - 1.0).astype(I32), -1)
    offc_ref[0] = off.astype(I32)


def _select(aff4, cap):
    bsz, ne, n_chunks, cw = aff4.shape
    spec = pl.BlockSpec((1, ne, n_chunks, cw), lambda b: (b, 0, 0, 0))
    return pl.pallas_call(
        functools.partial(_select_kernel, cap=cap),
        grid=(bsz,),
        in_specs=[spec],
        out_specs=[spec, spec],
        out_shape=[jax.ShapeDtypeStruct(aff4.shape, I32), jax.ShapeDtypeStruct(aff4.shape, I32)],
        compiler_params=_cparams(("arbitrary",)),
        name="select",
    )(aff4)


def _window(cap, tile):
    return min(tile + BF16_ROWS, cap)


def _window_start(off, cap, win):
    start = lax.shift_left(lax.shift_right_logical(off, 4), 4)
    return pl.multiple_of(jnp.minimum(start, cap - win), BF16_ROWS)


def _gather_kernel(offs_ref, small_ref, h2_ref, sl_ref, af_ref, xs_ref, val_ref, *, cap, tile, n_tiles):
    b, eg, s = pl.program_id(0), pl.program_id(1), pl.program_id(2)
    ge = sl_ref.shape[1]
    sub = h2_ref.shape[1] // tile
    full = _window(cap, tile)
    small = min(GATHER_SMALL, full)

    @pl.when(s == 0)
    def _():
        xs_ref[...] = jnp.zeros_like(xs_ref)
        val_ref[...] = jnp.zeros_like(val_ref)

    def tile_rows(u, win):
        ts = slice(u * tile, (u + 1) * tile)
        starts, hits = [], []
        for j in range(ge):
            off = offs_ref[(b * N_EXPERTS + eg * ge + j) * n_tiles + s * sub + u]
            start = _window_start(off, cap, win)
            rows = start + lax.broadcasted_iota(I32, (win, tile), 0)
            starts.append(start)
            hits.append(rows == sl_ref[0, j, :, ts])
        stacked = jnp.concatenate([h.astype(BF16) for h in hits], axis=0)
        res = jnp.dot(stacked, h2_ref[0, ts, :], preferred_element_type=F32)
        for j in range(ge):
            dst = pl.ds(starts[j], win)
            xs_ref[0, j, dst, :] = (xs_ref[0, j, dst, :].astype(F32) + res[j * win:(j + 1) * win]).astype(BF16)
            val_ref[0, j, dst, :] += jnp.sum(jnp.where(hits[j], af_ref[0, j, :, ts], 0.0), axis=1, keepdims=True)

    for u in range(sub):
        if small == full:
            tile_rows(u, full)
        else:
            fits = small_ref[(b * pl.num_programs(1) + eg) * n_tiles + s * sub + u] > 0
            pl.when(fits)(functools.partial(tile_rows, u, small))
            pl.when(jnp.logical_not(fits))(functools.partial(tile_rows, u, full))


def _gather(offs, small, h2, slot, aff, cap, tile):
    bsz, n, d = h2.shape
    tg = min(n, 1024)
    n_tiles = n // tile
    ge = GATHER_EXPERTS
    kern = functools.partial(_gather_kernel, cap=cap, tile=tile, n_tiles=n_tiles)
    return pl.pallas_call(
        kern,
        grid_spec=pltpu.PrefetchScalarGridSpec(
            num_scalar_prefetch=2,
            grid=(bsz, N_EXPERTS // ge, n // tg),
            in_specs=[
                pl.BlockSpec((1, tg, d), lambda b, e, s, offs, small: (b, s, 0)),
                pl.BlockSpec((1, ge, 1, tg), lambda b, e, s, offs, small: (b, e, 0, s)),
                pl.BlockSpec((1, ge, 1, tg), lambda b, e, s, offs, small: (b, e, 0, s)),
            ],
            out_specs=[
                pl.BlockSpec((1, ge, cap, d), lambda b, e, s, offs, small: (b, e, 0, 0)),
                pl.BlockSpec((1, ge, cap, 1), lambda b, e, s, offs, small: (b, e, 0, 0)),
            ],
        ),
        out_shape=[
            jax.ShapeDtypeStruct((bsz, N_EXPERTS, cap, d), BF16),
            jax.ShapeDtypeStruct((bsz, N_EXPERTS, cap, 1), F32),
        ],
        compiler_params=_cparams(("arbitrary", "arbitrary", "arbitrary")),
        name="gather",
    )(offs, small, h2, slot.reshape(bsz, N_EXPERTS, 1, n), aff.reshape(bsz, N_EXPERTS, 1, n))


def _up_kernel(*refs, has_ctx):
    if has_ctx:
        xl_ref, xc_ref, wg_ref, wu_ref, h_ref, xs_ref = refs
        cl = xl_ref.shape[2]

        @pl.when(pl.program_id(2) == 0)
        def _():
            xs_ref[0:cl, :] = xl_ref[0, 0]
            xs_ref[cl:, :] = xc_ref[0, 0]

        xs_of = lambda rs: xs_ref[rs, :]
    else:
        xl_ref, wg_ref, wu_ref, h_ref = refs
        xs_of = lambda rs: xl_ref[0, 0, rs, :]
    wg = wg_ref[0, 0].astype(BF16)
    wu = wu_ref[0, 0].astype(BF16)
    rows = h_ref.shape[2]
    for r in range(UP_ROW_SPLIT):
        rs = slice(r * rows // UP_ROW_SPLIT, (r + 1) * rows // UP_ROW_SPLIT)
        xs = xs_of(rs)
        a = jnp.dot(xs, wg, preferred_element_type=F32)
        u = jnp.dot(xs, wu, preferred_element_type=F32)
        h_ref[0, 0, rs, :] = (a * jax.nn.sigmoid(a) * u).astype(BF16)


def _down_kernel(*refs, has_ctx):
    if has_ctx:
        h_ref, vl_ref, vc_ref, wd_ref, y_ref = refs
    else:
        h_ref, vl_ref, wd_ref, y_ref = refs
    cl = vl_ref.shape[2]
    wd = wd_ref[0, 0].astype(BF16)
    rows = h_ref.shape[2]
    half = rows // 2
    res = jnp.dot(h_ref[0, 0, 0:half, :], wd, preferred_element_type=F32)
    y_ref[0, 0, 0:half, :] = (res * vl_ref[0, 0, 0:half, :]).astype(BF16)
    val = vl_ref[0, 0, half:, :]
    if has_ctx:
        val = jnp.concatenate([val, vc_ref[0, 0]], axis=0)
    res = jnp.dot(h_ref[0, 0, half:, :], wd, preferred_element_type=F32)
    y_ref[0, 0, half:, :] = (res * val).astype(BF16)


def _mlp(layer, xl, vl, xc, vc, wg, wu, wd):
    bsz, ne, cl, d = xl.shape
    fdim = wg.shape[-1]
    has_ctx = xc is not None
    cc = xc.shape[2] if has_ctx else 0
    rows = cl + cc
    fc = 1024
    dn = 1024

    xspecs = [pl.BlockSpec((1, 1, cl, d), lambda e, b, f: (b, e, 0, 0))]
    if has_ctx:
        xspecs.append(pl.BlockSpec((1, 1, cc, d), lambda e, b, f: (b, e, 0, 0)))
    hid = pl.pallas_call(
        functools.partial(_up_kernel, has_ctx=has_ctx),
        grid=(ne, bsz, fdim // fc),
        in_specs=xspecs + [
            pl.BlockSpec((1, 1, d, fc), lambda e, b, f: (layer, e, 0, f)),
            pl.BlockSpec((1, 1, d, fc), lambda e, b, f: (layer, e, 0, f)),
        ],
        out_specs=pl.BlockSpec((1, 1, rows, fc), lambda e, b, f: (b, e, 0, f)),
        out_shape=jax.ShapeDtypeStruct((bsz, ne, rows, fdim), BF16),
        scratch_shapes=[pltpu.VMEM((rows, d), BF16)] if has_ctx else [],
        compiler_params=_cparams(("arbitrary", "arbitrary", "arbitrary")),
        name="experts_up",
    )(*((xl, xc) if has_ctx else (xl,)), wg, wu)

    vspecs = [pl.BlockSpec((1, 1, cl, 1), lambda e, b, j: (b, e, 0, 0))]
    if has_ctx:
        vspecs.append(pl.BlockSpec((1, 1, cc, 1), lambda e, b, j: (b, e, 0, 0)))
    return pl.pallas_call(
        functools.partial(_down_kernel, has_ctx=has_ctx),
        grid=(ne, bsz, d // dn),
        in_specs=[pl.BlockSpec((1, 1, rows, fdim), lambda e, b, j: (b, e, 0, 0))] + vspecs + [
            pl.BlockSpec((1, 1, fdim, dn), lambda e, b, j: (layer, e, 0, j)),
        ],
        out_specs=pl.BlockSpec((1, 1, rows, dn), lambda e, b, j: (b, e, 0, j)),
        out_shape=jax.ShapeDtypeStruct((bsz, ne, rows, d), BF16),
        compiler_params=_cparams(("arbitrary", "arbitrary", "arbitrary")),
        name="experts_down",
    )(hid, *((vl, vc) if has_ctx else (vl,)), wd)


def _combine_window(cap, tile):
    return min(tile, cap)


def _combine_kernel(offs_ref, ovf_ref, small_ref, x1_ref, g2_ref, sl_ref, y_hbm, o_ref, ywin_ref, yext_ref,
                    sems, semx, *, cap, tile, n_tiles, base):
    b, t = pl.program_id(0), pl.program_id(1)
    n_steps = pl.num_programs(0) * n_tiles
    win = _combine_window(cap, tile)
    small = min(COMBINE_SMALL, win)
    step = b * n_tiles + t
    buf = lax.rem(step, 2)

    def start_of(bb, tt, e, w):
        return _window_start(offs_ref[(bb * N_EXPERTS + e) * n_tiles + tt], cap, w)

    def window_copy(bb, tt, bf, e, w):
        src = pl.multiple_of(base + start_of(bb, tt, e, w), BF16_ROWS)
        return pltpu.make_async_copy(y_hbm.at[bb, e, pl.ds(src, w), :],
                                     ywin_ref.at[bf, pl.ds(e * w, w), :], sems.at[bf, e])

    def by_window(stp, fn):
        if small == win:
            fn(win)
        else:
            fits = small_ref[stp] > 0
            pl.when(fits)(functools.partial(fn, small))
            pl.when(jnp.logical_not(fits))(functools.partial(fn, win))

    def start_windows(bb, tt, bf, w):
        for e in range(N_EXPERTS):
            window_copy(bb, tt, bf, e, w).start()

    @pl.when(step == 0)
    def _():
        by_window(step, functools.partial(start_windows, b, t, buf))

    @pl.when(step < n_steps - 1)
    def _():
        wrap = t == n_tiles - 1
        b_next = jnp.where(wrap, b + 1, b)
        t_next = jnp.where(wrap, 0, t + 1)
        by_window(step + 1, functools.partial(start_windows, b_next, t_next, 1 - buf))

    sl = sl_ref[0] + 1
    eye = (lax.broadcasted_iota(I32, (tile, tile), 0)
           == lax.broadcasted_iota(I32, (tile, tile), 1)).astype(BF16)
    d_hi = lax.shift_right_logical(sl, 6).astype(F32).astype(BF16)
    d_lo = (sl & 63).astype(F32).astype(BF16)
    slot_t = (lax.dot_general(eye, d_hi, NT_DIMS, preferred_element_type=F32) * 64.0
              + lax.dot_general(eye, d_lo, NT_DIMS, preferred_element_type=F32)).astype(I32) - 1

    def accumulate(w):
        lane = lax.broadcasted_iota(I32, (tile, w), 1)
        hits = [(lane == slot_t[:, e:e + 1] - start_of(b, t, e, w)).astype(BF16) for e in range(N_EXPERTS)]
        for e in range(N_EXPERTS):
            window_copy(b, t, buf, e, w).wait()
        acc = None
        for g in range(0, N_EXPERTS, COMBINE_GROUP):
            part = jnp.dot(jnp.concatenate(hits[g:g + COMBINE_GROUP], axis=1),
                           ywin_ref[buf, g * w:(g + COMBINE_GROUP) * w, :], preferred_element_type=F32)
            acc = part if acc is None else acc + part
        o_ref[0] = x1_ref[0] + g2_ref[0] * acc

    by_window(step, accumulate)

    if win < cap:
        @pl.when(ovf_ref[step] > 0)
        def _():
            def ext_start(e):
                s0 = start_of(b, t, e, win)
                return s0, pl.multiple_of(jnp.minimum(s0 + win, cap - BF16_ROWS), BF16_ROWS)

            def ext_copy(e):
                src = pl.multiple_of(base + ext_start(e)[1], BF16_ROWS)
                return pltpu.make_async_copy(y_hbm.at[b, e, pl.ds(src, BF16_ROWS), :],
                                             yext_ref.at[pl.ds(e * BF16_ROWS, BF16_ROWS), :], semx.at[e])

            for e in range(N_EXPERTS):
                ext_copy(e).start()
            lane_x = lax.broadcasted_iota(I32, (tile, BF16_ROWS), 1)
            hits_x = []
            for e in range(N_EXPERTS):
                ext_copy(e).wait()
                s0, s1 = ext_start(e)
                row = s1 + lane_x
                hits_x.append(((row == slot_t[:, e:e + 1]) & (row >= s0 + win)).astype(BF16))
            extra = jnp.dot(jnp.concatenate(hits_x, axis=1), yext_ref[...], preferred_element_type=F32)
            o_ref[0] += g2_ref[0] * extra


def _combine(offs, ovf, small, x1, g2, slot, y, cap, tile, base):
    bsz, n, d = x1.shape
    n_tiles = n // tile
    win = _combine_window(cap, tile)
    kern = functools.partial(_combine_kernel, cap=cap, tile=tile, n_tiles=n_tiles, base=base)
    return pl.pallas_call(
        kern,
        grid_spec=pltpu.PrefetchScalarGridSpec(
            num_scalar_prefetch=3,
            grid=(bsz, n_tiles),
            in_specs=[
                pl.BlockSpec((1, tile, d), lambda b, t, *_: (b, t, 0)),
                pl.BlockSpec((1, 1, d), lambda b, t, *_: (b, 0, 0)),
                pl.BlockSpec((1, N_EXPERTS, tile), lambda b, t, *_: (b, 0, t)),
                pl.BlockSpec(memory_space=pl.ANY),
            ],
            out_specs=pl.BlockSpec((1, tile, d), lambda b, t, *_: (b, t, 0)),
            scratch_shapes=[pltpu.VMEM((2, N_EXPERTS * win, d), BF16),
                            pltpu.VMEM((N_EXPERTS * BF16_ROWS, d), BF16),
                            pltpu.SemaphoreType.DMA((2, N_EXPERTS)),
                            pltpu.SemaphoreType.DMA((N_EXPERTS,))],
        ),
        out_shape=jax.ShapeDtypeStruct((bsz, n, d), F32),
        compiler_params=_cparams(("arbitrary", "arbitrary")),
        name="combine",
    )(offs, ovf, small, x1, g2, slot, y)


def _deinterleave(w, n_heads):
    lead = w.shape[:-1]
    w = w.reshape(lead + (n_heads, HEAD_DIM // 2, 2))
    return jnp.swapaxes(w, -1, -2).reshape(lead + (n_heads * HEAD_DIM,))


def _rope_tables(n):
    rows = n // GRID_W
    row = jnp.repeat(jnp.arange(rows), GRID_W).astype(F32)
    col = jnp.tile(jnp.arange(GRID_W), rows).astype(F32)
    n_freq = HEAD_DIM // 4
    inv = ROPE_THETA ** (-jnp.arange(n_freq, dtype=F32) / n_freq)
    ang = jnp.concatenate([row[:, None] * inv, col[:, None] * inv], axis=-1)
    cos = jnp.tile(jnp.cos(ang), (1, 4))
    sin = jnp.sin(ang)
    sin_signed = jnp.tile(jnp.concatenate([-sin, sin], axis=-1), (1, 2))
    return cos, sin_signed


def _split_bf16(w):
    hi = w.astype(BF16)
    return hi, (w - hi.astype(F32)).astype(BF16)


def _route_select(aff, cap, gather_tile, combine_tile):
    bsz, ne, n = aff.shape
    n_pad = max(n, HALO * LANES)
    a = aff if n_pad == n else jnp.pad(aff, ((0, 0), (0, 0), (0, n_pad - n)))
    slot4, offc4 = _select(a.reshape(bsz, ne, n_pad // LANES, LANES), cap)

    def tile_tables(tile):
        n_tiles = n // tile
        offs = offc4[:, :, ::tile // LANES, 0][:, :, :n_tiles]
        ends = jnp.concatenate([offs[:, :, 1:], jnp.full((bsz, ne, 1), cap, I32)], axis=2)

        def fits(win):
            return ends <= jnp.minimum((offs >> 4) << 4, cap - win) + win

        return offs.reshape(-1), fits

    g_offs, g_fits = tile_tables(gather_tile)
    gather = {
        "offs": g_offs,
        "small": jnp.all(g_fits(min(GATHER_SMALL, _window(cap, gather_tile)))
                         .reshape(bsz, ne // GATHER_EXPERTS, GATHER_EXPERTS, -1), axis=2).astype(I32).reshape(-1),
    }
    c_offs, c_fits = tile_tables(combine_tile)
    win_c = _combine_window(cap, combine_tile)
    combine = {
        "offs": c_offs,
        "overflow": jnp.any(~c_fits(win_c), axis=1).astype(I32).reshape(-1),
        "small": jnp.all(c_fits(min(COMBINE_SMALL, win_c)), axis=1).astype(I32).reshape(-1),
    }
    return slot4.reshape(bsz, ne, n_pad)[:, :, :n], gather, combine


def kernel(x, c, ctx, c_ctx, w_ada, b_ada, norm1_g, norm2_g, pool_w, pool_scale, attn_w_qkv, attn_w_o,
           attn_q_norm, attn_k_norm, attn_sink, router_w, exp_w_gate, exp_w_up, exp_w_down):
    bsz, n, d = x.shape
    lc = ctx.shape[1]
    cap = 2 * n // N_EXPERTS
    cap_c = 2 * lc // N_EXPERTS
    tile_c = min(COMBINE_TILE, lc)

    cc = jnp.zeros((8, d), F32).at[:bsz].set(c).at[bsz].set(c_ctx)
    mods = _ada(cc, w_ada, b_ada)

    cos, sin = _rope_tables(n)
    cos_c = jnp.ones((lc, LANES), F32)
    sin_c = jnp.zeros((lc, LANES), F32)
    blk = jnp.arange(256) // HEAD_DIM
    bavg = ((blk[:, None] == blk[None, :]).astype(F32) / HEAD_DIM).astype(BF16)

    def layer_mods(i):
        m = [mods[i, :, k * d:(k + 1) * d] for k in range(6)]
        lat = [v[:bsz, None, :] for v in m]
        con = [jnp.broadcast_to(v[bsz][None, None, :], (bsz, 1, d)) for v in m]
        return lat, con

    def qkv_params(i):
        j = i // 2
        nq = N_Q_HEADS * HEAD_DIM
        nk = N_KV_HEADS * HEAD_DIM
        wqkv = attn_w_qkv[j]
        wperm = jnp.concatenate([_deinterleave(wqkv[:, :nq], N_Q_HEADS),
                                 _deinterleave(wqkv[:, nq:nq + nk], N_KV_HEADS),
                                 wqkv[:, nq + nk:]], axis=1).astype(BF16)
        qg = jnp.tile(_deinterleave(attn_q_norm[j], 1), 4)[None, :]
        kg = jnp.tile(_deinterleave(attn_k_norm[j], 1), 4)[None, :]
        return wperm, bavg, qg, kg

    for i in range(DEPTH):
        update_ctx = i < DEPTH - 1
        j = i // 2
        is_pool = (i % 2) == 0
        (sh1, sc1, g1, sh2, sc2, g2), (csh1, csc1, cg1, csh2, csc2, cg2) = layer_mods(i)
        n1g = norm1_g[i][None, :]
        n2g = norm2_g[i][None, :]
        wrh, wrl = _split_bf16(router_w[i].T)

        if is_pool:
            pw = pool_w[j].astype(BF16)
            ps = pool_scale[j][None, :]
            x1, h2, aff = _pool_layer(x, n1g, sh1, sc1, g1, pw, ps, n2g, sh2, sc2, wrh, wrl, 512)
            if update_ctx:
                c1, hc2, affc = _pool_layer(ctx, n1g, csh1, csc1, cg1, pw, ps, n2g, csh2, csc2, wrh, wrl, lc)
        else:
            wperm, _, qg, kg = qkv_params(i)
            wo = attn_w_o[j].astype(BF16)
            sink = attn_sink[j]
            q, k, v = _qkv_layer(x, n1g, sh1, sc1, wperm, bavg, qg, kg, cos, sin, 512)
            qc, kc, vc = _qkv_layer(ctx, n1g, csh1, csc1, wperm, bavg, qg, kg, cos_c, sin_c, lc)
            o = _attn_layer(sink, q, k, v, kc, vc, True)
            x1, h2, aff = _oproj_layer(o, x, wo, g1, n2g, sh2, sc2, wrh, wrl, 512)
            if update_ctx:
                oc = _attn_layer(sink, qc, None, None, kc, vc, False)
                c1, hc2, affc = _oproj_layer(oc, ctx, wo, cg1, n2g, csh2, csc2, wrh, wrl, lc)

        slot, tg, tc = _route_select(aff, cap, GATHER_TILE, COMBINE_TILE)
        xs, vals = _gather(tg["offs"], tg["small"], h2, slot, aff, cap, GATHER_TILE)
        if update_ctx:
            slot_c, cg, cc_ = _route_select(affc, cap_c, tile_c, tile_c)
            xs_c, vals_c = _gather(cg["offs"], cg["small"], hc2, slot_c, affc, cap_c, tile_c)
            y = _mlp(i, xs, vals, xs_c, vals_c, exp_w_gate, exp_w_up, exp_w_down)
            ctx = _combine(cc_["offs"], cc_["overflow"], cc_["small"], c1, cg2, slot_c, y, cap_c, tile_c, cap)
        else:
            y = _mlp(i, xs, vals, None, None, exp_w_gate, exp_w_up, exp_w_down)
        x = _combine(tc["offs"], tc["overflow"], tc["small"], x1, g2, slot, y, cap, COMBINE_TILE, 0)
    return x
```

```python
import functools

import jax
import jax.numpy as jnp
from jax import lax
from jax.experimental import pallas as pl
from jax.experimental.pallas import tpu as pltpu

F32 = jnp.float32
BF16 = jnp.bfloat16
I32 = jnp.int32

D_MODEL = 1024
DEPTH = 4
GRID_W = 64
POOL_WINDOWS = (2, 4, 8, 16)
POOL_GROUP_DIM = 256
HEAD_DIM = 64
N_Q_HEADS = 16
N_KV_HEADS = 4
Q_PER_KV = 4
WINDOW = 128
BLOCK_Q = 128
ROPE_THETA = 10000.0
N_EXPERTS = 16
D_EXPERT = 2048
NORM_EPS = 1e-6
NEG_INF = -1e30

LANES = 128
BF16_ROWS = 16
HALO = 8
ROW_TILE = 1024
MAX_SHIFT = 40.0
SOFTMAX_ROWS = 32
GATHER_EXPERTS = 4
COMBINE_GROUP = 4
UP_ROW_SPLIT = 2
GATHER_TILE, GATHER_SMALL = 256, 64
COMBINE_TILE, COMBINE_SMALL = 256, 64
VMEM_LIMIT = 56 * 1024 * 1024

NT_DIMS = (((1,), (1,)), ((), ()))


def _cparams(sem, vmem=None):
    return pltpu.CompilerParams(dimension_semantics=sem, vmem_limit_bytes=vmem or VMEM_LIMIT)


def _norm_mod(x, g, shift, scale):
    ms = jnp.mean(x * x, axis=-1, keepdims=True)
    return (x * lax.rsqrt(ms + NORM_EPS)) * g * (1.0 + scale) + shift


def _ada_kernel(cc_ref, w_ref, b_ref, o_ref):
    cc = cc_ref[...]
    s = cc * jax.nn.sigmoid(cc)
    o_ref[0] = jnp.dot(s, w_ref[0], preferred_element_type=F32,
                       precision=lax.Precision.HIGHEST) + b_ref[0]


def _ada(cc, w_ada, b_ada):
    depth, d, six_d = w_ada.shape
    tn = 1536
    return pl.pallas_call(
        _ada_kernel,
        grid=(depth, six_d // tn),
        in_specs=[
            pl.BlockSpec((8, d), lambda i, j: (0, 0)),
            pl.BlockSpec((1, d, tn), lambda i, j: (i, 0, j)),
            pl.BlockSpec((1, 1, tn), lambda i, j: (i, 0, j)),
        ],
        out_specs=pl.BlockSpec((1, 8, tn), lambda i, j: (i, 0, j)),
        out_shape=jax.ShapeDtypeStruct((depth, 8, six_d), F32),
        compiler_params=_cparams(("arbitrary", "arbitrary")),
        name="ada",
    )(cc, w_ada, b_ada.reshape(depth, 1, six_d))


def _route(x1, n2g, sh2, sc2, wr_hi, wr_lo):
    h2 = _norm_mod(x1, n2g, sh2, sc2)
    h_hi = h2.astype(BF16)
    h_lo = (h2 - h_hi.astype(F32)).astype(BF16)
    lg = (lax.dot_general(wr_hi, h_hi, NT_DIMS, preferred_element_type=F32)
          + lax.dot_general(wr_hi, h_lo, NT_DIMS, preferred_element_type=F32)
          + lax.dot_general(wr_lo, h_hi, NT_DIMS, preferred_element_type=F32))
    m = jnp.max(lg, axis=0, keepdims=True)
    ex = jnp.exp(lg - m)
    aff = ex / jnp.sum(ex, axis=0, keepdims=True)
    return h_hi, aff


def _pool_kernel(x_ref, xp_ref, xn_ref, n1g_ref, sh1_ref, sc1_ref, g1_ref, pw_ref, ps_ref,
                 n2g_ref, sh2_ref, sc2_ref, wrh_ref, wrl_ref,
                 x1_ref, h2_ref, aff_ref, hs_ref, *, n_tokens):
    t = pl.program_id(1)
    nt = pl.num_programs(1)
    tt = x_ref.shape[1]
    n1g, sh1, sc1 = n1g_ref[...], sh1_ref[0], sc1_ref[0]
    x = x_ref[0]
    h = _norm_mod(x, n1g, sh1, sc1)
    hp = _norm_mod(xp_ref[0], n1g, sh1, sc1) * (t > 0).astype(F32)
    hn = _norm_mod(xn_ref[0], n1g, sh1, sc1) * (t < nt - 1).astype(F32)
    hs_ref[0:HALO, :] = hp
    hs_ref[HALO:HALO + tt, :] = h
    hs_ref[HALO + tt:HALO + tt + HALO, :] = hn
    pos = t * tt + lax.broadcasted_iota(I32, (tt, 1), 0)
    ys = []
    for g, w in enumerate(POOL_WINDOWS):
        half = w // 2
        c0 = g * POOL_GROUP_DIM
        s = hs_ref[HALO - half:HALO - half + tt, c0:c0 + POOL_GROUP_DIM]
        for k in range(1, w):
            s = s + hs_ref[HALO - half + k:HALO - half + k + tt, c0:c0 + POOL_GROUP_DIM]
        lo = jnp.clip(pos - half, 0, n_tokens)
        hi = jnp.clip(pos + half, 0, n_tokens)
        p = s / (hi - lo).astype(F32) - h[:, c0:c0 + POOL_GROUP_DIM]
        ys.append(jnp.dot(p.astype(BF16), pw_ref[g], preferred_element_type=F32))
    y = jnp.concatenate(ys, axis=1) * ps_ref[...]
    x1 = x + g1_ref[0] * y
    x1_ref[0] = x1
    h2, aff = _route(x1, n2g_ref[...], sh2_ref[0], sc2_ref[0], wrh_ref[...], wrl_ref[...])
    h2_ref[0] = h2
    aff_ref[0] = aff


def _row_spec(d):
    return pl.BlockSpec((1, d), lambda b, t: (0, 0))


def _mod_spec(d):
    return pl.BlockSpec((1, 1, d), lambda b, t: (b, 0, 0))


def _pool_layer(x, n1g, sh1, sc1, g1, pw, ps, n2g, sh2, sc2, wrh, wrl, tt):
    bsz, n, d = x.shape
    nb8 = n // HALO
    r = tt // HALO
    kern = functools.partial(_pool_kernel, n_tokens=n)
    return pl.pallas_call(
        kern,
        grid=(bsz, n // tt),
        in_specs=[
            pl.BlockSpec((1, tt, d), lambda b, t: (b, t, 0)),
            pl.BlockSpec((1, HALO, d), lambda b, t: (b, jnp.maximum(t * r - 1, 0), 0)),
            pl.BlockSpec((1, HALO, d), lambda b, t: (b, jnp.minimum((t + 1) * r, nb8 - 1), 0)),
            _row_spec(d), _mod_spec(d), _mod_spec(d), _mod_spec(d),
            pl.BlockSpec((4, POOL_GROUP_DIM, POOL_GROUP_DIM), lambda b, t: (0, 0, 0)),
            _row_spec(d),
            _row_spec(d), _mod_spec(d), _mod_spec(d),
            pl.BlockSpec((N_EXPERTS, d), lambda b, t: (0, 0)),
            pl.BlockSpec((N_EXPERTS, d), lambda b, t: (0, 0)),
        ],
        out_specs=[
            pl.BlockSpec((1, tt, d), lambda b, t: (b, t, 0)),
            pl.BlockSpec((1, tt, d), lambda b, t: (b, t, 0)),
            pl.BlockSpec((1, N_EXPERTS, tt), lambda b, t: (b, 0, t)),
        ],
        out_shape=[
            jax.ShapeDtypeStruct((bsz, n, d), F32),
            jax.ShapeDtypeStruct((bsz, n, d), BF16),
            jax.ShapeDtypeStruct((bsz, N_EXPERTS, n), F32),
        ],
        scratch_shapes=[pltpu.VMEM((tt + 2 * HALO, d), F32)],
        compiler_params=_cparams(("arbitrary", "arbitrary")),
        name="pool_route",
    )(x, x, x, n1g, sh1, sc1, g1, pw, ps, n2g, sh2, sc2, wrh, wrl)


def _head_rms(v, bavg, gain):
    sq = v * v
    sq_hi = sq.astype(BF16)
    sq_lo = (sq - sq_hi.astype(F32)).astype(BF16)
    ms = (jnp.dot(sq_hi, bavg, preferred_element_type=F32)
          + jnp.dot(sq_lo, bavg, preferred_element_type=F32))
    return v * lax.rsqrt(ms + NORM_EPS) * gain


def _rope(v, cos, sin_signed):
    lane = lax.broadcasted_iota(I32, v.shape, 1)
    fwd = pltpu.roll(v, 32, 1)
    bwd = pltpu.roll(v, 96, 1)
    partner = jnp.where((lane % HEAD_DIM) < HEAD_DIM // 2, bwd, fwd)
    return v * cos + partner * sin_signed


def _qkv_kernel(x_ref, n1g_ref, sh1_ref, sc1_ref, w_ref, bavg_ref, qg_ref, kg_ref,
                cos_ref, sin_ref, q_ref, k_ref, v_ref):
    _qkv_body(x_ref[0], n1g_ref, sh1_ref, sc1_ref, w_ref, bavg_ref, qg_ref, kg_ref,
              cos_ref, sin_ref, q_ref, k_ref, v_ref)


def _qkv_body(x, n1g_ref, sh1_ref, sc1_ref, w_ref, bavg_ref, qg_ref, kg_ref,
              cos_ref, sin_ref, q_ref, k_ref, v_ref):
    h = _norm_mod(x, n1g_ref[...], sh1_ref[0], sc1_ref[0]).astype(BF16)
    qkv = jnp.dot(h, w_ref[...], preferred_element_type=F32)
    nq = N_Q_HEADS * HEAD_DIM
    nk = N_KV_HEADS * HEAD_DIM
    bavg = bavg_ref[...]
    cos, sin = cos_ref[...], sin_ref[...]
    scale = HEAD_DIM ** -0.5
    for c in range(nq // 256):
        qn = _head_rms(qkv[:, c * 256:(c + 1) * 256], bavg, qg_ref[...])
        for s in range(2):
            qr = _rope(qn[:, s * LANES:(s + 1) * LANES], cos, sin) * scale
            for hh in range(2):
                head = c * 4 + s * 2 + hh
                q_ref[0, head] = qr[:, hh * HEAD_DIM:(hh + 1) * HEAD_DIM].astype(BF16)
    kn = _head_rms(qkv[:, nq:nq + nk], bavg, kg_ref[...])
    v = qkv[:, nq + nk:]
    for s in range(2):
        kr = _rope(kn[:, s * LANES:(s + 1) * LANES], cos, sin)
        for hh in range(2):
            head = s * 2 + hh
            k_ref[0, head] = kr[:, hh * HEAD_DIM:(hh + 1) * HEAD_DIM].astype(BF16)
            v_ref[0, head] = v[:, head * HEAD_DIM:(head + 1) * HEAD_DIM].astype(BF16)


def _qkv_layer(x, n1g, sh1, sc1, w, bavg, qg, kg, cos, sin, tt):
    bsz, n, d = x.shape
    return pl.pallas_call(
        _qkv_kernel,
        grid=(bsz, n // tt),
        in_specs=[
            pl.BlockSpec((1, tt, d), lambda b, t: (b, t, 0)),
            _row_spec(d), _mod_spec(d), _mod_spec(d),
            pl.BlockSpec(w.shape, lambda b, t: (0, 0)),
            pl.BlockSpec((256, 256), lambda b, t: (0, 0)),
            _row_spec(256), _row_spec(256),
            pl.BlockSpec((tt, LANES), lambda b, t: (t, 0)),
            pl.BlockSpec((tt, LANES), lambda b, t: (t, 0)),
        ],
        out_specs=[
            pl.BlockSpec((1, N_Q_HEADS, tt, HEAD_DIM), lambda b, t: (b, 0, t, 0)),
            pl.BlockSpec((1, N_KV_HEADS, tt, HEAD_DIM), lambda b, t: (b, 0, t, 0)),
            pl.BlockSpec((1, N_KV_HEADS, tt, HEAD_DIM), lambda b, t: (b, 0, t, 0)),
        ],
        out_shape=[
            jax.ShapeDtypeStruct((bsz, N_Q_HEADS, n, HEAD_DIM), BF16),
            jax.ShapeDtypeStruct((bsz, N_KV_HEADS, n, HEAD_DIM), BF16),
            jax.ShapeDtypeStruct((bsz, N_KV_HEADS, n, HEAD_DIM), BF16),
        ],
        compiler_params=_cparams(("arbitrary", "arbitrary")),
        name="qkv",
    )(x, n1g, sh1, sc1, w, bavg, qg, kg, cos, sin)


def _attn_kernel(sink_ref, q_ref, *refs, band, bounded):
    if band:
        kp_ref, kc_ref, kn_ref, vp_ref, vc_ref, vn_ref, kx_ref, vx_ref, bias_ref, o_ref = refs[:10]
    else:
        kx_ref, vx_ref, o_ref = refs[:3]
    if not bounded:
        s_all, p_all = refs[-2:]
    tq = q_ref.shape[2]
    n_chunks = Q_PER_KV * tq // SOFTMAX_ROWS
    for g in range(N_KV_HEADS):
        if band:
            kg = jnp.concatenate([kp_ref[0, g], kc_ref[0, g], kn_ref[0, g], kx_ref[0, g]], axis=0)
            vg = jnp.concatenate([vp_ref[0, g], vc_ref[0, g], vn_ref[0, g], vx_ref[0, g]], axis=0)
        else:
            kg = kx_ref[0, g]
            vg = vx_ref[0, g]
        vext = jnp.concatenate([vg, jnp.ones_like(vg)], axis=1)
        qg = q_ref[0, g * Q_PER_KV:(g + 1) * Q_PER_KV].reshape(Q_PER_KV * tq, HEAD_DIM)
        if bounded:
            shift = sink_ref[N_Q_HEADS]
            s = lax.dot_general(qg, kg, NT_DIMS, preferred_element_type=F32)
            if band:
                s = s + bias_ref[0]
            p = jnp.exp(s - shift).astype(BF16)
            sink_term = jnp.exp(jnp.concatenate(
                [jnp.full((tq, 1), sink_ref[g * Q_PER_KV + hh] - shift, F32) for hh in range(Q_PER_KV)], axis=0))
        else:
            s_ref, p_ref = s_all.at[g], p_all.at[g]
            s_ref[...] = lax.dot_general(qg, kg, NT_DIMS, preferred_element_type=F32)
            sink_terms = []
            for r in range(n_chunks):
                rs = slice(r * SOFTMAX_ROWS, (r + 1) * SOFTMAX_ROWS)
                sc = s_ref[rs, :]
                if band:
                    sc = sc + bias_ref[0, rs, :]
                sk = sink_ref[g * Q_PER_KV + (r * SOFTMAX_ROWS) // tq]
                m = jnp.maximum(jnp.max(sc, axis=-1, keepdims=True), sk)
                p_ref[rs, :] = jnp.exp(sc - m).astype(BF16)
                sink_terms.append(jnp.exp(sk - m))
            p = p_ref[...]
            sink_term = jnp.concatenate(sink_terms, axis=0)
        oe = jnp.dot(p, vext, preferred_element_type=F32)
        den = oe[:, HEAD_DIM:HEAD_DIM + 1] + sink_term
        o = (oe[:, :HEAD_DIM] / den).astype(o_ref.dtype)
        for hh in range(Q_PER_KV):
            head = g * Q_PER_KV + hh
            o_ref[0, :, head * HEAD_DIM:(head + 1) * HEAD_DIM] = o[hh * tq:(hh + 1) * tq]


def _band_bias(lc):
    nband = BLOCK_Q + 2 * WINDOW
    row = jnp.arange(Q_PER_KV * BLOCK_Q)[:, None] % BLOCK_Q
    col = jnp.arange(nband + lc)[None, :]
    dlt = col - row
    inband = (dlt >= 0) & (dlt <= 2 * WINDOW)
    variants = [inband & (col >= WINDOW), inband, inband & (col < WINDOW + BLOCK_Q)]
    return jnp.stack([jnp.where(v | (col >= nband), 0.0, NEG_INF) for v in variants]).astype(F32)


def _attn_layer(sink, score_bound, q, k, v, kx, vx, band):
    sink_ext = jnp.concatenate([sink.astype(F32), score_bound.reshape(1).astype(F32)])
    if not band:
        k, v = kx, vx
    return lax.cond(score_bound <= MAX_SHIFT,
                    functools.partial(_attn_call, band=band, bounded=True),
                    functools.partial(_attn_call, band=band, bounded=False),
                    sink_ext, q, k, v, kx, vx)


def _attn_call(sink, q, k, v, kx, vx, *, band, bounded):
    bsz, _, n, _ = q.shape
    lc = kx.shape[2]
    tq = BLOCK_Q if band else n
    nb = n // tq
    n_keys = (BLOCK_Q + 2 * WINDOW + lc) if band else lc
    qspec = pl.BlockSpec((1, N_Q_HEADS, tq, HEAD_DIM), lambda b, t: (b, 0, t, 0))
    xspec = pl.BlockSpec((1, N_KV_HEADS, lc, HEAD_DIM), lambda b, t: (b, 0, 0, 0))
    if band:
        prev = pl.BlockSpec((1, N_KV_HEADS, tq, HEAD_DIM), lambda b, t: (b, 0, jnp.maximum(t - 1, 0), 0))
        cur = pl.BlockSpec((1, N_KV_HEADS, tq, HEAD_DIM), lambda b, t: (b, 0, t, 0))
        nxt = pl.BlockSpec((1, N_KV_HEADS, tq, HEAD_DIM), lambda b, t: (b, 0, jnp.minimum(t + 1, nb - 1), 0))
        bias = _band_bias(lc)
        bspec = pl.BlockSpec((1,) + bias.shape[1:],
                             lambda b, t: (jnp.where(t == 0, 0, jnp.where(t == nb - 1, 2, 1)), 0, 0))
        in_specs = [qspec, prev, cur, nxt, prev, cur, nxt, xspec, xspec, bspec]
        args = (q, k, k, k, v, v, v, kx, vx, bias)
    else:
        in_specs = [qspec, xspec, xspec]
        args = (q, kx, vx)
    scratch = [] if bounded else [pltpu.VMEM((N_KV_HEADS, Q_PER_KV * tq, n_keys), F32),
                                  pltpu.VMEM((N_KV_HEADS, Q_PER_KV * tq, n_keys), BF16)]
    return pl.pallas_call(
        functools.partial(_attn_kernel, band=band, bounded=bounded),
        grid=(bsz, nb),
        in_specs=[pl.BlockSpec(memory_space=pltpu.SMEM)] + in_specs,
        out_specs=pl.BlockSpec((1, tq, N_Q_HEADS * HEAD_DIM), lambda b, t: (b, t, 0)),
        out_shape=jax.ShapeDtypeStruct((bsz, n, N_Q_HEADS * HEAD_DIM), BF16),
        scratch_shapes=scratch,
        compiler_params=_cparams(("arbitrary", "arbitrary")),
        name=("attn_band" if band else "attn_ctx") + ("_bounded" if bounded else ""),
    )(sink, *args)


def _oproj_kernel(o_ref, x_ref, wo_ref, g1_ref, n2g_ref, sh2_ref, sc2_ref, wrh_ref, wrl_ref,
                  x1_ref, h2_ref, aff_ref):
    y = jnp.dot(o_ref[0], wo_ref[...], preferred_element_type=F32)
    x1 = x_ref[0] + g1_ref[0] * y
    x1_ref[0] = x1
    h2, aff = _route(x1, n2g_ref[...], sh2_ref[0], sc2_ref[0], wrh_ref[...], wrl_ref[...])
    h2_ref[0] = h2
    aff_ref[0] = aff


def _oproj_layer(o, x, wo, g1, n2g, sh2, sc2, wrh, wrl, tt):
    bsz, n, d = x.shape
    return pl.pallas_call(
        _oproj_kernel,
        grid=(bsz, n // tt),
        in_specs=[
            pl.BlockSpec((1, tt, d), lambda b, t: (b, t, 0)),
            pl.BlockSpec((1, tt, d), lambda b, t: (b, t, 0)),
            pl.BlockSpec((d, d), lambda b, t: (0, 0)),
            _mod_spec(d), _row_spec(d), _mod_spec(d), _mod_spec(d),
            pl.BlockSpec((N_EXPERTS, d), lambda b, t: (0, 0)),
            pl.BlockSpec((N_EXPERTS, d), lambda b, t: (0, 0)),
        ],
        out_specs=[
            pl.BlockSpec((1, tt, d), lambda b, t: (b, t, 0)),
            pl.BlockSpec((1, tt, d), lambda b, t: (b, t, 0)),
            pl.BlockSpec((1, N_EXPERTS, tt), lambda b, t: (b, 0, t)),
        ],
        out_shape=[
            jax.ShapeDtypeStruct((bsz, n, d), F32),
            jax.ShapeDtypeStruct((bsz, n, d), BF16),
            jax.ShapeDtypeStruct((bsz, N_EXPERTS, n), F32),
        ],
        compiler_params=_cparams(("arbitrary", "arbitrary")),
        name="oproj_route",
    )(o, x, wo, g1, n2g, sh2, sc2, wrh, wrl)


def _prefix_counts(mask, n_chunks):
    ne = mask.shape[0]
    m2 = mask.reshape(ne * n_chunks, LANES).astype(BF16)
    ri = lax.broadcasted_iota(I32, (LANES, LANES), 0)
    ci = lax.broadcasted_iota(I32, (LANES, LANES), 1)
    upper = (ri <= ci).astype(BF16)
    within = jnp.dot(m2, upper, preferred_element_type=F32).reshape(ne, n_chunks, LANES)
    tot = jnp.dot(m2, jnp.ones((LANES, LANES), BF16), preferred_element_type=F32).reshape(ne, n_chunks, LANES)
    rc = lax.broadcasted_iota(I32, (n_chunks, n_chunks), 0)
    cc = lax.broadcasted_iota(I32, (n_chunks, n_chunks), 1)
    strict_lower = (cc < rc).astype(F32)
    off = jnp.stack([jnp.dot(strict_lower, tot[e], preferred_element_type=F32) for e in range(ne)])
    return within + off, off


def _select_kernel(a_ref, slot_ref, offc_ref, *, cap):
    a = a_ref[0]
    ne, n_chunks, _ = a.shape
    bits = lax.bitcast_convert_type(a, I32)

    def count(pred):
        c = jnp.sum(pred.astype(F32), axis=1, keepdims=True)
        return jnp.sum(c, axis=2, keepdims=True)

    def body(i, thr):
        cand = thr | lax.shift_left(jnp.int32(1), 30 - i)
        return jnp.where(count(bits >= cand) >= cap, cand, thr)

    thr = lax.fori_loop(0, 31, body, jnp.zeros((ne, 1, 1), I32))
    gt = bits > thr
    eq = bits == thr
    need = cap - count(gt)
    eq_rank, _ = _prefix_counts(eq.astype(F32), n_chunks)
    sel = gt | (eq & (eq_rank <= need))
    sel_f = sel.astype(F32)
    incl, off = _prefix_counts(sel_f, n_chunks)
    slot_ref[0] = jnp.where(sel, (incl - 1.0).astype(I32), -1)
    offc_ref[0] = off.astype(I32)


def _select(aff4, cap):
    bsz, ne, n_chunks, cw = aff4.shape
    spec = pl.BlockSpec((1, ne, n_chunks, cw), lambda b: (b, 0, 0, 0))
    return pl.pallas_call(
        functools.partial(_select_kernel, cap=cap),
        grid=(bsz,),
        in_specs=[spec],
        out_specs=[spec, spec],
        out_shape=[jax.ShapeDtypeStruct(aff4.shape, I32), jax.ShapeDtypeStruct(aff4.shape, I32)],
        compiler_params=_cparams(("arbitrary",)),
        name="select",
    )(aff4)


def _window(cap, tile):
    return min(tile + BF16_ROWS, cap)


def _window_start(off, cap, win):
    start = lax.shift_left(lax.shift_right_logical(off, 4), 4)
    return pl.multiple_of(jnp.minimum(start, cap - win), BF16_ROWS)


def _gather_kernel(offs_ref, small_ref, h2_ref, sl_ref, af_ref, xs_ref, val_ref, *, cap, tile, n_tiles):
    b, eg, s = pl.program_id(0), pl.program_id(1), pl.program_id(2)
    ge = sl_ref.shape[1]
    sub = h2_ref.shape[1] // tile
    full = _window(cap, tile)
    small = min(GATHER_SMALL, full)

    @pl.when(s == 0)
    def _():
        xs_ref[...] = jnp.zeros_like(xs_ref)
        val_ref[...] = jnp.zeros_like(val_ref)

    def tile_rows(u, win):
        ts = slice(u * tile, (u + 1) * tile)
        starts, hits = [], []
        for j in range(ge):
            off = offs_ref[(b * N_EXPERTS + eg * ge + j) * n_tiles + s * sub + u]
            start = _window_start(off, cap, win)
            rows = start + lax.broadcasted_iota(I32, (win, tile), 0)
            starts.append(start)
            hits.append(rows == sl_ref[0, j, :, ts])
        stacked = jnp.concatenate([h.astype(BF16) for h in hits], axis=0)
        res = jnp.dot(stacked, h2_ref[0, ts, :], preferred_element_type=F32)
        for j in range(ge):
            dst = pl.ds(starts[j], win)
            xs_ref[0, j, dst, :] = (xs_ref[0, j, dst, :].astype(F32) + res[j * win:(j + 1) * win]).astype(BF16)
            val_ref[0, j, dst, :] += jnp.sum(jnp.where(hits[j], af_ref[0, j, :, ts], 0.0), axis=1, keepdims=True)

    for u in range(sub):
        if small == full:
            tile_rows(u, full)
        else:
            fits = small_ref[(b * pl.num_programs(1) + eg) * n_tiles + s * sub + u] > 0
            pl.when(fits)(functools.partial(tile_rows, u, small))
            pl.when(jnp.logical_not(fits))(functools.partial(tile_rows, u, full))


def _gather(offs, small, h2, slot, aff, cap, tile):
    bsz, n, d = h2.shape
    tg = min(n, 1024)
    n_tiles = n // tile
    ge = GATHER_EXPERTS
    kern = functools.partial(_gather_kernel, cap=cap, tile=tile, n_tiles=n_tiles)
    return pl.pallas_call(
        kern,
        grid_spec=pltpu.PrefetchScalarGridSpec(
            num_scalar_prefetch=2,
            grid=(bsz, N_EXPERTS // ge, n // tg),
            in_specs=[
                pl.BlockSpec((1, tg, d), lambda b, e, s, offs, small: (b, s, 0)),
                pl.BlockSpec((1, ge, 1, tg), lambda b, e, s, offs, small: (b, e, 0, s)),
                pl.BlockSpec((1, ge, 1, tg), lambda b, e, s, offs, small: (b, e, 0, s)),
            ],
            out_specs=[
                pl.BlockSpec((1, ge, cap, d), lambda b, e, s, offs, small: (b, e, 0, 0)),
                pl.BlockSpec((1, ge, cap, 1), lambda b, e, s, offs, small: (b, e, 0, 0)),
            ],
        ),
        out_shape=[
            jax.ShapeDtypeStruct((bsz, N_EXPERTS, cap, d), BF16),
            jax.ShapeDtypeStruct((bsz, N_EXPERTS, cap, 1), F32),
        ],
        compiler_params=_cparams(("arbitrary", "arbitrary", "arbitrary")),
        name="gather",
    )(offs, small, h2, slot.reshape(bsz, N_EXPERTS, 1, n), aff.reshape(bsz, N_EXPERTS, 1, n))


def _up_kernel(*refs, has_ctx):
    if has_ctx:
        xl_ref, xc_ref, wg_ref, wu_ref, h_ref, xs_ref = refs
        cl = xl_ref.shape[2]

        @pl.when(pl.program_id(2) == 0)
        def _():
            xs_ref[0:cl, :] = xl_ref[0, 0]
            xs_ref[cl:, :] = xc_ref[0, 0]

        xs_of = lambda rs: xs_ref[rs, :]
    else:
        xl_ref, wg_ref, wu_ref, h_ref = refs
        xs_of = lambda rs: xl_ref[0, 0, rs, :]
    wg = wg_ref[0, 0].astype(BF16)
    wu = wu_ref[0, 0].astype(BF16)
    rows = h_ref.shape[2]
    for r in range(UP_ROW_SPLIT):
        rs = slice(r * rows // UP_ROW_SPLIT, (r + 1) * rows // UP_ROW_SPLIT)
        xs = xs_of(rs)
        a = jnp.dot(xs, wg, preferred_element_type=F32)
        u = jnp.dot(xs, wu, preferred_element_type=F32)
        h_ref[0, 0, rs, :] = (a * jax.nn.sigmoid(a) * u).astype(BF16)


def _down_kernel(*refs, has_ctx):
    if has_ctx:
        h_ref, vl_ref, vc_ref, wd_ref, y_ref = refs
    else:
        h_ref, vl_ref, wd_ref, y_ref = refs
    cl = vl_ref.shape[2]
    wd = wd_ref[0, 0].astype(BF16)
    rows = h_ref.shape[2]
    half = rows // 2
    res = jnp.dot(h_ref[0, 0, 0:half, :], wd, preferred_element_type=F32)
    y_ref[0, 0, 0:half, :] = (res * vl_ref[0, 0, 0:half, :]).astype(BF16)
    val = vl_ref[0, 0, half:, :]
    if has_ctx:
        val = jnp.concatenate([val, vc_ref[0, 0]], axis=0)
    res = jnp.dot(h_ref[0, 0, half:, :], wd, preferred_element_type=F32)
    y_ref[0, 0, half:, :] = (res * val).astype(BF16)


def _mlp(layer, xl, vl, xc, vc, wg, wu, wd):
    bsz, ne, cl, d = xl.shape
    fdim = wg.shape[-1]
    has_ctx = xc is not None
    cc = xc.shape[2] if has_ctx else 0
    rows = cl + cc
    fc = 1024
    dn = 1024

    xspecs = [pl.BlockSpec((1, 1, cl, d), lambda e, b, f: (b, e, 0, 0))]
    if has_ctx:
        xspecs.append(pl.BlockSpec((1, 1, cc, d), lambda e, b, f: (b, e, 0, 0)))
    hid = pl.pallas_call(
        functools.partial(_up_kernel, has_ctx=has_ctx),
        grid=(ne, bsz, fdim // fc),
        in_specs=xspecs + [
            pl.BlockSpec((1, 1, d, fc), lambda e, b, f: (layer, e, 0, f)),
            pl.BlockSpec((1, 1, d, fc), lambda e, b, f: (layer, e, 0, f)),
        ],
        out_specs=pl.BlockSpec((1, 1, rows, fc), lambda e, b, f: (b, e, 0, f)),
        out_shape=jax.ShapeDtypeStruct((bsz, ne, rows, fdim), BF16),
        scratch_shapes=[pltpu.VMEM((rows, d), BF16)] if has_ctx else [],
        compiler_params=_cparams(("arbitrary", "arbitrary", "arbitrary")),
        name="experts_up",
    )(*((xl, xc) if has_ctx else (xl,)), wg, wu)

    vspecs = [pl.BlockSpec((1, 1, cl, 1), lambda e, b, j: (b, e, 0, 0))]
    if has_ctx:
        vspecs.append(pl.BlockSpec((1, 1, cc, 1), lambda e, b, j: (b, e, 0, 0)))
    return pl.pallas_call(
        functools.partial(_down_kernel, has_ctx=has_ctx),
        grid=(ne, bsz, d // dn),
        in_specs=[pl.BlockSpec((1, 1, rows, fdim), lambda e, b, j: (b, e, 0, 0))] + vspecs + [
            pl.BlockSpec((1, 1, fdim, dn), lambda e, b, j: (layer, e, 0, j)),
        ],
        out_specs=pl.BlockSpec((1, 1, rows, dn), lambda e, b, j: (b, e, 0, j)),
        out_shape=jax.ShapeDtypeStruct((bsz, ne, rows, d), BF16),
        compiler_params=_cparams(("arbitrary", "arbitrary", "arbitrary")),
        name="experts_down",
    )(hid, *((vl, vc) if has_ctx else (vl,)), wd)


def _combine_window(cap, tile):
    return min(tile, cap)


def _combine_kernel(offs_ref, ovf_ref, small_ref, x1_ref, g2_ref, sl_ref, y_hbm, o_ref, ywin_ref, yext_ref,
                    sems, semx, *, cap, tile, n_tiles, base):
    b, t = pl.program_id(0), pl.program_id(1)
    n_steps = pl.num_programs(0) * n_tiles
    win = _combine_window(cap, tile)
    small = min(COMBINE_SMALL, win)
    step = b * n_tiles + t
    buf = lax.rem(step, 2)

    def start_of(bb, tt, e, w):
        return _window_start(offs_ref[(bb * N_EXPERTS + e) * n_tiles + tt], cap, w)

    def window_copy(bb, tt, bf, e, w):
        src = pl.multiple_of(base + start_of(bb, tt, e, w), BF16_ROWS)
        return pltpu.make_async_copy(y_hbm.at[bb, e, pl.ds(src, w), :],
                                     ywin_ref.at[bf, pl.ds(e * w, w), :], sems.at[bf, e])

    def by_window(stp, fn):
        if small == win:
            fn(win)
        else:
            fits = small_ref[stp] > 0
            pl.when(fits)(functools.partial(fn, small))
            pl.when(jnp.logical_not(fits))(functools.partial(fn, win))

    def start_windows(bb, tt, bf, w):
        for e in range(N_EXPERTS):
            window_copy(bb, tt, bf, e, w).start()

    @pl.when(step == 0)
    def _():
        by_window(step, functools.partial(start_windows, b, t, buf))

    @pl.when(step < n_steps - 1)
    def _():
        wrap = t == n_tiles - 1
        b_next = jnp.where(wrap, b + 1, b)
        t_next = jnp.where(wrap, 0, t + 1)
        by_window(step + 1, functools.partial(start_windows, b_next, t_next, 1 - buf))

    sl = sl_ref[0] + 1
    eye = (lax.broadcasted_iota(I32, (tile, tile), 0)
           == lax.broadcasted_iota(I32, (tile, tile), 1)).astype(BF16)
    d_hi = lax.shift_right_logical(sl, 6).astype(F32).astype(BF16)
    d_lo = (sl & 63).astype(F32).astype(BF16)
    slot_t = (lax.dot_general(eye, d_hi, NT_DIMS, preferred_element_type=F32) * 64.0
              + lax.dot_general(eye, d_lo, NT_DIMS, preferred_element_type=F32)).astype(I32) - 1

    def accumulate(w):
        lane = lax.broadcasted_iota(I32, (tile, w), 1)
        hits = [(lane == slot_t[:, e:e + 1] - start_of(b, t, e, w)).astype(BF16) for e in range(N_EXPERTS)]
        for e in range(N_EXPERTS):
            window_copy(b, t, buf, e, w).wait()
        acc = None
        for g in range(0, N_EXPERTS, COMBINE_GROUP):
            part = jnp.dot(jnp.concatenate(hits[g:g + COMBINE_GROUP], axis=1),
                           ywin_ref[buf, g * w:(g + COMBINE_GROUP) * w, :], preferred_element_type=F32)
            acc = part if acc is None else acc + part
        o_ref[0] = x1_ref[0] + g2_ref[0] * acc

    by_window(step, accumulate)

    if win < cap:
        @pl.when(ovf_ref[step] > 0)
        def _():
            def ext_start(e):
                s0 = start_of(b, t, e, win)
                return s0, pl.multiple_of(jnp.minimum(s0 + win, cap - BF16_ROWS), BF16_ROWS)

            def ext_copy(e):
                src = pl.multiple_of(base + ext_start(e)[1], BF16_ROWS)
                return pltpu.make_async_copy(y_hbm.at[b, e, pl.ds(src, BF16_ROWS), :],
                                             yext_ref.at[pl.ds(e * BF16_ROWS, BF16_ROWS), :], semx.at[e])

            for e in range(N_EXPERTS):
                ext_copy(e).start()
            lane_x = lax.broadcasted_iota(I32, (tile, BF16_ROWS), 1)
            hits_x = []
            for e in range(N_EXPERTS):
                ext_copy(e).wait()
                s0, s1 = ext_start(e)
                row = s1 + lane_x
                hits_x.append(((row == slot_t[:, e:e + 1]) & (row >= s0 + win)).astype(BF16))
            extra = jnp.dot(jnp.concatenate(hits_x, axis=1), yext_ref[...], preferred_element_type=F32)
            o_ref[0] += g2_ref[0] * extra


def _combine(offs, ovf, small, x1, g2, slot, y, cap, tile, base):
    bsz, n, d = x1.shape
    n_tiles = n // tile
    win = _combine_window(cap, tile)
    kern = functools.partial(_combine_kernel, cap=cap, tile=tile, n_tiles=n_tiles, base=base)
    return pl.pallas_call(
        kern,
        grid_spec=pltpu.PrefetchScalarGridSpec(
            num_scalar_prefetch=3,
            grid=(bsz, n_tiles),
            in_specs=[
                pl.BlockSpec((1, tile, d), lambda b, t, *_: (b, t, 0)),
                pl.BlockSpec((1, 1, d), lambda b, t, *_: (b, 0, 0)),
                pl.BlockSpec((1, N_EXPERTS, tile), lambda b, t, *_: (b, 0, t)),
                pl.BlockSpec(memory_space=pl.ANY),
            ],
            out_specs=pl.BlockSpec((1, tile, d), lambda b, t, *_: (b, t, 0)),
            scratch_shapes=[pltpu.VMEM((2, N_EXPERTS * win, d), BF16),
                            pltpu.VMEM((N_EXPERTS * BF16_ROWS, d), BF16),
                            pltpu.SemaphoreType.DMA((2, N_EXPERTS)),
                            pltpu.SemaphoreType.DMA((N_EXPERTS,))],
        ),
        out_shape=jax.ShapeDtypeStruct((bsz, n, d), F32),
        compiler_params=_cparams(("arbitrary", "arbitrary")),
        name="combine",
    )(offs, ovf, small, x1, g2, slot, y)


def _deinterleave(w, n_heads):
    lead = w.shape[:-1]
    w = w.reshape(lead + (n_heads, HEAD_DIM // 2, 2))
    return jnp.swapaxes(w, -1, -2).reshape(lead + (n_heads * HEAD_DIM,))


def _rope_tables(n):
    rows = n // GRID_W
    row = jnp.repeat(jnp.arange(rows), GRID_W).astype(F32)
    col = jnp.tile(jnp.arange(GRID_W), rows).astype(F32)
    n_freq = HEAD_DIM // 4
    inv = ROPE_THETA ** (-jnp.arange(n_freq, dtype=F32) / n_freq)
    ang = jnp.concatenate([row[:, None] * inv, col[:, None] * inv], axis=-1)
    cos = jnp.tile(jnp.cos(ang), (1, 4))
    sin = jnp.sin(ang)
    sin_signed = jnp.tile(jnp.concatenate([-sin, sin], axis=-1), (1, 2))
    return cos, sin_signed


def _split_bf16(w):
    hi = w.astype(BF16)
    return hi, (w - hi.astype(F32)).astype(BF16)


def _route_select(aff, cap, gather_tile, combine_tile):
    bsz, ne, n = aff.shape
    n_pad = max(n, HALO * LANES)
    a = aff if n_pad == n else jnp.pad(aff, ((0, 0), (0, 0), (0, n_pad - n)))
    slot4, offc4 = _select(a.reshape(bsz, ne, n_pad // LANES, LANES), cap)

    def tile_tables(tile):
        n_tiles = n // tile
        offs = offc4[:, :, ::tile // LANES, 0][:, :, :n_tiles]
        ends = jnp.concatenate([offs[:, :, 1:], jnp.full((bsz, ne, 1), cap, I32)], axis=2)

        def fits(win):
            return ends <= jnp.minimum((offs >> 4) << 4, cap - win) + win

        return offs.reshape(-1), fits

    g_offs, g_fits = tile_tables(gather_tile)
    gather = {
        "offs": g_offs,
        "small": jnp.all(g_fits(min(GATHER_SMALL, _window(cap, gather_tile)))
                         .reshape(bsz, ne // GATHER_EXPERTS, GATHER_EXPERTS, -1), axis=2).astype(I32).reshape(-1),
    }
    c_offs, c_fits = tile_tables(combine_tile)
    win_c = _combine_window(cap, combine_tile)
    combine = {
        "offs": c_offs,
        "overflow": jnp.any(~c_fits(win_c), axis=1).astype(I32).reshape(-1),
        "small": jnp.all(c_fits(min(COMBINE_SMALL, win_c)), axis=1).astype(I32).reshape(-1),
    }
    return slot4.reshape(bsz, ne, n_pad)[:, :, :n], gather, combine


def kernel(x, c, ctx, c_ctx, w_ada, b_ada, norm1_g, norm2_g, pool_w, pool_scale, attn_w_qkv, attn_w_o,
           attn_q_norm, attn_k_norm, attn_sink, router_w, exp_w_gate, exp_w_up, exp_w_down):
    bsz, n, d = x.shape
    lc = ctx.shape[1]
    cap = 2 * n // N_EXPERTS
    cap_c = 2 * lc // N_EXPERTS
    tile_c = min(COMBINE_TILE, lc)

    cc = jnp.zeros((8, d), F32).at[:bsz].set(c).at[bsz].set(c_ctx)
    mods = _ada(cc, w_ada, b_ada)

    cos, sin = _rope_tables(n)
    cos_c = jnp.ones((lc, LANES), F32)
    sin_c = jnp.zeros((lc, LANES), F32)
    blk = jnp.arange(256) // HEAD_DIM
    bavg = ((blk[:, None] == blk[None, :]).astype(F32) / HEAD_DIM).astype(BF16)

    def layer_mods(i):
        m = [mods[i, :, k * d:(k + 1) * d] for k in range(6)]
        lat = [v[:bsz, None, :] for v in m]
        con = [jnp.broadcast_to(v[bsz][None, None, :], (bsz, 1, d)) for v in m]
        return lat, con

    def qkv_params(i):
        j = i // 2
        nq = N_Q_HEADS * HEAD_DIM
        nk = N_KV_HEADS * HEAD_DIM
        wqkv = attn_w_qkv[j]
        wperm = jnp.concatenate([_deinterleave(wqkv[:, :nq], N_Q_HEADS),
                                 _deinterleave(wqkv[:, nq:nq + nk], N_KV_HEADS),
                                 wqkv[:, nq + nk:]], axis=1).astype(BF16)
        qg = jnp.tile(_deinterleave(attn_q_norm[j], 1), 4)[None, :]
        kg = jnp.tile(_deinterleave(attn_k_norm[j], 1), 4)[None, :]
        return wperm, bavg, qg, kg

    for i in range(DEPTH):
        update_ctx = i < DEPTH - 1
        j = i // 2
        is_pool = (i % 2) == 0
        (sh1, sc1, g1, sh2, sc2, g2), (csh1, csc1, cg1, csh2, csc2, cg2) = layer_mods(i)
        n1g = norm1_g[i][None, :]
        n2g = norm2_g[i][None, :]
        wrh, wrl = _split_bf16(router_w[i].T)

        if is_pool:
            pw = pool_w[j].astype(BF16)
            ps = pool_scale[j][None, :]
            x1, h2, aff = _pool_layer(x, n1g, sh1, sc1, g1, pw, ps, n2g, sh2, sc2, wrh, wrl, ROW_TILE)
            if update_ctx:
                c1, hc2, affc = _pool_layer(ctx, n1g, csh1, csc1, cg1, pw, ps, n2g, csh2, csc2, wrh, wrl, lc)
        else:
            wperm, _, qg, kg = qkv_params(i)
            wo = attn_w_o[j].astype(BF16)
            sink = attn_sink[j]
            bound = 1.01 * 8.0 * jnp.max(jnp.abs(attn_q_norm[j])) * jnp.max(jnp.abs(attn_k_norm[j]))
            q, k, v = _qkv_layer(x, n1g, sh1, sc1, wperm, bavg, qg, kg, cos, sin, ROW_TILE)
            qc, kc, vc = _qkv_layer(ctx, n1g, csh1, csc1, wperm, bavg, qg, kg, cos_c, sin_c, lc)
            o = _attn_layer(sink, bound, q, k, v, kc, vc, True)
            x1, h2, aff = _oproj_layer(o, x, wo, g1, n2g, sh2, sc2, wrh, wrl, ROW_TILE)
            if update_ctx:
                oc = _attn_layer(sink, bound, qc, None, None, kc, vc, False)
                c1, hc2, affc = _oproj_layer(oc, ctx, wo, cg1, n2g, csh2, csc2, wrh, wrl, lc)

        slot, tg, tc = _route_select(aff, cap, GATHER_TILE, COMBINE_TILE)
        xs, vals = _gather(tg["offs"], tg["small"], h2, slot, aff, cap, GATHER_TILE)
        if update_ctx:
            slot_c, cg, cc_ = _route_select(affc, cap_c, tile_c, tile_c)
            xs_c, vals_c = _gather(cg["offs"], cg["small"], hc2, slot_c, affc, cap_c, tile_c)
            y = _mlp(i, xs, vals, xs_c, vals_c, exp_w_gate, exp_w_up, exp_w_down)
            ctx = _combine(cc_["offs"], cc_["overflow"], cc_["small"], c1, cg2, slot_c, y, cap_c, tile_c, cap)
        else:
            y = _mlp(i, xs, vals, None, None, exp_w_gate, exp_w_up, exp_w_down)
        x = _combine(tc["offs"], tc["overflow"], tc["small"], x1, g2, slot, y, cap, COMBINE_TILE, 0)
    return x
```

```python
import functools

import jax
import jax.numpy as jnp
from jax import lax
from jax.experimental import pallas as pl
from jax.experimental.pallas import tpu as pltpu

F32 = jnp.float32
BF16 = jnp.bfloat16
I32 = jnp.int32

D_MODEL = 1024
DEPTH = 4
GRID_W = 64
POOL_WINDOWS = (2, 4, 8, 16)
POOL_GROUP_DIM = 256
HEAD_DIM = 64
N_Q_HEADS = 16
N_KV_HEADS = 4
Q_PER_KV = 4
WINDOW = 128
BLOCK_Q = 128
ROPE_THETA = 10000.0
N_EXPERTS = 16
D_EXPERT = 2048
NORM_EPS = 1e-6
NEG_INF = -1e30

LANES = 128
BF16_ROWS = 16
HALO = 8
ROW_TILE = 1024
MAX_SHIFT = 40.0
SOFTMAX_ROWS = 32
GATHER_EXPERTS = 8
COMBINE_GROUP = 4
UP_ROW_SPLIT = 2
GATHER_TILE, GATHER_SMALL = 256, 64
COMBINE_TILE, COMBINE_SMALL = 256, 64
VMEM_LIMIT = 56 * 1024 * 1024

NT_DIMS = (((1,), (1,)), ((), ()))


def _cparams(sem, vmem=None):
    return pltpu.CompilerParams(dimension_semantics=sem, vmem_limit_bytes=vmem or VMEM_LIMIT)


def _norm_mod(x, g, shift, scale):
    ms = jnp.mean(x * x, axis=-1, keepdims=True)
    return (x * lax.rsqrt(ms + NORM_EPS)) * g * (1.0 + scale) + shift


def _ada_kernel(cc_ref, w_ref, b_ref, o_ref):
    cc = cc_ref[...]
    s = cc * jax.nn.sigmoid(cc)
    o_ref[0] = jnp.dot(s, w_ref[0], preferred_element_type=F32,
                       precision=lax.Precision.HIGHEST) + b_ref[0]


def _ada(cc, w_ada, b_ada):
    depth, d, six_d = w_ada.shape
    tn = 1536
    return pl.pallas_call(
        _ada_kernel,
        grid=(depth, six_d // tn),
        in_specs=[
            pl.BlockSpec((8, d), lambda i, j: (0, 0)),
            pl.BlockSpec((1, d, tn), lambda i, j: (i, 0, j)),
            pl.BlockSpec((1, 1, tn), lambda i, j: (i, 0, j)),
        ],
        out_specs=pl.BlockSpec((1, 8, tn), lambda i, j: (i, 0, j)),
        out_shape=jax.ShapeDtypeStruct((depth, 8, six_d), F32),
        compiler_params=_cparams(("arbitrary", "arbitrary")),
        name="ada",
    )(cc, w_ada, b_ada.reshape(depth, 1, six_d))


def _route(x1, n2g, sh2, sc2, wr_hi, wr_lo):
    h2 = _norm_mod(x1, n2g, sh2, sc2)
    h_hi = h2.astype(BF16)
    h_lo = (h2 - h_hi.astype(F32)).astype(BF16)
    lg = (lax.dot_general(wr_hi, h_hi, NT_DIMS, preferred_element_type=F32)
          + lax.dot_general(wr_hi, h_lo, NT_DIMS, preferred_element_type=F32)
          + lax.dot_general(wr_lo, h_hi, NT_DIMS, preferred_element_type=F32))
    m = jnp.max(lg, axis=0, keepdims=True)
    ex = jnp.exp(lg - m)
    aff = ex / jnp.sum(ex, axis=0, keepdims=True)
    return h_hi, aff


def _pool_kernel(x_ref, xp_ref, xn_ref, n1g_ref, sh1_ref, sc1_ref, g1_ref, pw_ref, ps_ref,
                 n2g_ref, sh2_ref, sc2_ref, wrh_ref, wrl_ref,
                 x1_ref, h2_ref, aff_ref, hs_ref, p1_ref, p2_ref, p3_ref, *, n_tokens):
    t = pl.program_id(1)
    nt = pl.num_programs(1)
    tt = x_ref.shape[1]
    gd = POOL_GROUP_DIM
    n1g, sh1, sc1 = n1g_ref[...], sh1_ref[0], sc1_ref[0]
    x = x_ref[0]
    h = _norm_mod(x, n1g, sh1, sc1)
    hp = _norm_mod(xp_ref[0], n1g, sh1, sc1) * (t > 0).astype(F32)
    hn = _norm_mod(xn_ref[0], n1g, sh1, sc1) * (t < nt - 1).astype(F32)
    ext = tt + 2 * HALO
    hs_ref[0:HALO, :] = hp
    hs_ref[HALO:HALO + tt, :] = h
    hs_ref[HALO + tt:ext, :] = hn
    for ref in (hs_ref, p1_ref, p2_ref):
        ref[ext:, :] = jnp.zeros((ref.shape[0] - ext, ref.shape[1]), F32)
    p1_ref[0:ext, :] = hs_ref[0:ext, gd:] + hs_ref[1:ext + 1, gd:]
    p2_ref[0:ext, :] = p1_ref[0:ext, gd:] + p1_ref[2:ext + 2, gd:]
    p3_ref[...] = p2_ref[0:ext, gd:] + p2_ref[4:ext + 4, gd:]
    sums = [
        hs_ref[HALO - 1:HALO - 1 + tt, 0:gd] + hs_ref[HALO:HALO + tt, 0:gd],
        p1_ref[HALO - 2:HALO - 2 + tt, 0:gd] + p1_ref[HALO:HALO + tt, 0:gd],
        p2_ref[HALO - 4:HALO - 4 + tt, 0:gd] + p2_ref[HALO:HALO + tt, 0:gd],
        p3_ref[0:tt, :] + p3_ref[HALO:HALO + tt, :],
    ]
    pos = t * tt + lax.broadcasted_iota(I32, (tt, 1), 0)
    ys = []
    for g, w in enumerate(POOL_WINDOWS):
        half = w // 2
        c0 = g * POOL_GROUP_DIM
        s = sums[g]
        lo = jnp.clip(pos - half, 0, n_tokens)
        hi = jnp.clip(pos + half, 0, n_tokens)
        p = s / (hi - lo).astype(F32) - h[:, c0:c0 + POOL_GROUP_DIM]
        ys.append(jnp.dot(p.astype(BF16), pw_ref[g], preferred_element_type=F32))
    y = jnp.concatenate(ys, axis=1) * ps_ref[...]
    x1 = x + g1_ref[0] * y
    x1_ref[0] = x1
    h2, aff = _route(x1, n2g_ref[...], sh2_ref[0], sc2_ref[0], wrh_ref[...], wrl_ref[...])
    h2_ref[0] = h2
    aff_ref[0] = aff


def _row_spec(d):
    return pl.BlockSpec((1, d), lambda b, t: (0, 0))


def _mod_spec(d):
    return pl.BlockSpec((1, 1, d), lambda b, t: (b, 0, 0))


def _pool_layer(x, n1g, sh1, sc1, g1, pw, ps, n2g, sh2, sc2, wrh, wrl, tt):
    bsz, n, d = x.shape
    assert POOL_WINDOWS == (2, 4, 8, 16) and d == 4 * POOL_GROUP_DIM
    nb8 = n // HALO
    r = tt // HALO
    kern = functools.partial(_pool_kernel, n_tokens=n)
    return pl.pallas_call(
        kern,
        grid=(bsz, n // tt),
        in_specs=[
            pl.BlockSpec((1, tt, d), lambda b, t: (b, t, 0)),
            pl.BlockSpec((1, HALO, d), lambda b, t: (b, jnp.maximum(t * r - 1, 0), 0)),
            pl.BlockSpec((1, HALO, d), lambda b, t: (b, jnp.minimum((t + 1) * r, nb8 - 1), 0)),
            _row_spec(d), _mod_spec(d), _mod_spec(d), _mod_spec(d),
            pl.BlockSpec((4, POOL_GROUP_DIM, POOL_GROUP_DIM), lambda b, t: (0, 0, 0)),
            _row_spec(d),
            _row_spec(d), _mod_spec(d), _mod_spec(d),
            pl.BlockSpec((N_EXPERTS, d), lambda b, t: (0, 0)),
            pl.BlockSpec((N_EXPERTS, d), lambda b, t: (0, 0)),
        ],
        out_specs=[
            pl.BlockSpec((1, tt, d), lambda b, t: (b, t, 0)),
            pl.BlockSpec((1, tt, d), lambda b, t: (b, t, 0)),
            pl.BlockSpec((1, N_EXPERTS, tt), lambda b, t: (b, 0, t)),
        ],
        out_shape=[
            jax.ShapeDtypeStruct((bsz, n, d), F32),
            jax.ShapeDtypeStruct((bsz, n, d), BF16),
            jax.ShapeDtypeStruct((bsz, N_EXPERTS, n), F32),
        ],
        scratch_shapes=[pltpu.VMEM((tt + 3 * HALO, d), F32),
                        pltpu.VMEM((tt + 3 * HALO, d - POOL_GROUP_DIM), F32),
                        pltpu.VMEM((tt + 3 * HALO, d - 2 * POOL_GROUP_DIM), F32),
                        pltpu.VMEM((tt + 2 * HALO, d - 3 * POOL_GROUP_DIM), F32)],
        compiler_params=_cparams(("arbitrary", "arbitrary")),
        name="pool_route",
    )(x, x, x, n1g, sh1, sc1, g1, pw, ps, n2g, sh2, sc2, wrh, wrl)


def _head_rms(v, bavg, gain):
    sq = v * v
    sq_hi = sq.astype(BF16)
    sq_lo = (sq - sq_hi.astype(F32)).astype(BF16)
    ms = (jnp.dot(sq_hi, bavg, preferred_element_type=F32)
          + jnp.dot(sq_lo, bavg, preferred_element_type=F32))
    return v * lax.rsqrt(ms + NORM_EPS) * gain


def _rope(v, cos, sin_signed):
    lane = lax.broadcasted_iota(I32, v.shape, 1)
    fwd = pltpu.roll(v, 32, 1)
    bwd = pltpu.roll(v, 96, 1)
    partner = jnp.where((lane % HEAD_DIM) < HEAD_DIM // 2, bwd, fwd)
    return v * cos + partner * sin_signed


def _qkv_kernel(x_ref, n1g_ref, sh1_ref, sc1_ref, w_ref, bavg_ref, qg_ref, kg_ref,
                cos_ref, sin_ref, q_ref, k_ref, v_ref):
    _qkv_body(x_ref[0], n1g_ref, sh1_ref, sc1_ref, w_ref, bavg_ref, qg_ref, kg_ref,
              cos_ref, sin_ref, q_ref, k_ref, v_ref)


def _qkv_body(x, n1g_ref, sh1_ref, sc1_ref, w_ref, bavg_ref, qg_ref, kg_ref,
              cos_ref, sin_ref, q_ref, k_ref, v_ref):
    h = _norm_mod(x, n1g_ref[...], sh1_ref[0], sc1_ref[0]).astype(BF16)
    qkv = jnp.dot(h, w_ref[...], preferred_element_type=F32)
    nq = N_Q_HEADS * HEAD_DIM
    nk = N_KV_HEADS * HEAD_DIM
    bavg = bavg_ref[...]
    cos, sin = cos_ref[...], sin_ref[...]
    scale = HEAD_DIM ** -0.5
    for c in range(nq // 256):
        qn = _head_rms(qkv[:, c * 256:(c + 1) * 256], bavg, qg_ref[...])
        for s in range(2):
            qr = _rope(qn[:, s * LANES:(s + 1) * LANES], cos, sin) * scale
            for hh in range(2):
                head = c * 4 + s * 2 + hh
                q_ref[0, head] = qr[:, hh * HEAD_DIM:(hh + 1) * HEAD_DIM].astype(BF16)
    kn = _head_rms(qkv[:, nq:nq + nk], bavg, kg_ref[...])
    v = qkv[:, nq + nk:]
    for s in range(2):
        kr = _rope(kn[:, s * LANES:(s + 1) * LANES], cos, sin)
        for hh in range(2):
            head = s * 2 + hh
            k_ref[0, head] = kr[:, hh * HEAD_DIM:(hh + 1) * HEAD_DIM].astype(BF16)
            v_ref[0, head] = v[:, head * HEAD_DIM:(head + 1) * HEAD_DIM].astype(BF16)


def _qkv_layer(x, n1g, sh1, sc1, w, bavg, qg, kg, cos, sin, tt):
    bsz, n, d = x.shape
    return pl.pallas_call(
        _qkv_kernel,
        grid=(bsz, n // tt),
        in_specs=[
            pl.BlockSpec((1, tt, d), lambda b, t: (b, t, 0)),
            _row_spec(d), _mod_spec(d), _mod_spec(d),
            pl.BlockSpec(w.shape, lambda b, t: (0, 0)),
            pl.BlockSpec((256, 256), lambda b, t: (0, 0)),
            _row_spec(256), _row_spec(256),
            pl.BlockSpec((tt, LANES), lambda b, t: (t, 0)),
            pl.BlockSpec((tt, LANES), lambda b, t: (t, 0)),
        ],
        out_specs=[
            pl.BlockSpec((1, N_Q_HEADS, tt, HEAD_DIM), lambda b, t: (b, 0, t, 0)),
            pl.BlockSpec((1, N_KV_HEADS, tt, HEAD_DIM), lambda b, t: (b, 0, t, 0)),
            pl.BlockSpec((1, N_KV_HEADS, tt, HEAD_DIM), lambda b, t: (b, 0, t, 0)),
        ],
        out_shape=[
            jax.ShapeDtypeStruct((bsz, N_Q_HEADS, n, HEAD_DIM), BF16),
            jax.ShapeDtypeStruct((bsz, N_KV_HEADS, n, HEAD_DIM), BF16),
            jax.ShapeDtypeStruct((bsz, N_KV_HEADS, n, HEAD_DIM), BF16),
        ],
        compiler_params=_cparams(("arbitrary", "arbitrary")),
        name="qkv",
    )(x, n1g, sh1, sc1, w, bavg, qg, kg, cos, sin)


def _attn_kernel(sink_ref, q_ref, *refs, band, bounded):
    if band:
        kp_ref, kc_ref, kn_ref, vp_ref, vc_ref, vn_ref, kx_ref, vx_ref, bias_ref, o_ref = refs[:10]
    else:
        kx_ref, vx_ref, o_ref = refs[:3]
    if not bounded:
        s_all, p_all = refs[-2:]
    tq = q_ref.shape[2]
    n_chunks = Q_PER_KV * tq // SOFTMAX_ROWS
    for g in range(N_KV_HEADS):
        if band:
            kg = jnp.concatenate([kp_ref[0, g], kc_ref[0, g], kn_ref[0, g], kx_ref[0, g]], axis=0)
            vg = jnp.concatenate([vp_ref[0, g], vc_ref[0, g], vn_ref[0, g], vx_ref[0, g]], axis=0)
        else:
            kg = kx_ref[0, g]
            vg = vx_ref[0, g]
        vext = jnp.concatenate([vg, jnp.ones_like(vg)], axis=1)
        qg = q_ref[0, g * Q_PER_KV:(g + 1) * Q_PER_KV].reshape(Q_PER_KV * tq, HEAD_DIM)
        if bounded:
            shift = sink_ref[N_Q_HEADS]
            s = lax.dot_general(qg, kg, NT_DIMS, preferred_element_type=F32)
            if band:
                s = s + bias_ref[0]
            p = jnp.exp(s - shift).astype(BF16)
            sink_term = jnp.exp(jnp.concatenate(
                [jnp.full((tq, 1), sink_ref[g * Q_PER_KV + hh] - shift, F32) for hh in range(Q_PER_KV)], axis=0))
        else:
            s_ref, p_ref = s_all.at[g], p_all.at[g]
            s_ref[...] = lax.dot_general(qg, kg, NT_DIMS, preferred_element_type=F32)
            sink_terms = []
            for r in range(n_chunks):
                rs = slice(r * SOFTMAX_ROWS, (r + 1) * SOFTMAX_ROWS)
                sc = s_ref[rs, :]
                if band:
                    sc = sc + bias_ref[0, rs, :]
                sk = sink_ref[g * Q_PER_KV + (r * SOFTMAX_ROWS) // tq]
                m = jnp.maximum(jnp.max(sc, axis=-1, keepdims=True), sk)
                p_ref[rs, :] = jnp.exp(sc - m).astype(BF16)
                sink_terms.append(jnp.exp(sk - m))
            p = p_ref[...]
            sink_term = jnp.concatenate(sink_terms, axis=0)
        oe = jnp.dot(p, vext, preferred_element_type=F32)
        den = oe[:, HEAD_DIM:HEAD_DIM + 1] + sink_term
        o = (oe[:, :HEAD_DIM] / den).astype(o_ref.dtype)
        for hh in range(Q_PER_KV):
            head = g * Q_PER_KV + hh
            o_ref[0, :, head * HEAD_DIM:(head + 1) * HEAD_DIM] = o[hh * tq:(hh + 1) * tq]


def _band_bias(lc):
    nband = BLOCK_Q + 2 * WINDOW
    row = jnp.arange(Q_PER_KV * BLOCK_Q)[:, None] % BLOCK_Q
    col = jnp.arange(nband + lc)[None, :]
    dlt = col - row
    inband = (dlt >= 0) & (dlt <= 2 * WINDOW)
    variants = [inband & (col >= WINDOW), inband, inband & (col < WINDOW + BLOCK_Q)]
    return jnp.stack([jnp.where(v | (col >= nband), 0.0, NEG_INF) for v in variants]).astype(F32)


def _attn_layer(sink, score_bound, q, k, v, kx, vx, band):
    sink_ext = jnp.concatenate([sink.astype(F32), score_bound.reshape(1).astype(F32)])
    if not band:
        k, v = kx, vx
    return lax.cond(score_bound <= MAX_SHIFT,
                    functools.partial(_attn_call, band=band, bounded=True),
                    functools.partial(_attn_call, band=band, bounded=False),
                    sink_ext, q, k, v, kx, vx)


def _attn_call(sink, q, k, v, kx, vx, *, band, bounded):
    bsz, _, n, _ = q.shape
    lc = kx.shape[2]
    tq = BLOCK_Q if band else n
    nb = n // tq
    n_keys = (BLOCK_Q + 2 * WINDOW + lc) if band else lc
    qspec = pl.BlockSpec((1, N_Q_HEADS, tq, HEAD_DIM), lambda b, t: (b, 0, t, 0))
    xspec = pl.BlockSpec((1, N_KV_HEADS, lc, HEAD_DIM), lambda b, t: (b, 0, 0, 0))
    if band:
        prev = pl.BlockSpec((1, N_KV_HEADS, tq, HEAD_DIM), lambda b, t: (b, 0, jnp.maximum(t - 1, 0), 0))
        cur = pl.BlockSpec((1, N_KV_HEADS, tq, HEAD_DIM), lambda b, t: (b, 0, t, 0))
        nxt = pl.BlockSpec((1, N_KV_HEADS, tq, HEAD_DIM), lambda b, t: (b, 0, jnp.minimum(t + 1, nb - 1), 0))
        bias = _band_bias(lc)
        bspec = pl.BlockSpec((1,) + bias.shape[1:],
                             lambda b, t: (jnp.where(t == 0, 0, jnp.where(t == nb - 1, 2, 1)), 0, 0))
        in_specs = [qspec, prev, cur, nxt, prev, cur, nxt, xspec, xspec, bspec]
        args = (q, k, k, k, v, v, v, kx, vx, bias)
    else:
        in_specs = [qspec, xspec, xspec]
        args = (q, kx, vx)
    scratch = [] if bounded else [pltpu.VMEM((N_KV_HEADS, Q_PER_KV * tq, n_keys), F32),
                                  pltpu.VMEM((N_KV_HEADS, Q_PER_KV * tq, n_keys), BF16)]
    return pl.pallas_call(
        functools.partial(_attn_kernel, band=band, bounded=bounded),
        grid=(bsz, nb),
        in_specs=[pl.BlockSpec(memory_space=pltpu.SMEM)] + in_specs,
        out_specs=pl.BlockSpec((1, tq, N_Q_HEADS * HEAD_DIM), lambda b, t: (b, t, 0)),
        out_shape=jax.ShapeDtypeStruct((bsz, n, N_Q_HEADS * HEAD_DIM), BF16),
        scratch_shapes=scratch,
        compiler_params=_cparams(("arbitrary", "arbitrary")),
        name=("attn_band" if band else "attn_ctx") + ("_bounded" if bounded else ""),
    )(sink, *args)


def _oproj_kernel(o_ref, x_ref, wo_ref, g1_ref, n2g_ref, sh2_ref, sc2_ref, wrh_ref, wrl_ref,
                  x1_ref, h2_ref, aff_ref):
    y = jnp.dot(o_ref[0], wo_ref[...], preferred_element_type=F32)
    x1 = x_ref[0] + g1_ref[0] * y
    x1_ref[0] = x1
    h2, aff = _route(x1, n2g_ref[...], sh2_ref[0], sc2_ref[0], wrh_ref[...], wrl_ref[...])
    h2_ref[0] = h2
    aff_ref[0] = aff


def _oproj_layer(o, x, wo, g1, n2g, sh2, sc2, wrh, wrl, tt):
    bsz, n, d = x.shape
    return pl.pallas_call(
        _oproj_kernel,
        grid=(bsz, n // tt),
        in_specs=[
            pl.BlockSpec((1, tt, d), lambda b, t: (b, t, 0)),
            pl.BlockSpec((1, tt, d), lambda b, t: (b, t, 0)),
            pl.BlockSpec((d, d), lambda b, t: (0, 0)),
            _mod_spec(d), _row_spec(d), _mod_spec(d), _mod_spec(d),
            pl.BlockSpec((N_EXPERTS, d), lambda b, t: (0, 0)),
            pl.BlockSpec((N_EXPERTS, d), lambda b, t: (0, 0)),
        ],
        out_specs=[
            pl.BlockSpec((1, tt, d), lambda b, t: (b, t, 0)),
            pl.BlockSpec((1, tt, d), lambda b, t: (b, t, 0)),
            pl.BlockSpec((1, N_EXPERTS, tt), lambda b, t: (b, 0, t)),
        ],
        out_shape=[
            jax.ShapeDtypeStruct((bsz, n, d), F32),
            jax.ShapeDtypeStruct((bsz, n, d), BF16),
            jax.ShapeDtypeStruct((bsz, N_EXPERTS, n), F32),
        ],
        compiler_params=_cparams(("arbitrary", "arbitrary")),
        name="oproj_route",
    )(o, x, wo, g1, n2g, sh2, sc2, wrh, wrl)


def _prefix_counts(mask, n_chunks):
    ne = mask.shape[0]
    m2 = mask.reshape(ne * n_chunks, LANES).astype(BF16)
    ri = lax.broadcasted_iota(I32, (LANES, LANES), 0)
    ci = lax.broadcasted_iota(I32, (LANES, LANES), 1)
    upper = (ri <= ci).astype(BF16)
    within = jnp.dot(m2, upper, preferred_element_type=F32).reshape(ne, n_chunks, LANES)
    tot = jnp.dot(m2, jnp.ones((LANES, LANES), BF16), preferred_element_type=F32).reshape(ne, n_chunks, LANES)
    rc = lax.broadcasted_iota(I32, (n_chunks, n_chunks), 0)
    cc = lax.broadcasted_iota(I32, (n_chunks, n_chunks), 1)
    strict_lower = (cc < rc).astype(F32)
    off = jnp.stack([jnp.dot(strict_lower, tot[e], preferred_element_type=F32) for e in range(ne)])
    return within + off, off


def _select_kernel(a_ref, slot_ref, offc_ref, *, cap):
    a = a_ref[0]
    ne, n_chunks, _ = a.shape
    bits = lax.bitcast_convert_type(a, I32)

    def count(pred):
        c = jnp.sum(pred.astype(F32), axis=1, keepdims=True)
        return jnp.sum(c, axis=2, keepdims=True)

    def body(i, thr):
        cand = thr | lax.shift_left(jnp.int32(1), 30 - i)
        return jnp.where(count(bits >= cand) >= cap, cand, thr)

    thr = lax.fori_loop(0, 31, body, jnp.zeros((ne, 1, 1), I32))
    gt = bits > thr
    eq = bits == thr
    need = cap - count(gt)
    eq_rank, _ = _prefix_counts(eq.astype(F32), n_chunks)
    sel = gt | (eq & (eq_rank <= need))
    sel_f = sel.astype(F32)
    incl, off = _prefix_counts(sel_f, n_chunks)
    slot_ref[0] = jnp.where(sel, (incl - 1.0).astype(I32), -1)
    offc_ref[0] = off.astype(I32)


def _select(aff4, cap):
    bsz, ne, n_chunks, cw = aff4.shape
    spec = pl.BlockSpec((1, ne, n_chunks, cw), lambda b: (b, 0, 0, 0))
    return pl.pallas_call(
        functools.partial(_select_kernel, cap=cap),
        grid=(bsz,),
        in_specs=[spec],
        out_specs=[spec, spec],
        out_shape=[jax.ShapeDtypeStruct(aff4.shape, I32), jax.ShapeDtypeStruct(aff4.shape, I32)],
        compiler_params=_cparams(("arbitrary",)),
        name="select",
    )(aff4)


def _window(cap, tile):
    return min(tile + BF16_ROWS, cap)


def _window_start(off, cap, win):
    start = lax.shift_left(lax.shift_right_logical(off, 4), 4)
    return pl.multiple_of(jnp.minimum(start, cap - win), BF16_ROWS)


def _gather_kernel(offs_ref, small_ref, h2_ref, sl_ref, af_ref, xs_ref, val_ref, *, cap, tile, n_tiles):
    b, eg, s = pl.program_id(0), pl.program_id(1), pl.program_id(2)
    ge = sl_ref.shape[1]
    sub = h2_ref.shape[1] // tile
    full = _window(cap, tile)
    small = min(GATHER_SMALL, full)

    @pl.when(s == 0)
    def _():
        xs_ref[...] = jnp.zeros_like(xs_ref)
        val_ref[...] = jnp.zeros_like(val_ref)

    def tile_rows(u, win):
        ts = slice(u * tile, (u + 1) * tile)
        starts, hits = [], []
        for j in range(ge):
            off = offs_ref[(b * N_EXPERTS + eg * ge + j) * n_tiles + s * sub + u]
            start = _window_start(off, cap, win)
            rows = start + lax.broadcasted_iota(I32, (win, tile), 0)
            starts.append(start)
            hits.append(rows == sl_ref[0, j, :, ts])
        stacked = jnp.concatenate([h.astype(BF16) for h in hits], axis=0)
        res = jnp.dot(stacked, h2_ref[0, ts, :], preferred_element_type=F32)
        for j in range(ge):
            dst = pl.ds(starts[j], win)
            xs_ref[0, j, dst, :] = (xs_ref[0, j, dst, :].astype(F32) + res[j * win:(j + 1) * win]).astype(BF16)
            val_ref[0, j, dst, :] += jnp.sum(jnp.where(hits[j], af_ref[0, j, :, ts], 0.0), axis=1, keepdims=True)

    for u in range(sub):
        if small == full:
            tile_rows(u, full)
        else:
            fits = small_ref[(b * pl.num_programs(1) + eg) * n_tiles + s * sub + u] > 0
            pl.when(fits)(functools.partial(tile_rows, u, small))
            pl.when(jnp.logical_not(fits))(functools.partial(tile_rows, u, full))


def _gather(offs, small, h2, slot, aff, cap, tile):
    bsz, n, d = h2.shape
    tg = min(n, 1024)
    n_tiles = n // tile
    ge = GATHER_EXPERTS
    kern = functools.partial(_gather_kernel, cap=cap, tile=tile, n_tiles=n_tiles)
    return pl.pallas_call(
        kern,
        grid_spec=pltpu.PrefetchScalarGridSpec(
            num_scalar_prefetch=2,
            grid=(bsz, N_EXPERTS // ge, n // tg),
            in_specs=[
                pl.BlockSpec((1, tg, d), lambda b, e, s, offs, small: (b, s, 0)),
                pl.BlockSpec((1, ge, 1, tg), lambda b, e, s, offs, small: (b, e, 0, s)),
                pl.BlockSpec((1, ge, 1, tg), lambda b, e, s, offs, small: (b, e, 0, s)),
            ],
            out_specs=[
                pl.BlockSpec((1, ge, cap, d), lambda b, e, s, offs, small: (b, e, 0, 0),
                             pipeline_mode=pl.Buffered(1)),
                pl.BlockSpec((1, ge, cap, 1), lambda b, e, s, offs, small: (b, e, 0, 0),
                             pipeline_mode=pl.Buffered(1)),
            ],
        ),
        out_shape=[
            jax.ShapeDtypeStruct((bsz, N_EXPERTS, cap, d), BF16),
            jax.ShapeDtypeStruct((bsz, N_EXPERTS, cap, 1), F32),
        ],
        compiler_params=_cparams(("arbitrary", "arbitrary", "arbitrary")),
        name="gather",
    )(offs, small, h2, slot.reshape(bsz, N_EXPERTS, 1, n), aff.reshape(bsz, N_EXPERTS, 1, n))


def _up_kernel(*refs, has_ctx):
    if has_ctx:
        xl_ref, xc_ref, wg_ref, wu_ref, h_ref, xs_ref = refs
        cl = xl_ref.shape[2]

        @pl.when(pl.program_id(2) == 0)
        def _():
            xs_ref[0:cl, :] = xl_ref[0, 0]
            xs_ref[cl:, :] = xc_ref[0, 0]

        xs_of = lambda rs: xs_ref[rs, :]
    else:
        xl_ref, wg_ref, wu_ref, h_ref = refs
        xs_of = lambda rs: xl_ref[0, 0, rs, :]
    wg = wg_ref[0, 0].astype(BF16)
    wu = wu_ref[0, 0].astype(BF16)
    rows = h_ref.shape[2]
    for r in range(UP_ROW_SPLIT):
        rs = slice(r * rows // UP_ROW_SPLIT, (r + 1) * rows // UP_ROW_SPLIT)
        xs = xs_of(rs)
        a = jnp.dot(xs, wg, preferred_element_type=F32)
        u = jnp.dot(xs, wu, preferred_element_type=F32)
        h_ref[0, 0, rs, :] = (a * jax.nn.sigmoid(a) * u).astype(BF16)


def _down_kernel(*refs, has_ctx):
    if has_ctx:
        h_ref, vl_ref, vc_ref, wd_ref, y_ref = refs
    else:
        h_ref, vl_ref, wd_ref, y_ref = refs
    cl = vl_ref.shape[2]
    wd = wd_ref[0, 0].astype(BF16)
    rows = h_ref.shape[2]
    half = rows // 2
    res = jnp.dot(h_ref[0, 0, 0:half, :], wd, preferred_element_type=F32)
    y_ref[0, 0, 0:half, :] = (res * vl_ref[0, 0, 0:half, :]).astype(BF16)
    val = vl_ref[0, 0, half:, :]
    if has_ctx:
        val = jnp.concatenate([val, vc_ref[0, 0]], axis=0)
    res = jnp.dot(h_ref[0, 0, half:, :], wd, preferred_element_type=F32)
    y_ref[0, 0, half:, :] = (res * val).astype(BF16)


def _mlp(layer, xl, vl, xc, vc, wg, wu, wd):
    bsz, ne, cl, d = xl.shape
    fdim = wg.shape[-1]
    has_ctx = xc is not None
    cc = xc.shape[2] if has_ctx else 0
    rows = cl + cc
    fc = 1024
    dn = 1024

    xspecs = [pl.BlockSpec((1, 1, cl, d), lambda e, b, f: (b, e, 0, 0))]
    if has_ctx:
        xspecs.append(pl.BlockSpec((1, 1, cc, d), lambda e, b, f: (b, e, 0, 0)))
    hid = pl.pallas_call(
        functools.partial(_up_kernel, has_ctx=has_ctx),
        grid=(ne, bsz, fdim // fc),
        in_specs=xspecs + [
            pl.BlockSpec((1, 1, d, fc), lambda e, b, f: (layer, e, 0, f)),
            pl.BlockSpec((1, 1, d, fc), lambda e, b, f: (layer, e, 0, f)),
        ],
        out_specs=pl.BlockSpec((1, 1, rows, fc), lambda e, b, f: (b, e, 0, f)),
        out_shape=jax.ShapeDtypeStruct((bsz, ne, rows, fdim), BF16),
        scratch_shapes=[pltpu.VMEM((rows, d), BF16)] if has_ctx else [],
        compiler_params=_cparams(("arbitrary", "arbitrary", "arbitrary")),
        name="experts_up",
    )(*((xl, xc) if has_ctx else (xl,)), wg, wu)

    vspecs = [pl.BlockSpec((1, 1, cl, 1), lambda e, b, j: (b, e, 0, 0))]
    if has_ctx:
        vspecs.append(pl.BlockSpec((1, 1, cc, 1), lambda e, b, j: (b, e, 0, 0)))
    return pl.pallas_call(
        functools.partial(_down_kernel, has_ctx=has_ctx),
        grid=(ne, bsz, d // dn),
        in_specs=[pl.BlockSpec((1, 1, rows, fdim), lambda e, b, j: (b, e, 0, 0))] + vspecs + [
            pl.BlockSpec((1, 1, fdim, dn), lambda e, b, j: (layer, e, 0, j)),
        ],
        out_specs=pl.BlockSpec((1, 1, rows, dn), lambda e, b, j: (b, e, 0, j)),
        out_shape=jax.ShapeDtypeStruct((bsz, ne, rows, d), BF16),
        compiler_params=_cparams(("arbitrary", "arbitrary", "arbitrary")),
        name="experts_down",
    )(hid, *((vl, vc) if has_ctx else (vl,)), wd)


def _combine_window(cap, tile):
    return min(tile, cap)


def _combine_kernel(offs_ref, ovf_ref, small_ref, x1_ref, g2_ref, sl_ref, y_hbm, o_ref, ywin_ref, yext_ref,
                    sems, semx, *, cap, tile, n_tiles, base):
    b, t = pl.program_id(0), pl.program_id(1)
    n_steps = pl.num_programs(0) * n_tiles
    win = _combine_window(cap, tile)
    small = min(COMBINE_SMALL, win)
    step = b * n_tiles + t
    buf = lax.rem(step, 2)

    def start_of(bb, tt, e, w):
        return _window_start(offs_ref[(bb * N_EXPERTS + e) * n_tiles + tt], cap, w)

    def window_copy(bb, tt, bf, e, w):
        src = pl.multiple_of(base + start_of(bb, tt, e, w), BF16_ROWS)
        return pltpu.make_async_copy(y_hbm.at[bb, e, pl.ds(src, w), :],
                                     ywin_ref.at[bf, pl.ds(e * w, w), :], sems.at[bf, e])

    def by_window(stp, fn):
        if small == win:
            fn(win)
        else:
            fits = small_ref[stp] > 0
            pl.when(fits)(functools.partial(fn, small))
            pl.when(jnp.logical_not(fits))(functools.partial(fn, win))

    def start_windows(bb, tt, bf, w):
        for e in range(N_EXPERTS):
            window_copy(bb, tt, bf, e, w).start()

    @pl.when(step == 0)
    def _():
        by_window(step, functools.partial(start_windows, b, t, buf))

    @pl.when(step < n_steps - 1)
    def _():
        wrap = t == n_tiles - 1
        b_next = jnp.where(wrap, b + 1, b)
        t_next = jnp.where(wrap, 0, t + 1)
        by_window(step + 1, functools.partial(start_windows, b_next, t_next, 1 - buf))

    sl = sl_ref[0] + 1
    eye = (lax.broadcasted_iota(I32, (tile, tile), 0)
           == lax.broadcasted_iota(I32, (tile, tile), 1)).astype(BF16)
    d_hi = lax.shift_right_logical(sl, 6).astype(F32).astype(BF16)
    d_lo = (sl & 63).astype(F32).astype(BF16)
    slot_t = (lax.dot_general(eye, d_hi, NT_DIMS, preferred_element_type=F32) * 64.0
              + lax.dot_general(eye, d_lo, NT_DIMS, preferred_element_type=F32)).astype(I32) - 1

    def accumulate(w):
        lane = lax.broadcasted_iota(I32, (tile, w), 1)
        hits = [(lane == slot_t[:, e:e + 1] - start_of(b, t, e, w)).astype(BF16) for e in range(N_EXPERTS)]
        for e in range(N_EXPERTS):
            window_copy(b, t, buf, e, w).wait()
        acc = None
        for g in range(0, N_EXPERTS, COMBINE_GROUP):
            part = jnp.dot(jnp.concatenate(hits[g:g + COMBINE_GROUP], axis=1),
                           ywin_ref[buf, g * w:(g + COMBINE_GROUP) * w, :], preferred_element_type=F32)
            acc = part if acc is None else acc + part
        o_ref[0] = x1_ref[0] + g2_ref[0] * acc

    by_window(step, accumulate)

    if win < cap:
        @pl.when(ovf_ref[step] > 0)
        def _():
            def ext_start(e):
                s0 = start_of(b, t, e, win)
                return s0, pl.multiple_of(jnp.minimum(s0 + win, cap - BF16_ROWS), BF16_ROWS)

            def ext_copy(e):
                src = pl.multiple_of(base + ext_start(e)[1], BF16_ROWS)
                return pltpu.make_async_copy(y_hbm.at[b, e, pl.ds(src, BF16_ROWS), :],
                                             yext_ref.at[pl.ds(e * BF16_ROWS, BF16_ROWS), :], semx.at[e])

            for e in range(N_EXPERTS):
                ext_copy(e).start()
            lane_x = lax.broadcasted_iota(I32, (tile, BF16_ROWS), 1)
            hits_x = []
            for e in range(N_EXPERTS):
                ext_copy(e).wait()
                s0, s1 = ext_start(e)
                row = s1 + lane_x
                hits_x.append(((row == slot_t[:, e:e + 1]) & (row >= s0 + win)).astype(BF16))
            extra = jnp.dot(jnp.concatenate(hits_x, axis=1), yext_ref[...], preferred_element_type=F32)
            o_ref[0] += g2_ref[0] * extra


def _combine(offs, ovf, small, x1, g2, slot, y, cap, tile, base):
    bsz, n, d = x1.shape
    n_tiles = n // tile
    win = _combine_window(cap, tile)
    kern = functools.partial(_combine_kernel, cap=cap, tile=tile, n_tiles=n_tiles, base=base)
    return pl.pallas_call(
        kern,
        grid_spec=pltpu.PrefetchScalarGridSpec(
            num_scalar_prefetch=3,
            grid=(bsz, n_tiles),
            in_specs=[
                pl.BlockSpec((1, tile, d), lambda b, t, *_: (b, t, 0)),
                pl.BlockSpec((1, 1, d), lambda b, t, *_: (b, 0, 0)),
                pl.BlockSpec((1, N_EXPERTS, tile), lambda b, t, *_: (b, 0, t)),
                pl.BlockSpec(memory_space=pl.ANY),
            ],
            out_specs=pl.BlockSpec((1, tile, d), lambda b, t, *_: (b, t, 0)),
            scratch_shapes=[pltpu.VMEM((2, N_EXPERTS * win, d), BF16),
                            pltpu.VMEM((N_EXPERTS * BF16_ROWS, d), BF16),
                            pltpu.SemaphoreType.DMA((2, N_EXPERTS)),
                            pltpu.SemaphoreType.DMA((N_EXPERTS,))],
        ),
        out_shape=jax.ShapeDtypeStruct((bsz, n, d), F32),
        compiler_params=_cparams(("arbitrary", "arbitrary")),
        name="combine",
    )(offs, ovf, small, x1, g2, slot, y)


def _deinterleave(w, n_heads):
    lead = w.shape[:-1]
    w = w.reshape(lead + (n_heads, HEAD_DIM // 2, 2))
    return jnp.swapaxes(w, -1, -2).reshape(lead + (n_heads * HEAD_DIM,))


def _rope_tables(n):
    rows = n // GRID_W
    row = jnp.repeat(jnp.arange(rows), GRID_W).astype(F32)
    col = jnp.tile(jnp.arange(GRID_W), rows).astype(F32)
    n_freq = HEAD_DIM // 4
    inv = ROPE_THETA ** (-jnp.arange(n_freq, dtype=F32) / n_freq)
    ang = jnp.concatenate([row[:, None] * inv, col[:, None] * inv], axis=-1)
    cos = jnp.tile(jnp.cos(ang), (1, 4))
    sin = jnp.sin(ang)
    sin_signed = jnp.tile(jnp.concatenate([-sin, sin], axis=-1), (1, 2))
    return cos, sin_signed


def _split_bf16(w):
    hi = w.astype(BF16)
    return hi, (w - hi.astype(F32)).astype(BF16)


def _route_select(aff, cap, gather_tile, combine_tile):
    bsz, ne, n = aff.shape
    n_pad = max(n, HALO * LANES)
    a = aff if n_pad == n else jnp.pad(aff, ((0, 0), (0, 0), (0, n_pad - n)))
    slot4, offc4 = _select(a.reshape(bsz, ne, n_pad // LANES, LANES), cap)

    def tile_tables(tile):
        n_tiles = n // tile
        offs = offc4[:, :, ::tile // LANES, 0][:, :, :n_tiles]
        ends = jnp.concatenate([offs[:, :, 1:], jnp.full((bsz, ne, 1), cap, I32)], axis=2)

        def fits(win):
            return ends <= jnp.minimum((offs >> 4) << 4, cap - win) + win

        return offs.reshape(-1), fits

    g_offs, g_fits = tile_tables(gather_tile)
    gather = {
        "offs": g_offs,
        "small": jnp.all(g_fits(min(GATHER_SMALL, _window(cap, gather_tile)))
                         .reshape(bsz, ne // GATHER_EXPERTS, GATHER_EXPERTS, -1), axis=2).astype(I32).reshape(-1),
    }
    c_offs, c_fits = tile_tables(combine_tile)
    win_c = _combine_window(cap, combine_tile)
    combine = {
        "offs": c_offs,
        "overflow": jnp.any(~c_fits(win_c), axis=1).astype(I32).reshape(-1),
        "small": jnp.all(c_fits(min(COMBINE_SMALL, win_c)), axis=1).astype(I32).reshape(-1),
    }
    return slot4.reshape(bsz, ne, n_pad)[:, :, :n], gather, combine


def kernel(x, c, ctx, c_ctx, w_ada, b_ada, norm1_g, norm2_g, pool_w, pool_scale, attn_w_qkv, attn_w_o,
           attn_q_norm, attn_k_norm, attn_sink, router_w, exp_w_gate, exp_w_up, exp_w_down):
    bsz, n, d = x.shape
    lc = ctx.shape[1]
    cap = 2 * n // N_EXPERTS
    cap_c = 2 * lc // N_EXPERTS
    tile_c = min(COMBINE_TILE, lc)

    cc = jnp.zeros((8, d), F32).at[:bsz].set(c).at[bsz].set(c_ctx)
    mods = _ada(cc, w_ada, b_ada)

    cos, sin = _rope_tables(n)
    cos_c = jnp.ones((lc, LANES), F32)
    sin_c = jnp.zeros((lc, LANES), F32)
    blk = jnp.arange(256) // HEAD_DIM
    bavg = ((blk[:, None] == blk[None, :]).astype(F32) / HEAD_DIM).astype(BF16)

    def layer_mods(i):
        m = [mods[i, :, k * d:(k + 1) * d] for k in range(6)]
        lat = [v[:bsz, None, :] for v in m]
        con = [jnp.broadcast_to(v[bsz][None, None, :], (bsz, 1, d)) for v in m]
        return lat, con

    def qkv_params(i):
        j = i // 2
        nq = N_Q_HEADS * HEAD_DIM
        nk = N_KV_HEADS * HEAD_DIM
        wqkv = attn_w_qkv[j]
        wperm = jnp.concatenate([_deinterleave(wqkv[:, :nq], N_Q_HEADS),
                                 _deinterleave(wqkv[:, nq:nq + nk], N_KV_HEADS),
                                 wqkv[:, nq + nk:]], axis=1).astype(BF16)
        qg = jnp.tile(_deinterleave(attn_q_norm[j], 1), 4)[None, :]
        kg = jnp.tile(_deinterleave(attn_k_norm[j], 1), 4)[None, :]
        return wperm, bavg, qg, kg

    for i in range(DEPTH):
        update_ctx = i < DEPTH - 1
        j = i // 2
        is_pool = (i % 2) == 0
        (sh1, sc1, g1, sh2, sc2, g2), (csh1, csc1, cg1, csh2, csc2, cg2) = layer_mods(i)
        n1g = norm1_g[i][None, :]
        n2g = norm2_g[i][None, :]
        wrh, wrl = _split_bf16(router_w[i].T)

        if is_pool:
            pw = pool_w[j].astype(BF16)
            ps = pool_scale[j][None, :]
            x1, h2, aff = _pool_layer(x, n1g, sh1, sc1, g1, pw, ps, n2g, sh2, sc2, wrh, wrl, ROW_TILE)
            if update_ctx:
                c1, hc2, affc = _pool_layer(ctx, n1g, csh1, csc1, cg1, pw, ps, n2g, csh2, csc2, wrh, wrl, lc)
        else:
            wperm, _, qg, kg = qkv_params(i)
            wo = attn_w_o[j].astype(BF16)
            sink = attn_sink[j]
            bound = 1.01 * 8.0 * jnp.max(jnp.abs(attn_q_norm[j])) * jnp.max(jnp.abs(attn_k_norm[j]))
            q, k, v = _qkv_layer(x, n1g, sh1, sc1, wperm, bavg, qg, kg, cos, sin, ROW_TILE)
            qc, kc, vc = _qkv_layer(ctx, n1g, csh1, csc1, wperm, bavg, qg, kg, cos_c, sin_c, lc)
            o = _attn_layer(sink, bound, q, k, v, kc, vc, True)
            x1, h2, aff = _oproj_layer(o, x, wo, g1, n2g, sh2, sc2, wrh, wrl, ROW_TILE)
            if update_ctx:
                oc = _attn_layer(sink, bound, qc, None, None, kc, vc, False)
                c1, hc2, affc = _oproj_layer(oc, ctx, wo, cg1, n2g, csh2, csc2, wrh, wrl, lc)

        slot, tg, tc = _route_select(aff, cap, GATHER_TILE, COMBINE_TILE)
        xs, vals = _gather(tg["offs"], tg["small"], h2, slot, aff, cap, GATHER_TILE)
        if update_ctx:
            slot_c, cg, cc_ = _route_select(affc, cap_c, tile_c, tile_c)
            xs_c, vals_c = _gather(cg["offs"], cg["small"], hc2, slot_c, affc, cap_c, tile_c)
            y = _mlp(i, xs, vals, xs_c, vals_c, exp_w_gate, exp_w_up, exp_w_down)
            ctx = _combine(cc_["offs"], cc_["overflow"], cc_["small"], c1, cg2, slot_c, y, cap_c, tile_c, cap)
        else:
            y = _mlp(i, xs, vals, None, None, exp_w_gate, exp_w_up, exp_w_down)
        x = _combine(tc["offs"], tc["overflow"], tc["small"], x1, g2, slot, y, cap, COMBINE_TILE, 0)
    return x
```

```python
import functools

import jax
import jax.numpy as jnp
from jax import lax
from jax.experimental import pallas as pl
from jax.experimental.pallas import tpu as pltpu

F32 = jnp.float32
BF16 = jnp.bfloat16
I32 = jnp.int32

DEPTH = 4
GRID_W = 64
POOL_WINDOWS = (2, 4, 8, 16)
POOL_GROUP_DIM = 256
HEAD_DIM = 64
N_Q_HEADS = 16
N_KV_HEADS = 4
Q_PER_KV = 4
WINDOW = 128
BLOCK_Q = 128
ROPE_THETA = 10000.0
N_EXPERTS = 16
NORM_EPS = 1e-6
NEG_INF = -1e30

LANES = 128
BF16_ROWS = 16
BF16_ROWS_LOG2 = 4
MXU_DIM = 256
VMEM_LIMIT = 56 * 1024 * 1024

HALO = 8
ROW_TILE = 1024
ADA_COLS = 1536
HEAD_GROUP = MXU_DIM
ROW_SPLIT = 4
MAX_SHIFT = 40.0
SOFTMAX_ROWS = 32
GATHER_ROWS = 1024
GATHER_EXPERTS = 8
COMBINE_GROUP = 4
UP_CHUNK = 1024
UP_ROW_SPLIT = 2
DOWN_CHUNK = 1024
SLOT_DIGIT_BITS = 6
GATHER_TILE, GATHER_SMALL = 256, 64
COMBINE_TILE, COMBINE_SMALL = 256, 64

NT_DIMS = (((1,), (1,)), ((), ()))


def _cparams(sem):
    return pltpu.CompilerParams(dimension_semantics=sem, vmem_limit_bytes=VMEM_LIMIT)


def _row_pieces(rows):
    n = ROW_SPLIT if rows % (ROW_SPLIT * LANES) == 0 else 1
    return [slice(r * rows // n, (r + 1) * rows // n) for r in range(n)]


def _norm_mod(x, g, shift, scale):
    ms = jnp.mean(x * x, axis=-1, keepdims=True)
    return (x * lax.rsqrt(ms + NORM_EPS)) * g * (1.0 + scale) + shift


def _ada_kernel(cc_ref, w_ref, b_ref, o_ref):
    cc = cc_ref[...]
    s = cc * jax.nn.sigmoid(cc)
    o_ref[0] = jnp.dot(s, w_ref[0], preferred_element_type=F32,
                       precision=lax.Precision.HIGHEST) + b_ref[0]


def _ada(cc, w_ada, b_ada):
    depth, d, six_d = w_ada.shape
    tn = ADA_COLS
    return pl.pallas_call(
        _ada_kernel,
        grid=(depth, six_d // tn),
        in_specs=[
            pl.BlockSpec((8, d), lambda i, j: (0, 0)),
            pl.BlockSpec((1, d, tn), lambda i, j: (i, 0, j)),
            pl.BlockSpec((1, 1, tn), lambda i, j: (i, 0, j)),
        ],
        out_specs=pl.BlockSpec((1, 8, tn), lambda i, j: (i, 0, j)),
        out_shape=jax.ShapeDtypeStruct((depth, 8, six_d), F32),
        compiler_params=_cparams(("arbitrary", "arbitrary")),
        name="ada",
    )(cc, w_ada, b_ada.reshape(depth, 1, six_d))


def _route(x1, n2g, sh2, sc2, wr_hi, wr_lo):
    h2 = _norm_mod(x1, n2g, sh2, sc2)
    h_hi = h2.astype(BF16)
    h_lo = (h2 - h_hi.astype(F32)).astype(BF16)
    lg = (lax.dot_general(wr_hi, h_hi, NT_DIMS, preferred_element_type=F32)
          + lax.dot_general(wr_hi, h_lo, NT_DIMS, preferred_element_type=F32)
          + lax.dot_general(wr_lo, h_hi, NT_DIMS, preferred_element_type=F32))
    m = jnp.max(lg, axis=0, keepdims=True)
    ex = jnp.exp(lg - m)
    aff = ex / jnp.sum(ex, axis=0, keepdims=True)
    return h_hi, aff


def _pool_kernel(x_ref, xp_ref, xn_ref, n1g_ref, sh1_ref, sc1_ref, g1_ref, pw_ref, ps_ref,
                 n2g_ref, sh2_ref, sc2_ref, wrh_ref, wrl_ref,
                 x1_ref, h2_ref, aff_ref, hs_ref, p1_ref, p2_ref, p3_ref, *, n_tokens):
    t = pl.program_id(1)
    nt = pl.num_programs(1)
    tt = x_ref.shape[1]
    gd = POOL_GROUP_DIM
    n1g, sh1, sc1 = n1g_ref[...], sh1_ref[0], sc1_ref[0]
    x = x_ref[0]
    h = _norm_mod(x, n1g, sh1, sc1)
    hp = _norm_mod(xp_ref[0], n1g, sh1, sc1) * (t > 0).astype(F32)
    hn = _norm_mod(xn_ref[0], n1g, sh1, sc1) * (t < nt - 1).astype(F32)
    ext = tt + 2 * HALO
    hs_ref[0:HALO, :] = hp
    hs_ref[HALO:HALO + tt, :] = h
    hs_ref[HALO + tt:ext, :] = hn
    for ref in (hs_ref, p1_ref, p2_ref):
        ref[ext:, :] = jnp.zeros((ref.shape[0] - ext, ref.shape[1]), F32)
    p1_ref[0:ext, :] = hs_ref[0:ext, gd:] + hs_ref[1:ext + 1, gd:]
    p2_ref[0:ext, :] = p1_ref[0:ext, gd:] + p1_ref[2:ext + 2, gd:]
    p3_ref[...] = p2_ref[0:ext, gd:] + p2_ref[4:ext + 4, gd:]
    for rs in _row_pieces(tt):
        r0, nr = rs.start, rs.stop - rs.start

        def rows(ref, shift, cols=slice(0, gd)):
            return ref[r0 + shift:r0 + shift + nr, cols]

        sums = [
            rows(hs_ref, HALO - 1) + rows(hs_ref, HALO),
            rows(p1_ref, HALO - 2) + rows(p1_ref, HALO),
            rows(p2_ref, HALO - 4) + rows(p2_ref, HALO),
            rows(p3_ref, 0) + rows(p3_ref, HALO),
        ]
        pos = t * tt + r0 + lax.broadcasted_iota(I32, (nr, 1), 0)
        ys = []
        for g, w in enumerate(POOL_WINDOWS):
            half = w // 2
            c0 = g * POOL_GROUP_DIM
            lo = jnp.clip(pos - half, 0, n_tokens)
            hi = jnp.clip(pos + half, 0, n_tokens)
            p = sums[g] / (hi - lo).astype(F32) - rows(hs_ref, HALO, slice(c0, c0 + gd))
            ys.append(jnp.dot(p.astype(BF16), pw_ref[g], preferred_element_type=F32))
        y = jnp.concatenate(ys, axis=1) * ps_ref[...]
        x1 = x_ref[0, rs, :] + g1_ref[0] * y
        x1_ref[0, rs, :] = x1
        h2, aff = _route(x1, n2g_ref[...], sh2_ref[0], sc2_ref[0], wrh_ref[...], wrl_ref[...])
        h2_ref[0, rs, :] = h2
        aff_ref[0, :, rs] = aff


def _row_spec(d):
    return pl.BlockSpec((1, d), lambda b, t: (0, 0))


def _mod_spec(d):
    return pl.BlockSpec((1, 1, d), lambda b, t: (b, 0, 0))


def _pool_layer(x, n1g, sh1, sc1, g1, pw, ps, n2g, sh2, sc2, wrh, wrl, tt):
    bsz, n, d = x.shape
    assert POOL_WINDOWS == (2, 4, 8, 16) and d == 4 * POOL_GROUP_DIM
    nb8 = n // HALO
    r = tt // HALO
    kern = functools.partial(_pool_kernel, n_tokens=n)
    return pl.pallas_call(
        kern,
        grid=(bsz, n // tt),
        in_specs=[
            pl.BlockSpec((1, tt, d), lambda b, t: (b, t, 0)),
            pl.BlockSpec((1, HALO, d), lambda b, t: (b, jnp.maximum(t * r - 1, 0), 0)),
            pl.BlockSpec((1, HALO, d), lambda b, t: (b, jnp.minimum((t + 1) * r, nb8 - 1), 0)),
            _row_spec(d), _mod_spec(d), _mod_spec(d), _mod_spec(d),
            pl.BlockSpec((4, POOL_GROUP_DIM, POOL_GROUP_DIM), lambda b, t: (0, 0, 0)),
            _row_spec(d),
            _row_spec(d), _mod_spec(d), _mod_spec(d),
            pl.BlockSpec((N_EXPERTS, d), lambda b, t: (0, 0)),
            pl.BlockSpec((N_EXPERTS, d), lambda b, t: (0, 0)),
        ],
        out_specs=[
            pl.BlockSpec((1, tt, d), lambda b, t: (b, t, 0)),
            pl.BlockSpec((1, tt, d), lambda b, t: (b, t, 0)),
            pl.BlockSpec((1, N_EXPERTS, tt), lambda b, t: (b, 0, t)),
        ],
        out_shape=[
            jax.ShapeDtypeStruct((bsz, n, d), F32),
            jax.ShapeDtypeStruct((bsz, n, d), BF16),
            jax.ShapeDtypeStruct((bsz, N_EXPERTS, n), F32),
        ],
        scratch_shapes=[pltpu.VMEM((tt + 3 * HALO, d), F32),
                        pltpu.VMEM((tt + 3 * HALO, d - POOL_GROUP_DIM), F32),
                        pltpu.VMEM((tt + 3 * HALO, d - 2 * POOL_GROUP_DIM), F32),
                        pltpu.VMEM((tt + 2 * HALO, d - 3 * POOL_GROUP_DIM), F32)],
        compiler_params=_cparams(("arbitrary", "arbitrary")),
        name="pool_route",
    )(x, x, x, n1g, sh1, sc1, g1, pw, ps, n2g, sh2, sc2, wrh, wrl)


def _head_rms(v, bavg, gain):
    sq = v * v
    sq_hi = sq.astype(BF16)
    sq_lo = (sq - sq_hi.astype(F32)).astype(BF16)
    ms = (jnp.dot(sq_hi, bavg, preferred_element_type=F32)
          + jnp.dot(sq_lo, bavg, preferred_element_type=F32))
    return v * lax.rsqrt(ms + NORM_EPS) * gain


def _rope(v, cos, sin_signed):
    lane = lax.broadcasted_iota(I32, v.shape, 1)
    fwd = pltpu.roll(v, 32, 1)
    bwd = pltpu.roll(v, 96, 1)
    partner = jnp.where((lane % HEAD_DIM) < HEAD_DIM // 2, bwd, fwd)
    return v * cos + partner * sin_signed


def _qkv_kernel(x_ref, n1g_ref, sh1_ref, sc1_ref, w_ref, bavg_ref, qg_ref, kg_ref,
                cos_ref, sin_ref, q_ref, k_ref, v_ref):
    for rs in _row_pieces(x_ref.shape[1]):
        _qkv_rows(rs, x_ref, n1g_ref, sh1_ref, sc1_ref, w_ref, bavg_ref, qg_ref, kg_ref,
                  cos_ref, sin_ref, q_ref, k_ref, v_ref)


def _qkv_rows(rs, x_ref, n1g_ref, sh1_ref, sc1_ref, w_ref, bavg_ref, qg_ref, kg_ref,
              cos_ref, sin_ref, q_ref, k_ref, v_ref):
    h = _norm_mod(x_ref[0, rs, :], n1g_ref[...], sh1_ref[0], sc1_ref[0]).astype(BF16)
    qkv = jnp.dot(h, w_ref[...], preferred_element_type=F32)
    nq = N_Q_HEADS * HEAD_DIM
    nk = N_KV_HEADS * HEAD_DIM
    bavg = bavg_ref[...]
    cos, sin = cos_ref[rs, :], sin_ref[rs, :]
    scale = HEAD_DIM ** -0.5
    for c in range(nq // HEAD_GROUP):
        qn = _head_rms(qkv[:, c * HEAD_GROUP:(c + 1) * HEAD_GROUP], bavg, qg_ref[...])
        for s in range(2):
            qr = _rope(qn[:, s * LANES:(s + 1) * LANES], cos, sin) * scale
            for hh in range(2):
                head = c * 4 + s * 2 + hh
                q_ref[0, head, rs, :] = qr[:, hh * HEAD_DIM:(hh + 1) * HEAD_DIM].astype(BF16)
    kn = _head_rms(qkv[:, nq:nq + nk], bavg, kg_ref[...])
    v = qkv[:, nq + nk:]
    for s in range(2):
        kr = _rope(kn[:, s * LANES:(s + 1) * LANES], cos, sin)
        for hh in range(2):
            head = s * 2 + hh
            k_ref[0, head, rs, :] = kr[:, hh * HEAD_DIM:(hh + 1) * HEAD_DIM].astype(BF16)
            v_ref[0, head, rs, :] = v[:, head * HEAD_DIM:(head + 1) * HEAD_DIM].astype(BF16)


def _qkv_layer(x, n1g, sh1, sc1, w, bavg, qg, kg, cos, sin, tt):
    bsz, n, d = x.shape
    return pl.pallas_call(
        _qkv_kernel,
        grid=(bsz, n // tt),
        in_specs=[
            pl.BlockSpec((1, tt, d), lambda b, t: (b, t, 0)),
            _row_spec(d), _mod_spec(d), _mod_spec(d),
            pl.BlockSpec(w.shape, lambda b, t: (0, 0)),
            pl.BlockSpec((HEAD_GROUP, HEAD_GROUP), lambda b, t: (0, 0)),
            _row_spec(HEAD_GROUP), _row_spec(HEAD_GROUP),
            pl.BlockSpec((tt, LANES), lambda b, t: (t, 0)),
            pl.BlockSpec((tt, LANES), lambda b, t: (t, 0)),
        ],
        out_specs=[
            pl.BlockSpec((1, N_Q_HEADS, tt, HEAD_DIM), lambda b, t: (b, 0, t, 0)),
            pl.BlockSpec((1, N_KV_HEADS, tt, HEAD_DIM), lambda b, t: (b, 0, t, 0)),
            pl.BlockSpec((1, N_KV_HEADS, tt, HEAD_DIM), lambda b, t: (b, 0, t, 0)),
        ],
        out_shape=[
            jax.ShapeDtypeStruct((bsz, N_Q_HEADS, n, HEAD_DIM), BF16),
            jax.ShapeDtypeStruct((bsz, N_KV_HEADS, n, HEAD_DIM), BF16),
            jax.ShapeDtypeStruct((bsz, N_KV_HEADS, n, HEAD_DIM), BF16),
        ],
        compiler_params=_cparams(("arbitrary", "arbitrary")),
        name="qkv",
    )(x, n1g, sh1, sc1, w, bavg, qg, kg, cos, sin)


def _attn_kernel(sink_ref, q_ref, *refs, band, bounded):
    if band:
        kp_ref, kc_ref, kn_ref, vp_ref, vc_ref, vn_ref, kx_ref, vx_ref, bias_ref, o_ref = refs[:10]
    else:
        kx_ref, vx_ref, o_ref = refs[:3]
    if not bounded:
        s_all, p_all = refs[-2:]
    tq = q_ref.shape[2]
    n_chunks = Q_PER_KV * tq // SOFTMAX_ROWS
    for g in range(N_KV_HEADS):
        if band:
            kg = jnp.concatenate([kp_ref[0, g], kc_ref[0, g], kn_ref[0, g], kx_ref[0, g]], axis=0)
            vg = jnp.concatenate([vp_ref[0, g], vc_ref[0, g], vn_ref[0, g], vx_ref[0, g]], axis=0)
        else:
            kg = kx_ref[0, g]
            vg = vx_ref[0, g]
        vext = jnp.concatenate([vg, jnp.ones_like(vg)], axis=1)
        qg = q_ref[0, g * Q_PER_KV:(g + 1) * Q_PER_KV].reshape(Q_PER_KV * tq, HEAD_DIM)
        if bounded:
            shift = sink_ref[N_Q_HEADS]
            s = lax.dot_general(qg, kg, NT_DIMS, preferred_element_type=F32)
            if band:
                s = s + bias_ref[0]
            p = jnp.exp(s - shift).astype(BF16)
            sink_term = jnp.exp(jnp.concatenate(
                [jnp.full((tq, 1), sink_ref[g * Q_PER_KV + hh] - shift, F32) for hh in range(Q_PER_KV)], axis=0))
        else:
            s_ref, p_ref = s_all.at[g], p_all.at[g]
            s_ref[...] = lax.dot_general(qg, kg, NT_DIMS, preferred_element_type=F32)
            sink_terms = []
            for r in range(n_chunks):
                rs = slice(r * SOFTMAX_ROWS, (r + 1) * SOFTMAX_ROWS)
                sc = s_ref[rs, :]
                if band:
                    sc = sc + bias_ref[0, rs, :]
                sk = sink_ref[g * Q_PER_KV + (r * SOFTMAX_ROWS) // tq]
                m = jnp.maximum(jnp.max(sc, axis=-1, keepdims=True), sk)
                p_ref[rs, :] = jnp.exp(sc - m).astype(BF16)
                sink_terms.append(jnp.exp(sk - m))
            p = p_ref[...]
            sink_term = jnp.concatenate(sink_terms, axis=0)
        oe = jnp.dot(p, vext, preferred_element_type=F32)
        den = oe[:, HEAD_DIM:HEAD_DIM + 1] + sink_term
        o = (oe[:, :HEAD_DIM] / den).astype(o_ref.dtype)
        for hh in range(Q_PER_KV):
            head = g * Q_PER_KV + hh
            o_ref[0, :, head * HEAD_DIM:(head + 1) * HEAD_DIM] = o[hh * tq:(hh + 1) * tq]


def _band_bias(lc):
    nband = BLOCK_Q + 2 * WINDOW
    row = jnp.arange(Q_PER_KV * BLOCK_Q)[:, None] % BLOCK_Q
    col = jnp.arange(nband + lc)[None, :]
    dlt = col - row
    inband = (dlt >= 0) & (dlt <= 2 * WINDOW)
    variants = [inband & (col >= WINDOW), inband, inband & (col < WINDOW + BLOCK_Q)]
    return jnp.stack([jnp.where(v | (col >= nband), 0.0, NEG_INF) for v in variants]).astype(F32)


def _attn_layer(sink, score_bound, q, k, v, kx, vx, band):
    sink_ext = jnp.concatenate([sink.astype(F32), score_bound.reshape(1).astype(F32)])
    if not band:
        k, v = kx, vx
    return lax.cond(score_bound <= MAX_SHIFT,
                    functools.partial(_attn_call, band=band, bounded=True),
                    functools.partial(_attn_call, band=band, bounded=False),
                    sink_ext, q, k, v, kx, vx)


def _attn_call(sink, q, k, v, kx, vx, *, band, bounded):
    bsz, _, n, _ = q.shape
    lc = kx.shape[2]
    tq = BLOCK_Q if band else n
    nb = n // tq
    n_keys = (BLOCK_Q + 2 * WINDOW + lc) if band else lc
    qspec = pl.BlockSpec((1, N_Q_HEADS, tq, HEAD_DIM), lambda b, t: (b, 0, t, 0))
    xspec = pl.BlockSpec((1, N_KV_HEADS, lc, HEAD_DIM), lambda b, t: (b, 0, 0, 0))
    if band:
        prev = pl.BlockSpec((1, N_KV_HEADS, tq, HEAD_DIM), lambda b, t: (b, 0, jnp.maximum(t - 1, 0), 0))
        cur = pl.BlockSpec((1, N_KV_HEADS, tq, HEAD_DIM), lambda b, t: (b, 0, t, 0))
        nxt = pl.BlockSpec((1, N_KV_HEADS, tq, HEAD_DIM), lambda b, t: (b, 0, jnp.minimum(t + 1, nb - 1), 0))
        bias = _band_bias(lc)
        bspec = pl.BlockSpec((1,) + bias.shape[1:],
                             lambda b, t: (jnp.where(t == 0, 0, jnp.where(t == nb - 1, 2, 1)), 0, 0))
        in_specs = [qspec, prev, cur, nxt, prev, cur, nxt, xspec, xspec, bspec]
        args = (q, k, k, k, v, v, v, kx, vx, bias)
    else:
        in_specs = [qspec, xspec, xspec]
        args = (q, kx, vx)
    scratch = [] if bounded else [pltpu.VMEM((N_KV_HEADS, Q_PER_KV * tq, n_keys), F32),
                                  pltpu.VMEM((N_KV_HEADS, Q_PER_KV * tq, n_keys), BF16)]
    return pl.pallas_call(
        functools.partial(_attn_kernel, band=band, bounded=bounded),
        grid=(bsz, nb),
        in_specs=[pl.BlockSpec(memory_space=pltpu.SMEM)] + in_specs,
        out_specs=pl.BlockSpec((1, tq, N_Q_HEADS * HEAD_DIM), lambda b, t: (b, t, 0)),
        out_shape=jax.ShapeDtypeStruct((bsz, n, N_Q_HEADS * HEAD_DIM), BF16),
        scratch_shapes=scratch,
        compiler_params=_cparams(("arbitrary", "arbitrary")),
        name=("attn_band" if band else "attn_ctx") + ("_bounded" if bounded else ""),
    )(sink, *args)


def _oproj_kernel(o_ref, x_ref, wo_ref, g1_ref, n2g_ref, sh2_ref, sc2_ref, wrh_ref, wrl_ref,
                  x1_ref, h2_ref, aff_ref):
    y = jnp.dot(o_ref[0], wo_ref[...], preferred_element_type=F32)
    x1 = x_ref[0] + g1_ref[0] * y
    x1_ref[0] = x1
    h2, aff = _route(x1, n2g_ref[...], sh2_ref[0], sc2_ref[0], wrh_ref[...], wrl_ref[...])
    h2_ref[0] = h2
    aff_ref[0] = aff


def _oproj_layer(o, x, wo, g1, n2g, sh2, sc2, wrh, wrl, tt):
    bsz, n, d = x.shape
    return pl.pallas_call(
        _oproj_kernel,
        grid=(bsz, n // tt),
        in_specs=[
            pl.BlockSpec((1, tt, d), lambda b, t: (b, t, 0)),
            pl.BlockSpec((1, tt, d), lambda b, t: (b, t, 0)),
            pl.BlockSpec((d, d), lambda b, t: (0, 0)),
            _mod_spec(d), _row_spec(d), _mod_spec(d), _mod_spec(d),
            pl.BlockSpec((N_EXPERTS, d), lambda b, t: (0, 0)),
            pl.BlockSpec((N_EXPERTS, d), lambda b, t: (0, 0)),
        ],
        out_specs=[
            pl.BlockSpec((1, tt, d), lambda b, t: (b, t, 0)),
            pl.BlockSpec((1, tt, d), lambda b, t: (b, t, 0)),
            pl.BlockSpec((1, N_EXPERTS, tt), lambda b, t: (b, 0, t)),
        ],
        out_shape=[
            jax.ShapeDtypeStruct((bsz, n, d), F32),
            jax.ShapeDtypeStruct((bsz, n, d), BF16),
            jax.ShapeDtypeStruct((bsz, N_EXPERTS, n), F32),
        ],
        compiler_params=_cparams(("arbitrary", "arbitrary")),
        name="oproj_route",
    )(o, x, wo, g1, n2g, sh2, sc2, wrh, wrl)


def _prefix_counts(mask, n_chunks):
    ne = mask.shape[0]
    m2 = mask.reshape(ne * n_chunks, LANES).astype(BF16)
    ri = lax.broadcasted_iota(I32, (LANES, LANES), 0)
    ci = lax.broadcasted_iota(I32, (LANES, LANES), 1)
    upper = (ri <= ci).astype(BF16)
    within = jnp.dot(m2, upper, preferred_element_type=F32).reshape(ne, n_chunks, LANES)
    tot = jnp.dot(m2, jnp.ones((LANES, LANES), BF16), preferred_element_type=F32).reshape(ne, n_chunks, LANES)
    rc = lax.broadcasted_iota(I32, (n_chunks, n_chunks), 0)
    cc = lax.broadcasted_iota(I32, (n_chunks, n_chunks), 1)
    strict_lower = (cc < rc).astype(F32)
    off = jnp.stack([jnp.dot(strict_lower, tot[e], preferred_element_type=F32) for e in range(ne)])
    return within + off, off


def _select_kernel(a_ref, slot_ref, offc_ref, *, cap):
    a = a_ref[0]
    ne, n_chunks, _ = a.shape
    bits = lax.bitcast_convert_type(a, I32)

    def count(pred):
        c = jnp.sum(pred.astype(F32), axis=1, keepdims=True)
        return jnp.sum(c, axis=2, keepdims=True)

    def body(i, thr):
        cand = thr | lax.shift_left(jnp.int32(1), 30 - i)
        return jnp.where(count(bits >= cand) >= cap, cand, thr)

    thr = lax.fori_loop(0, 31, body, jnp.zeros((ne, 1, 1), I32))
    gt = bits > thr
    eq = bits == thr
    need = cap - count(gt)
    eq_rank, _ = _prefix_counts(eq.astype(F32), n_chunks)
    sel = gt | (eq & (eq_rank <= need))
    sel_f = sel.astype(F32)
    incl, off = _prefix_counts(sel_f, n_chunks)
    slot_ref[0] = jnp.where(sel, (incl - 1.0).astype(I32), -1)
    offc_ref[0] = off.astype(I32)


def _select(aff4, cap):
    bsz, ne, n_chunks, cw = aff4.shape
    spec = pl.BlockSpec((1, ne, n_chunks, cw), lambda b: (b, 0, 0, 0))
    return pl.pallas_call(
        functools.partial(_select_kernel, cap=cap),
        grid=(bsz,),
        in_specs=[spec],
        out_specs=[spec, spec],
        out_shape=[jax.ShapeDtypeStruct(aff4.shape, I32), jax.ShapeDtypeStruct(aff4.shape, I32)],
        compiler_params=_cparams(("arbitrary",)),
        name="select",
    )(aff4)


def _window(cap, tile):
    return min(tile + BF16_ROWS, cap)


def _window_start(off, cap, win):
    start = lax.shift_left(lax.shift_right_logical(off, BF16_ROWS_LOG2), BF16_ROWS_LOG2)
    return pl.multiple_of(jnp.minimum(start, cap - win), BF16_ROWS)


def _gather_kernel(offs_ref, small_ref, h2_ref, sl_ref, af_ref, xs_ref, val_ref, *, cap, tile, n_tiles):
    b, eg, s = pl.program_id(0), pl.program_id(1), pl.program_id(2)
    ge = sl_ref.shape[1]
    sub = h2_ref.shape[1] // tile
    full = _window(cap, tile)
    small = min(GATHER_SMALL, full)

    @pl.when(s == 0)
    def _():
        xs_ref[...] = jnp.zeros_like(xs_ref)
        val_ref[...] = jnp.zeros_like(val_ref)

    def tile_rows(u, win):
        ts = slice(u * tile, (u + 1) * tile)
        starts, hits = [], []
        for j in range(ge):
            off = offs_ref[(b * N_EXPERTS + eg * ge + j) * n_tiles + s * sub + u]
            start = _window_start(off, cap, win)
            rows = start + lax.broadcasted_iota(I32, (win, tile), 0)
            starts.append(start)
            hits.append(rows == sl_ref[0, j, :, ts])
        stacked = jnp.concatenate([h.astype(BF16) for h in hits], axis=0)
        res = jnp.dot(stacked, h2_ref[0, ts, :], preferred_element_type=F32)
        for j in range(ge):
            dst = pl.ds(starts[j], win)
            xs_ref[0, j, dst, :] = (xs_ref[0, j, dst, :].astype(F32) + res[j * win:(j + 1) * win]).astype(BF16)
            val_ref[0, j, dst, :] += jnp.sum(jnp.where(hits[j], af_ref[0, j, :, ts], 0.0), axis=1, keepdims=True)

    if small == full:
        for u in range(sub):
            tile_rows(u, full)
        return
    flags = [small_ref[(b * pl.num_programs(1) + eg) * n_tiles + s * sub + u] for u in range(sub)]
    all_fit = functools.reduce(jnp.minimum, flags) > 0

    @pl.when(all_fit)
    def _():
        for u in range(sub):
            tile_rows(u, small)

    @pl.when(jnp.logical_not(all_fit))
    def _():
        for u in range(sub):
            pl.when(flags[u] > 0)(functools.partial(tile_rows, u, small))
            pl.when(flags[u] <= 0)(functools.partial(tile_rows, u, full))


def _gather(offs, small, h2, slot, aff, cap, tile):
    bsz, n, d = h2.shape
    tg = min(n, GATHER_ROWS)
    n_tiles = n // tile
    ge = GATHER_EXPERTS
    kern = functools.partial(_gather_kernel, cap=cap, tile=tile, n_tiles=n_tiles)
    return pl.pallas_call(
        kern,
        grid_spec=pltpu.PrefetchScalarGridSpec(
            num_scalar_prefetch=2,
            grid=(bsz, N_EXPERTS // ge, n // tg),
            in_specs=[
                pl.BlockSpec((1, tg, d), lambda b, e, s, offs, small: (b, s, 0)),
                pl.BlockSpec((1, ge, 1, tg), lambda b, e, s, offs, small: (b, e, 0, s)),
                pl.BlockSpec((1, ge, 1, tg), lambda b, e, s, offs, small: (b, e, 0, s)),
            ],
            out_specs=[
                pl.BlockSpec((1, ge, cap, d), lambda b, e, s, offs, small: (b, e, 0, 0),
                             pipeline_mode=pl.Buffered(1)),
                pl.BlockSpec((1, ge, cap, 1), lambda b, e, s, offs, small: (b, e, 0, 0),
                             pipeline_mode=pl.Buffered(1)),
            ],
        ),
        out_shape=[
            jax.ShapeDtypeStruct((bsz, N_EXPERTS, cap, d), BF16),
            jax.ShapeDtypeStruct((bsz, N_EXPERTS, cap, 1), F32),
        ],
        compiler_params=_cparams(("arbitrary", "arbitrary", "arbitrary")),
        name="gather",
    )(offs, small, h2, slot.reshape(bsz, N_EXPERTS, 1, n), aff.reshape(bsz, N_EXPERTS, 1, n))


def _up_kernel(*refs, has_ctx):
    if has_ctx:
        xl_ref, xc_ref, wg_ref, wu_ref, h_ref, xs_ref = refs
        cl = xl_ref.shape[2]

        @pl.when(pl.program_id(2) == 0)
        def _():
            xs_ref[0:cl, :] = xl_ref[0, 0]
            xs_ref[cl:, :] = xc_ref[0, 0]

        xs_of = lambda rs: xs_ref[rs, :]
    else:
        xl_ref, wg_ref, wu_ref, h_ref = refs
        xs_of = lambda rs: xl_ref[0, 0, rs, :]
    wg = wg_ref[0, 0].astype(BF16)
    wu = wu_ref[0, 0].astype(BF16)
    rows = h_ref.shape[2]
    for r in range(UP_ROW_SPLIT):
        rs = slice(r * rows // UP_ROW_SPLIT, (r + 1) * rows // UP_ROW_SPLIT)
        xs = xs_of(rs)
        a = jnp.dot(xs, wg, preferred_element_type=F32)
        u = jnp.dot(xs, wu, preferred_element_type=F32)
        h_ref[0, 0, rs, :] = (a * jax.nn.sigmoid(a) * u).astype(BF16)


def _down_kernel(*refs, has_ctx):
    if has_ctx:
        h_ref, vl_ref, vc_ref, wd_ref, y_ref = refs
    else:
        h_ref, vl_ref, wd_ref, y_ref = refs
    cl = vl_ref.shape[2]
    wd = wd_ref[0, 0].astype(BF16)
    rows = h_ref.shape[2]
    half = rows // 2
    res = jnp.dot(h_ref[0, 0, 0:half, :], wd, preferred_element_type=F32)
    y_ref[0, 0, 0:half, :] = (res * vl_ref[0, 0, 0:half, :]).astype(BF16)
    val = vl_ref[0, 0, half:, :]
    if has_ctx:
        val = jnp.concatenate([val, vc_ref[0, 0]], axis=0)
    res = jnp.dot(h_ref[0, 0, half:, :], wd, preferred_element_type=F32)
    y_ref[0, 0, half:, :] = (res * val).astype(BF16)


def _mlp(layer, xl, vl, xc, vc, wg, wu, wd):
    bsz, ne, cl, d = xl.shape
    fdim = wg.shape[-1]
    has_ctx = xc is not None
    cc = xc.shape[2] if has_ctx else 0
    rows = cl + cc
    fc = UP_CHUNK
    dn = DOWN_CHUNK

    xspecs = [pl.BlockSpec((1, 1, cl, d), lambda e, b, f: (b, e, 0, 0))]
    if has_ctx:
        xspecs.append(pl.BlockSpec((1, 1, cc, d), lambda e, b, f: (b, e, 0, 0)))
    hid = pl.pallas_call(
        functools.partial(_up_kernel, has_ctx=has_ctx),
        grid=(ne, bsz, fdim // fc),
        in_specs=xspecs + [
            pl.BlockSpec((1, 1, d, fc), lambda e, b, f: (layer, e, 0, f)),
            pl.BlockSpec((1, 1, d, fc), lambda e, b, f: (layer, e, 0, f)),
        ],
        out_specs=pl.BlockSpec((1, 1, rows, fc), lambda e, b, f: (b, e, 0, f)),
        out_shape=jax.ShapeDtypeStruct((bsz, ne, rows, fdim), BF16),
        scratch_shapes=[pltpu.VMEM((rows, d), BF16)] if has_ctx else [],
        compiler_params=_cparams(("arbitrary", "arbitrary", "arbitrary")),
        name="experts_up",
    )(*((xl, xc) if has_ctx else (xl,)), wg, wu)

    vspecs = [pl.BlockSpec((1, 1, cl, 1), lambda e, b, j: (b, e, 0, 0))]
    if has_ctx:
        vspecs.append(pl.BlockSpec((1, 1, cc, 1), lambda e, b, j: (b, e, 0, 0)))
    return pl.pallas_call(
        functools.partial(_down_kernel, has_ctx=has_ctx),
        grid=(ne, bsz, d // dn),
        in_specs=[pl.BlockSpec((1, 1, rows, fdim), lambda e, b, j: (b, e, 0, 0))] + vspecs + [
            pl.BlockSpec((1, 1, fdim, dn), lambda e, b, j: (layer, e, 0, j)),
        ],
        out_specs=pl.BlockSpec((1, 1, rows, dn), lambda e, b, j: (b, e, 0, j)),
        out_shape=jax.ShapeDtypeStruct((bsz, ne, rows, d), BF16),
        compiler_params=_cparams(("arbitrary", "arbitrary", "arbitrary")),
        name="experts_down",
    )(hid, *((vl, vc) if has_ctx else (vl,)), wd)


def _combine_window(cap, tile):
    return min(tile, cap)


def _combine_kernel(offs_ref, ovf_ref, small_ref, x1_ref, g2_ref, sl_ref, y_hbm, o_ref, ywin_ref, yext_ref,
                    sems, semx, *, cap, tile, n_tiles, base):
    b, t = pl.program_id(0), pl.program_id(1)
    n_steps = pl.num_programs(0) * n_tiles
    win = _combine_window(cap, tile)
    small = min(COMBINE_SMALL, win)
    step = b * n_tiles + t
    buf = lax.rem(step, 2)

    def start_of(bb, tt, e, w):
        return _window_start(offs_ref[(bb * N_EXPERTS + e) * n_tiles + tt], cap, w)

    def window_copy(bb, tt, bf, e, w):
        src = pl.multiple_of(base + start_of(bb, tt, e, w), BF16_ROWS)
        return pltpu.make_async_copy(y_hbm.at[bb, e, pl.ds(src, w), :],
                                     ywin_ref.at[bf, pl.ds(e * w, w), :], sems.at[bf, e])

    def by_window(stp, fn):
        if small == win:
            fn(win)
        else:
            fits = small_ref[stp] > 0
            pl.when(fits)(functools.partial(fn, small))
            pl.when(jnp.logical_not(fits))(functools.partial(fn, win))

    def start_windows(bb, tt, bf, w):
        for e in range(N_EXPERTS):
            window_copy(bb, tt, bf, e, w).start()

    @pl.when(step == 0)
    def _():
        by_window(step, functools.partial(start_windows, b, t, buf))

    @pl.when(step < n_steps - 1)
    def _():
        wrap = t == n_tiles - 1
        b_next = jnp.where(wrap, b + 1, b)
        t_next = jnp.where(wrap, 0, t + 1)
        by_window(step + 1, functools.partial(start_windows, b_next, t_next, 1 - buf))

    sl = sl_ref[0] + 1
    eye = (lax.broadcasted_iota(I32, (tile, tile), 0)
           == lax.broadcasted_iota(I32, (tile, tile), 1)).astype(BF16)
    d_hi = lax.shift_right_logical(sl, SLOT_DIGIT_BITS).astype(F32).astype(BF16)
    d_lo = (sl & ((1 << SLOT_DIGIT_BITS) - 1)).astype(F32).astype(BF16)
    slot_t = (lax.dot_general(eye, d_hi, NT_DIMS, preferred_element_type=F32) * float(1 << SLOT_DIGIT_BITS)
              + lax.dot_general(eye, d_lo, NT_DIMS, preferred_element_type=F32)).astype(I32) - 1

    def accumulate(w):
        lane = lax.broadcasted_iota(I32, (tile, w), 1)
        hits = [(lane == slot_t[:, e:e + 1] - start_of(b, t, e, w)).astype(BF16) for e in range(N_EXPERTS)]
        for e in range(N_EXPERTS):
            window_copy(b, t, buf, e, w).wait()
        acc = None
        for g in range(0, N_EXPERTS, COMBINE_GROUP):
            part = jnp.dot(jnp.concatenate(hits[g:g + COMBINE_GROUP], axis=1),
                           ywin_ref[buf, g * w:(g + COMBINE_GROUP) * w, :], preferred_element_type=F32)
            acc = part if acc is None else acc + part
        o_ref[0] = x1_ref[0] + g2_ref[0] * acc

    by_window(step, accumulate)

    if win < cap:
        @pl.when(ovf_ref[step] > 0)
        def _():
            def ext_start(e):
                s0 = start_of(b, t, e, win)
                return s0, pl.multiple_of(jnp.minimum(s0 + win, cap - BF16_ROWS), BF16_ROWS)

            def ext_copy(e):
                src = pl.multiple_of(base + ext_start(e)[1], BF16_ROWS)
                return pltpu.make_async_copy(y_hbm.at[b, e, pl.ds(src, BF16_ROWS), :],
                                             yext_ref.at[pl.ds(e * BF16_ROWS, BF16_ROWS), :], semx.at[e])

            for e in range(N_EXPERTS):
                ext_copy(e).start()
            lane_x = lax.broadcasted_iota(I32, (tile, BF16_ROWS), 1)
            hits_x = []
            for e in range(N_EXPERTS):
                ext_copy(e).wait()
                s0, s1 = ext_start(e)
                row = s1 + lane_x
                hits_x.append(((row == slot_t[:, e:e + 1]) & (row >= s0 + win)).astype(BF16))
            extra = jnp.dot(jnp.concatenate(hits_x, axis=1), yext_ref[...], preferred_element_type=F32)
            o_ref[0] += g2_ref[0] * extra


def _combine(offs, ovf, small, x1, g2, slot, y, cap, tile, base):
    bsz, n, d = x1.shape
    n_tiles = n // tile
    win = _combine_window(cap, tile)
    kern = functools.partial(_combine_kernel, cap=cap, tile=tile, n_tiles=n_tiles, base=base)
    return pl.pallas_call(
        kern,
        grid_spec=pltpu.PrefetchScalarGridSpec(
            num_scalar_prefetch=3,
            grid=(bsz, n_tiles),
            in_specs=[
                pl.BlockSpec((1, tile, d), lambda b, t, *_: (b, t, 0)),
                pl.BlockSpec((1, 1, d), lambda b, t, *_: (b, 0, 0)),
                pl.BlockSpec((1, N_EXPERTS, tile), lambda b, t, *_: (b, 0, t)),
                pl.BlockSpec(memory_space=pl.ANY),
            ],
            out_specs=pl.BlockSpec((1, tile, d), lambda b, t, *_: (b, t, 0)),
            scratch_shapes=[pltpu.VMEM((2, N_EXPERTS * win, d), BF16),
                            pltpu.VMEM((N_EXPERTS * BF16_ROWS, d), BF16),
                            pltpu.SemaphoreType.DMA((2, N_EXPERTS)),
                            pltpu.SemaphoreType.DMA((N_EXPERTS,))],
        ),
        out_shape=jax.ShapeDtypeStruct((bsz, n, d), F32),
        compiler_params=_cparams(("arbitrary", "arbitrary")),
        name="combine",
    )(offs, ovf, small, x1, g2, slot, y)


def _deinterleave(w, n_heads):
    lead = w.shape[:-1]
    w = w.reshape(lead + (n_heads, HEAD_DIM // 2, 2))
    return jnp.swapaxes(w, -1, -2).reshape(lead + (n_heads * HEAD_DIM,))


def _rope_tables(n):
    rows = n // GRID_W
    row = jnp.repeat(jnp.arange(rows), GRID_W).astype(F32)
    col = jnp.tile(jnp.arange(GRID_W), rows).astype(F32)
    n_freq = HEAD_DIM // 4
    inv = ROPE_THETA ** (-jnp.arange(n_freq, dtype=F32) / n_freq)
    ang = jnp.concatenate([row[:, None] * inv, col[:, None] * inv], axis=-1)
    cos = jnp.tile(jnp.cos(ang), (1, 4))
    sin = jnp.sin(ang)
    sin_signed = jnp.tile(jnp.concatenate([-sin, sin], axis=-1), (1, 2))
    return cos, sin_signed


def _split_bf16(w):
    hi = w.astype(BF16)
    return hi, (w - hi.astype(F32)).astype(BF16)


def _route_select(aff, cap, gather_tile, combine_tile):
    bsz, ne, n = aff.shape
    n_pad = max(n, HALO * LANES)
    a = aff if n_pad == n else jnp.pad(aff, ((0, 0), (0, 0), (0, n_pad - n)))
    slot4, offc4 = _select(a.reshape(bsz, ne, n_pad // LANES, LANES), cap)

    def tile_tables(tile):
        n_tiles = n // tile
        offs = offc4[:, :, ::tile // LANES, 0][:, :, :n_tiles]
        ends = jnp.concatenate([offs[:, :, 1:], jnp.full((bsz, ne, 1), cap, I32)], axis=2)

        def fits(win):
            return ends <= jnp.minimum((offs >> BF16_ROWS_LOG2) << BF16_ROWS_LOG2, cap - win) + win

        return offs.reshape(-1), fits

    g_offs, g_fits = tile_tables(gather_tile)
    gather = {
        "offs": g_offs,
        "small": jnp.all(g_fits(min(GATHER_SMALL, _window(cap, gather_tile)))
                         .reshape(bsz, ne // GATHER_EXPERTS, GATHER_EXPERTS, -1), axis=2).astype(I32).reshape(-1),
    }
    c_offs, c_fits = tile_tables(combine_tile)
    win_c = _combine_window(cap, combine_tile)
    combine = {
        "offs": c_offs,
        "overflow": jnp.any(~c_fits(win_c), axis=1).astype(I32).reshape(-1),
        "small": jnp.all(c_fits(min(COMBINE_SMALL, win_c)), axis=1).astype(I32).reshape(-1),
    }
    return slot4.reshape(bsz, ne, n_pad)[:, :, :n], gather, combine


def kernel(x, c, ctx, c_ctx, w_ada, b_ada, norm1_g, norm2_g, pool_w, pool_scale, attn_w_qkv, attn_w_o,
           attn_q_norm, attn_k_norm, attn_sink, router_w, exp_w_gate, exp_w_up, exp_w_down):
    bsz, n, d = x.shape
    lc = ctx.shape[1]
    cap = 2 * n // N_EXPERTS
    cap_c = 2 * lc // N_EXPERTS
    tile_c = min(COMBINE_TILE, lc)

    cc = jnp.zeros((8, d), F32).at[:bsz].set(c).at[bsz].set(c_ctx)
    mods = _ada(cc, w_ada, b_ada)

    cos, sin = _rope_tables(n)
    cos_c = jnp.ones((lc, LANES), F32)
    sin_c = jnp.zeros((lc, LANES), F32)
    blk = jnp.arange(HEAD_GROUP) // HEAD_DIM
    bavg = ((blk[:, None] == blk[None, :]).astype(F32) / HEAD_DIM).astype(BF16)

    def layer_mods(i):
        m = [mods[i, :, k * d:(k + 1) * d] for k in range(6)]
        lat = [v[:bsz, None, :] for v in m]
        con = [jnp.broadcast_to(v[bsz][None, None, :], (bsz, 1, d)) for v in m]
        return lat, con

    def qkv_params(i):
        j = i // 2
        nq = N_Q_HEADS * HEAD_DIM
        nk = N_KV_HEADS * HEAD_DIM
        wqkv = attn_w_qkv[j]
        wperm = jnp.concatenate([_deinterleave(wqkv[:, :nq], N_Q_HEADS),
                                 _deinterleave(wqkv[:, nq:nq + nk], N_KV_HEADS),
                                 wqkv[:, nq + nk:]], axis=1).astype(BF16)
        qg = jnp.tile(_deinterleave(attn_q_norm[j], 1), 4)[None, :]
        kg = jnp.tile(_deinterleave(attn_k_norm[j], 1), 4)[None, :]
        return wperm, bavg, qg, kg

    for i in range(DEPTH):
        update_ctx = i < DEPTH - 1
        j = i // 2
        is_pool = (i % 2) == 0
        (sh1, sc1, g1, sh2, sc2, g2), (csh1, csc1, cg1, csh2, csc2, cg2) = layer_mods(i)
        n1g = norm1_g[i][None, :]
        n2g = norm2_g[i][None, :]
        wrh, wrl = _split_bf16(router_w[i].T)

        if is_pool:
            pw = pool_w[j].astype(BF16)
            ps = pool_scale[j][None, :]
            x1, h2, aff = _pool_layer(x, n1g, sh1, sc1, g1, pw, ps, n2g, sh2, sc2, wrh, wrl, ROW_TILE)
            if update_ctx:
                c1, hc2, affc = _pool_layer(ctx, n1g, csh1, csc1, cg1, pw, ps, n2g, csh2, csc2, wrh, wrl, lc)
        else:
            wperm, _, qg, kg = qkv_params(i)
            wo = attn_w_o[j].astype(BF16)
            sink = attn_sink[j]
            bound = 1.01 * 8.0 * jnp.max(jnp.abs(attn_q_norm[j])) * jnp.max(jnp.abs(attn_k_norm[j]))
            q, k, v = _qkv_layer(x, n1g, sh1, sc1, wperm, bavg, qg, kg, cos, sin, ROW_TILE)
            qc, kc, vc = _qkv_layer(ctx, n1g, csh1, csc1, wperm, bavg, qg, kg, cos_c, sin_c, lc)
            o = _attn_layer(sink, bound, q, k, v, kc, vc, True)
            x1, h2, aff = _oproj_layer(o, x, wo, g1, n2g, sh2, sc2, wrh, wrl, ROW_TILE)
            if update_ctx:
                oc = _attn_layer(sink, bound, qc, None, None, kc, vc, False)
                c1, hc2, affc = _oproj_layer(oc, ctx, wo, cg1, n2g, csh2, csc2, wrh, wrl, lc)

        slot, tg, tc = _route_select(aff, cap, GATHER_TILE, COMBINE_TILE)
        xs, vals = _gather(tg["offs"], tg["small"], h2, slot, aff, cap, GATHER_TILE)
        if update_ctx:
            slot_c, cg, cc_ = _route_select(affc, cap_c, tile_c, tile_c)
            xs_c, vals_c = _gather(cg["offs"], cg["small"], hc2, slot_c, affc, cap_c, tile_c)
            y = _mlp(i, xs, vals, xs_c, vals_c, exp_w_gate, exp_w_up, exp_w_down)
            ctx = _combine(cc_["offs"], cc_["overflow"], cc_["small"], c1, cg2, slot_c, y, cap_c, tile_c, cap)
        else:
            y = _mlp(i, xs, vals, None, None, exp_w_gate, exp_w_up, exp_w_down)
        x = _combine(tc["offs"], tc["overflow"], tc["small"], x1, g2, slot, y, cap, COMBINE_TILE, 0)
    return x
```

```python
import functools

import jax
import jax.numpy as jnp
from jax import lax
from jax.experimental import pallas as pl
from jax.experimental.pallas import tpu as pltpu

F32 = jnp.float32
BF16 = jnp.bfloat16
I32 = jnp.int32

DEPTH = 4
GRID_W = 64
POOL_WINDOWS = (2, 4, 8, 16)
POOL_GROUP_DIM = 256
HEAD_DIM = 64
N_Q_HEADS = 16
N_KV_HEADS = 4
Q_PER_KV = 4
WINDOW = 128
BLOCK_Q = 128
ROPE_THETA = 10000.0
N_EXPERTS = 16
NORM_EPS = 1e-6
NEG_INF = -1e30

LANES = 128
BF16_ROWS = 16
BF16_ROWS_LOG2 = 4
MXU_DIM = 256
VMEM_LIMIT = 56 * 1024 * 1024

HALO = 8
ROW_TILE = 1024
ADA_COLS = 1536
HEAD_GROUP = MXU_DIM
ROW_SPLIT = 4
MAX_SHIFT = 40.0
SOFTMAX_ROWS = 32
GATHER_ROWS = 1024
GATHER_EXPERTS = 8
COMBINE_GROUP = 4
COMBINE_TILES_PER_STEP = 2
UP_CHUNK = 1024
UP_ROW_SPLIT = 2
DOWN_CHUNK = 1024
SLOT_DIGIT_BITS = 6
GATHER_TILE, GATHER_SMALL = 256, 64
COMBINE_TILE, COMBINE_SMALL = 256, 64

NT_DIMS = (((1,), (1,)), ((), ()))


def _cparams(sem):
    return pltpu.CompilerParams(dimension_semantics=sem, vmem_limit_bytes=VMEM_LIMIT)


def _row_pieces(rows):
    n = ROW_SPLIT if rows % (ROW_SPLIT * LANES) == 0 else 1
    return [slice(r * rows // n, (r + 1) * rows // n) for r in range(n)]


def _norm_mod(x, g, shift, scale):
    ms = jnp.mean(x * x, axis=-1, keepdims=True)
    return (x * lax.rsqrt(ms + NORM_EPS)) * g * (1.0 + scale) + shift


def _ada_kernel(cc_ref, w_ref, b_ref, o_ref):
    cc = cc_ref[...]
    s = cc * jax.nn.sigmoid(cc)
    o_ref[0] = jnp.dot(s, w_ref[0], preferred_element_type=F32,
                       precision=lax.Precision.HIGHEST) + b_ref[0]


def _ada(cc, w_ada, b_ada):
    depth, d, six_d = w_ada.shape
    tn = ADA_COLS
    return pl.pallas_call(
        _ada_kernel,
        grid=(depth, six_d // tn),
        in_specs=[
            pl.BlockSpec((8, d), lambda i, j: (0, 0)),
            pl.BlockSpec((1, d, tn), lambda i, j: (i, 0, j)),
            pl.BlockSpec((1, 1, tn), lambda i, j: (i, 0, j)),
        ],
        out_specs=pl.BlockSpec((1, 8, tn), lambda i, j: (i, 0, j)),
        out_shape=jax.ShapeDtypeStruct((depth, 8, six_d), F32),
        compiler_params=_cparams(("arbitrary", "arbitrary")),
        name="ada",
    )(cc, w_ada, b_ada.reshape(depth, 1, six_d))


def _route(x1, n2g, sh2, sc2, wr_hi, wr_lo):
    h2 = _norm_mod(x1, n2g, sh2, sc2)
    h_hi = h2.astype(BF16)
    h_lo = (h2 - h_hi.astype(F32)).astype(BF16)
    lg = (lax.dot_general(wr_hi, h_hi, NT_DIMS, preferred_element_type=F32)
          + lax.dot_general(wr_hi, h_lo, NT_DIMS, preferred_element_type=F32)
          + lax.dot_general(wr_lo, h_hi, NT_DIMS, preferred_element_type=F32))
    m = jnp.max(lg, axis=0, keepdims=True)
    ex = jnp.exp(lg - m)
    aff = ex / jnp.sum(ex, axis=0, keepdims=True)
    return h_hi, aff


def _pool_kernel(x_ref, xp_ref, xn_ref, n1g_ref, sh1_ref, sc1_ref, g1_ref, pw_ref, ps_ref,
                 n2g_ref, sh2_ref, sc2_ref, wrh_ref, wrl_ref,
                 x1_ref, h2_ref, aff_ref, hs_ref, p1_ref, p2_ref, p3_ref, *, n_tokens):
    t = pl.program_id(1)
    nt = pl.num_programs(1)
    tt = x_ref.shape[1]
    gd = POOL_GROUP_DIM
    n1g, sh1, sc1 = n1g_ref[...], sh1_ref[0], sc1_ref[0]
    x = x_ref[0]
    h = _norm_mod(x, n1g, sh1, sc1)
    hp = _norm_mod(xp_ref[0], n1g, sh1, sc1) * (t > 0).astype(F32)
    hn = _norm_mod(xn_ref[0], n1g, sh1, sc1) * (t < nt - 1).astype(F32)
    ext = tt + 2 * HALO
    hs_ref[0:HALO, :] = hp
    hs_ref[HALO:HALO + tt, :] = h
    hs_ref[HALO + tt:ext, :] = hn
    for ref in (hs_ref, p1_ref, p2_ref):
        ref[ext:, :] = jnp.zeros((ref.shape[0] - ext, ref.shape[1]), F32)
    p1_ref[0:ext, :] = hs_ref[0:ext, gd:] + hs_ref[1:ext + 1, gd:]
    p2_ref[0:ext, :] = p1_ref[0:ext, gd:] + p1_ref[2:ext + 2, gd:]
    p3_ref[...] = p2_ref[0:ext, gd:] + p2_ref[4:ext + 4, gd:]
    for rs in _row_pieces(tt):
        r0, nr = rs.start, rs.stop - rs.start

        def rows(ref, shift, cols=slice(0, gd)):
            return ref[r0 + shift:r0 + shift + nr, cols]

        sums = [
            rows(hs_ref, HALO - 1) + rows(hs_ref, HALO),
            rows(p1_ref, HALO - 2) + rows(p1_ref, HALO),
            rows(p2_ref, HALO - 4) + rows(p2_ref, HALO),
            rows(p3_ref, 0) + rows(p3_ref, HALO),
        ]
        pos = t * tt + r0 + lax.broadcasted_iota(I32, (nr, 1), 0)
        ys = []
        for g, w in enumerate(POOL_WINDOWS):
            half = w // 2
            c0 = g * POOL_GROUP_DIM
            lo = jnp.clip(pos - half, 0, n_tokens)
            hi = jnp.clip(pos + half, 0, n_tokens)
            p = sums[g] / (hi - lo).astype(F32) - rows(hs_ref, HALO, slice(c0, c0 + gd))
            ys.append(jnp.dot(p.astype(BF16), pw_ref[g], preferred_element_type=F32))
        y = jnp.concatenate(ys, axis=1) * ps_ref[...]
        x1 = x_ref[0, rs, :] + g1_ref[0] * y
        x1_ref[0, rs, :] = x1
        h2, aff = _route(x1, n2g_ref[...], sh2_ref[0], sc2_ref[0], wrh_ref[...], wrl_ref[...])
        h2_ref[0, rs, :] = h2
        aff_ref[0, :, rs] = aff


def _row_spec(d):
    return pl.BlockSpec((1, d), lambda b, t: (0, 0))


def _mod_spec(d):
    return pl.BlockSpec((1, 1, d), lambda b, t: (b, 0, 0))


def _pool_layer(x, n1g, sh1, sc1, g1, pw, ps, n2g, sh2, sc2, wrh, wrl, tt):
    bsz, n, d = x.shape
    assert POOL_WINDOWS == (2, 4, 8, 16) and d == 4 * POOL_GROUP_DIM
    nb8 = n // HALO
    r = tt // HALO
    kern = functools.partial(_pool_kernel, n_tokens=n)
    return pl.pallas_call(
        kern,
        grid=(bsz, n // tt),
        in_specs=[
            pl.BlockSpec((1, tt, d), lambda b, t: (b, t, 0)),
            pl.BlockSpec((1, HALO, d), lambda b, t: (b, jnp.maximum(t * r - 1, 0), 0)),
            pl.BlockSpec((1, HALO, d), lambda b, t: (b, jnp.minimum((t + 1) * r, nb8 - 1), 0)),
            _row_spec(d), _mod_spec(d), _mod_spec(d), _mod_spec(d),
            pl.BlockSpec((4, POOL_GROUP_DIM, POOL_GROUP_DIM), lambda b, t: (0, 0, 0)),
            _row_spec(d),
            _row_spec(d), _mod_spec(d), _mod_spec(d),
            pl.BlockSpec((N_EXPERTS, d), lambda b, t: (0, 0)),
            pl.BlockSpec((N_EXPERTS, d), lambda b, t: (0, 0)),
        ],
        out_specs=[
            pl.BlockSpec((1, tt, d), lambda b, t: (b, t, 0)),
            pl.BlockSpec((1, tt, d), lambda b, t: (b, t, 0)),
            pl.BlockSpec((1, N_EXPERTS, tt), lambda b, t: (b, 0, t)),
        ],
        out_shape=[
            jax.ShapeDtypeStruct((bsz, n, d), F32),
            jax.ShapeDtypeStruct((bsz, n, d), BF16),
            jax.ShapeDtypeStruct((bsz, N_EXPERTS, n), F32),
        ],
        scratch_shapes=[pltpu.VMEM((tt + 3 * HALO, d), F32),
                        pltpu.VMEM((tt + 3 * HALO, d - POOL_GROUP_DIM), F32),
                        pltpu.VMEM((tt + 3 * HALO, d - 2 * POOL_GROUP_DIM), F32),
                        pltpu.VMEM((tt + 2 * HALO, d - 3 * POOL_GROUP_DIM), F32)],
        compiler_params=_cparams(("arbitrary", "arbitrary")),
        name="pool_route",
    )(x, x, x, n1g, sh1, sc1, g1, pw, ps, n2g, sh2, sc2, wrh, wrl)


def _head_rms(v, bavg, gain):
    sq = v * v
    sq_hi = sq.astype(BF16)
    sq_lo = (sq - sq_hi.astype(F32)).astype(BF16)
    ms = (jnp.dot(sq_hi, bavg, preferred_element_type=F32)
          + jnp.dot(sq_lo, bavg, preferred_element_type=F32))
    return v * lax.rsqrt(ms + NORM_EPS) * gain


def _rope(v, cos, sin_signed):
    lane = lax.broadcasted_iota(I32, v.shape, 1)
    fwd = pltpu.roll(v, 32, 1)
    bwd = pltpu.roll(v, 96, 1)
    partner = jnp.where((lane % HEAD_DIM) < HEAD_DIM // 2, bwd, fwd)
    return v * cos + partner * sin_signed


def _qkv_kernel(x_ref, n1g_ref, sh1_ref, sc1_ref, w_ref, bavg_ref, qg_ref, kg_ref,
                cos_ref, sin_ref, q_ref, k_ref, v_ref):
    for rs in _row_pieces(x_ref.shape[1]):
        _qkv_rows(rs, x_ref, n1g_ref, sh1_ref, sc1_ref, w_ref, bavg_ref, qg_ref, kg_ref,
                  cos_ref, sin_ref, q_ref, k_ref, v_ref)


def _qkv_rows(rs, x_ref, n1g_ref, sh1_ref, sc1_ref, w_ref, bavg_ref, qg_ref, kg_ref,
              cos_ref, sin_ref, q_ref, k_ref, v_ref):
    h = _norm_mod(x_ref[0, rs, :], n1g_ref[...], sh1_ref[0], sc1_ref[0]).astype(BF16)
    qkv = jnp.dot(h, w_ref[...], preferred_element_type=F32)
    nq = N_Q_HEADS * HEAD_DIM
    nk = N_KV_HEADS * HEAD_DIM
    bavg = bavg_ref[...]
    cos, sin = cos_ref[rs, :], sin_ref[rs, :]
    scale = HEAD_DIM ** -0.5
    for c in range(nq // HEAD_GROUP):
        qn = _head_rms(qkv[:, c * HEAD_GROUP:(c + 1) * HEAD_GROUP], bavg, qg_ref[...])
        for s in range(2):
            qr = _rope(qn[:, s * LANES:(s + 1) * LANES], cos, sin) * scale
            for hh in range(2):
                head = c * 4 + s * 2 + hh
                q_ref[0, head, rs, :] = qr[:, hh * HEAD_DIM:(hh + 1) * HEAD_DIM].astype(BF16)
    kn = _head_rms(qkv[:, nq:nq + nk], bavg, kg_ref[...])
    v = qkv[:, nq + nk:]
    for s in range(2):
        kr = _rope(kn[:, s * LANES:(s + 1) * LANES], cos, sin)
        for hh in range(2):
            head = s * 2 + hh
            k_ref[0, head, rs, :] = kr[:, hh * HEAD_DIM:(hh + 1) * HEAD_DIM].astype(BF16)
            v_ref[0, head, rs, :] = v[:, head * HEAD_DIM:(head + 1) * HEAD_DIM].astype(BF16)


def _qkv_layer(x, n1g, sh1, sc1, w, bavg, qg, kg, cos, sin, tt):
    bsz, n, d = x.shape
    return pl.pallas_call(
        _qkv_kernel,
        grid=(bsz, n // tt),
        in_specs=[
            pl.BlockSpec((1, tt, d), lambda b, t: (b, t, 0)),
            _row_spec(d), _mod_spec(d), _mod_spec(d),
            pl.BlockSpec(w.shape, lambda b, t: (0, 0)),
            pl.BlockSpec((HEAD_GROUP, HEAD_GROUP), lambda b, t: (0, 0)),
            _row_spec(HEAD_GROUP), _row_spec(HEAD_GROUP),
            pl.BlockSpec((tt, LANES), lambda b, t: (t, 0)),
            pl.BlockSpec((tt, LANES), lambda b, t: (t, 0)),
        ],
        out_specs=[
            pl.BlockSpec((1, N_Q_HEADS, tt, HEAD_DIM), lambda b, t: (b, 0, t, 0)),
            pl.BlockSpec((1, N_KV_HEADS, tt, HEAD_DIM), lambda b, t: (b, 0, t, 0)),
            pl.BlockSpec((1, N_KV_HEADS, tt, HEAD_DIM), lambda b, t: (b, 0, t, 0)),
        ],
        out_shape=[
            jax.ShapeDtypeStruct((bsz, N_Q_HEADS, n, HEAD_DIM), BF16),
            jax.ShapeDtypeStruct((bsz, N_KV_HEADS, n, HEAD_DIM), BF16),
            jax.ShapeDtypeStruct((bsz, N_KV_HEADS, n, HEAD_DIM), BF16),
        ],
        compiler_params=_cparams(("arbitrary", "arbitrary")),
        name="qkv",
    )(x, n1g, sh1, sc1, w, bavg, qg, kg, cos, sin)


def _attn_kernel(sink_ref, q_ref, *refs, band, bounded):
    if band:
        kp_ref, kc_ref, kn_ref, vp_ref, vc_ref, vn_ref, kx_ref, vx_ref, bias_ref, o_ref = refs[:10]
    else:
        kx_ref, vx_ref, o_ref = refs[:3]
    if not bounded:
        s_all, p_all = refs[-2:]
    tq = q_ref.shape[2]
    n_chunks = Q_PER_KV * tq // SOFTMAX_ROWS
    for g in range(N_KV_HEADS):
        if band:
            kg = jnp.concatenate([kp_ref[0, g], kc_ref[0, g], kn_ref[0, g], kx_ref[0, g]], axis=0)
            vg = jnp.concatenate([vp_ref[0, g], vc_ref[0, g], vn_ref[0, g], vx_ref[0, g]], axis=0)
        else:
            kg = kx_ref[0, g]
            vg = vx_ref[0, g]
        vext = jnp.concatenate([vg, jnp.ones_like(vg)], axis=1)
        qg = q_ref[0, g * Q_PER_KV:(g + 1) * Q_PER_KV].reshape(Q_PER_KV * tq, HEAD_DIM)
        if bounded:
            shift = sink_ref[N_Q_HEADS]
            s = lax.dot_general(qg, kg, NT_DIMS, preferred_element_type=F32)
            if band:
                s = s + bias_ref[0]
            p = jnp.exp(s - shift).astype(BF16)
            sink_term = jnp.exp(jnp.concatenate(
                [jnp.full((tq, 1), sink_ref[g * Q_PER_KV + hh] - shift, F32) for hh in range(Q_PER_KV)], axis=0))
        else:
            s_ref, p_ref = s_all.at[g], p_all.at[g]
            s_ref[...] = lax.dot_general(qg, kg, NT_DIMS, preferred_element_type=F32)
            sink_terms = []
            for r in range(n_chunks):
                rs = slice(r * SOFTMAX_ROWS, (r + 1) * SOFTMAX_ROWS)
                sc = s_ref[rs, :]
                if band:
                    sc = sc + bias_ref[0, rs, :]
                sk = sink_ref[g * Q_PER_KV + (r * SOFTMAX_ROWS) // tq]
                m = jnp.maximum(jnp.max(sc, axis=-1, keepdims=True), sk)
                p_ref[rs, :] = jnp.exp(sc - m).astype(BF16)
                sink_terms.append(jnp.exp(sk - m))
            p = p_ref[...]
            sink_term = jnp.concatenate(sink_terms, axis=0)
        oe = jnp.dot(p, vext, preferred_element_type=F32)
        den = oe[:, HEAD_DIM:HEAD_DIM + 1] + sink_term
        o = (oe[:, :HEAD_DIM] / den).astype(o_ref.dtype)
        for hh in range(Q_PER_KV):
            head = g * Q_PER_KV + hh
            o_ref[0, :, head * HEAD_DIM:(head + 1) * HEAD_DIM] = o[hh * tq:(hh + 1) * tq]


def _band_bias(lc):
    nband = BLOCK_Q + 2 * WINDOW
    row = jnp.arange(Q_PER_KV * BLOCK_Q)[:, None] % BLOCK_Q
    col = jnp.arange(nband + lc)[None, :]
    dlt = col - row
    inband = (dlt >= 0) & (dlt <= 2 * WINDOW)
    variants = [inband & (col >= WINDOW), inband, inband & (col < WINDOW + BLOCK_Q)]
    return jnp.stack([jnp.where(v | (col >= nband), 0.0, NEG_INF) for v in variants]).astype(F32)


def _attn_layer(sink, score_bound, q, k, v, kx, vx, band):
    sink_ext = jnp.concatenate([sink.astype(F32), score_bound.reshape(1).astype(F32)])
    if not band:
        k, v = kx, vx
    return lax.cond(score_bound <= MAX_SHIFT,
                    functools.partial(_attn_call, band=band, bounded=True),
                    functools.partial(_attn_call, band=band, bounded=False),
                    sink_ext, q, k, v, kx, vx)


def _attn_call(sink, q, k, v, kx, vx, *, band, bounded):
    bsz, _, n, _ = q.shape
    lc = kx.shape[2]
    tq = BLOCK_Q if band else n
    nb = n // tq
    n_keys = (BLOCK_Q + 2 * WINDOW + lc) if band else lc
    qspec = pl.BlockSpec((1, N_Q_HEADS, tq, HEAD_DIM), lambda b, t: (b, 0, t, 0))
    xspec = pl.BlockSpec((1, N_KV_HEADS, lc, HEAD_DIM), lambda b, t: (b, 0, 0, 0))
    if band:
        prev = pl.BlockSpec((1, N_KV_HEADS, tq, HEAD_DIM), lambda b, t: (b, 0, jnp.maximum(t - 1, 0), 0))
        cur = pl.BlockSpec((1, N_KV_HEADS, tq, HEAD_DIM), lambda b, t: (b, 0, t, 0))
        nxt = pl.BlockSpec((1, N_KV_HEADS, tq, HEAD_DIM), lambda b, t: (b, 0, jnp.minimum(t + 1, nb - 1), 0))
        bias = _band_bias(lc)
        bspec = pl.BlockSpec((1,) + bias.shape[1:],
                             lambda b, t: (jnp.where(t == 0, 0, jnp.where(t == nb - 1, 2, 1)), 0, 0))
        in_specs = [qspec, prev, cur, nxt, prev, cur, nxt, xspec, xspec, bspec]
        args = (q, k, k, k, v, v, v, kx, vx, bias)
    else:
        in_specs = [qspec, xspec, xspec]
        args = (q, kx, vx)
    scratch = [] if bounded else [pltpu.VMEM((N_KV_HEADS, Q_PER_KV * tq, n_keys), F32),
                                  pltpu.VMEM((N_KV_HEADS, Q_PER_KV * tq, n_keys), BF16)]
    return pl.pallas_call(
        functools.partial(_attn_kernel, band=band, bounded=bounded),
        grid=(bsz, nb),
        in_specs=[pl.BlockSpec(memory_space=pltpu.SMEM)] + in_specs,
        out_specs=pl.BlockSpec((1, tq, N_Q_HEADS * HEAD_DIM), lambda b, t: (b, t, 0)),
        out_shape=jax.ShapeDtypeStruct((bsz, n, N_Q_HEADS * HEAD_DIM), BF16),
        scratch_shapes=scratch,
        compiler_params=_cparams(("arbitrary", "arbitrary")),
        name=("attn_band" if band else "attn_ctx") + ("_bounded" if bounded else ""),
    )(sink, *args)


def _oproj_kernel(o_ref, x_ref, wo_ref, g1_ref, n2g_ref, sh2_ref, sc2_ref, wrh_ref, wrl_ref,
                  x1_ref, h2_ref, aff_ref):
    y = jnp.dot(o_ref[0], wo_ref[...], preferred_element_type=F32)
    x1 = x_ref[0] + g1_ref[0] * y
    x1_ref[0] = x1
    h2, aff = _route(x1, n2g_ref[...], sh2_ref[0], sc2_ref[0], wrh_ref[...], wrl_ref[...])
    h2_ref[0] = h2
    aff_ref[0] = aff


def _oproj_layer(o, x, wo, g1, n2g, sh2, sc2, wrh, wrl, tt):
    bsz, n, d = x.shape
    return pl.pallas_call(
        _oproj_kernel,
        grid=(bsz, n // tt),
        in_specs=[
            pl.BlockSpec((1, tt, d), lambda b, t: (b, t, 0)),
            pl.BlockSpec((1, tt, d), lambda b, t: (b, t, 0)),
            pl.BlockSpec((d, d), lambda b, t: (0, 0)),
            _mod_spec(d), _row_spec(d), _mod_spec(d), _mod_spec(d),
            pl.BlockSpec((N_EXPERTS, d), lambda b, t: (0, 0)),
            pl.BlockSpec((N_EXPERTS, d), lambda b, t: (0, 0)),
        ],
        out_specs=[
            pl.BlockSpec((1, tt, d), lambda b, t: (b, t, 0)),
            pl.BlockSpec((1, tt, d), lambda b, t: (b, t, 0)),
            pl.BlockSpec((1, N_EXPERTS, tt), lambda b, t: (b, 0, t)),
        ],
        out_shape=[
            jax.ShapeDtypeStruct((bsz, n, d), F32),
            jax.ShapeDtypeStruct((bsz, n, d), BF16),
            jax.ShapeDtypeStruct((bsz, N_EXPERTS, n), F32),
        ],
        compiler_params=_cparams(("arbitrary", "arbitrary")),
        name="oproj_route",
    )(o, x, wo, g1, n2g, sh2, sc2, wrh, wrl)


def _prefix_counts(mask, n_chunks):
    ne = mask.shape[0]
    m2 = mask.reshape(ne * n_chunks, LANES).astype(BF16)
    ri = lax.broadcasted_iota(I32, (LANES, LANES), 0)
    ci = lax.broadcasted_iota(I32, (LANES, LANES), 1)
    upper = (ri <= ci).astype(BF16)
    within = jnp.dot(m2, upper, preferred_element_type=F32).reshape(ne, n_chunks, LANES)
    tot = jnp.dot(m2, jnp.ones((LANES, LANES), BF16), preferred_element_type=F32).reshape(ne, n_chunks, LANES)
    rc = lax.broadcasted_iota(I32, (n_chunks, n_chunks), 0)
    cc = lax.broadcasted_iota(I32, (n_chunks, n_chunks), 1)
    strict_lower = (cc < rc).astype(F32)
    off = jnp.stack([jnp.dot(strict_lower, tot[e], preferred_element_type=F32) for e in range(ne)])
    return within + off, off


def _select_kernel(a_ref, slot_ref, offc_ref, *, cap):
    a = a_ref[0]
    ne, n_chunks, _ = a.shape
    def count(pred):
        c = jnp.sum(pred.astype(F32), axis=1, keepdims=True)
        return jnp.sum(c, axis=2, keepdims=True)

    def body(i, thr):
        cand = thr | lax.shift_left(jnp.int32(1), 30 - i)
        return jnp.where(count(a >= lax.bitcast_convert_type(cand, F32)) >= cap, cand, thr)

    thr = lax.bitcast_convert_type(lax.fori_loop(0, 31, body, jnp.zeros((ne, 1, 1), I32)), F32)
    gt = a > thr
    eq = a == thr
    need = cap - count(gt)
    eq_rank, _ = _prefix_counts(eq.astype(F32), n_chunks)
    sel = gt | (eq & (eq_rank <= need))
    sel_f = sel.astype(F32)
    incl, off = _prefix_counts(sel_f, n_chunks)
    slot_ref[0] = jnp.where(sel, (incl - 1.0).astype(I32), -1)
    offc_ref[0] = off.astype(I32)


def _select(aff4, cap):
    bsz, ne, n_chunks, cw = aff4.shape
    spec = pl.BlockSpec((1, ne, n_chunks, cw), lambda b: (b, 0, 0, 0))
    return pl.pallas_call(
        functools.partial(_select_kernel, cap=cap),
        grid=(bsz,),
        in_specs=[spec],
        out_specs=[spec, spec],
        out_shape=[jax.ShapeDtypeStruct(aff4.shape, I32), jax.ShapeDtypeStruct(aff4.shape, I32)],
        compiler_params=_cparams(("arbitrary",)),
        name="select",
    )(aff4)


def _window(cap, tile):
    return min(tile + BF16_ROWS, cap)


def _window_start(off, cap, win):
    start = lax.shift_left(lax.shift_right_logical(off, BF16_ROWS_LOG2), BF16_ROWS_LOG2)
    return pl.multiple_of(jnp.minimum(start, cap - win), BF16_ROWS)


def _gather_kernel(offs_ref, small_ref, h2_ref, sl_ref, af_ref, xs_ref, val_ref, *, cap, tile, n_tiles):
    b, eg, s = pl.program_id(0), pl.program_id(1), pl.program_id(2)
    ge = sl_ref.shape[1]
    sub = h2_ref.shape[1] // tile
    full = _window(cap, tile)
    small = min(GATHER_SMALL, full)

    @pl.when(s == 0)
    def _():
        xs_ref[...] = jnp.zeros_like(xs_ref)
        val_ref[...] = jnp.zeros_like(val_ref)

    def tile_rows(u, win):
        ts = slice(u * tile, (u + 1) * tile)
        starts, hits = [], []
        for j in range(ge):
            off = offs_ref[(b * N_EXPERTS + eg * ge + j) * n_tiles + s * sub + u]
            start = _window_start(off, cap, win)
            rows = start + lax.broadcasted_iota(I32, (win, tile), 0)
            starts.append(start)
            hits.append(rows == sl_ref[0, j, :, ts])
        stacked = jnp.concatenate([h.astype(BF16) for h in hits], axis=0)
        res = jnp.dot(stacked, h2_ref[0, ts, :], preferred_element_type=F32)
        for j in range(ge):
            dst = pl.ds(starts[j], win)
            xs_ref[0, j, dst, :] = (xs_ref[0, j, dst, :].astype(F32) + res[j * win:(j + 1) * win]).astype(BF16)
            val_ref[0, j, dst, :] += jnp.sum(jnp.where(hits[j], af_ref[0, j, :, ts], 0.0), axis=1, keepdims=True)

    if small == full:
        for u in range(sub):
            tile_rows(u, full)
        return
    flags = [small_ref[(b * pl.num_programs(1) + eg) * n_tiles + s * sub + u] for u in range(sub)]
    all_fit = functools.reduce(jnp.minimum, flags) > 0

    @pl.when(all_fit)
    def _():
        for u in range(sub):
            tile_rows(u, small)

    @pl.when(jnp.logical_not(all_fit))
    def _():
        for u in range(sub):
            pl.when(flags[u] > 0)(functools.partial(tile_rows, u, small))
            pl.when(flags[u] <= 0)(functools.partial(tile_rows, u, full))


def _gather(offs, small, h2, slot, aff, cap, tile):
    bsz, n, d = h2.shape
    tg = min(n, GATHER_ROWS)
    n_tiles = n // tile
    ge = GATHER_EXPERTS
    kern = functools.partial(_gather_kernel, cap=cap, tile=tile, n_tiles=n_tiles)
    return pl.pallas_call(
        kern,
        grid_spec=pltpu.PrefetchScalarGridSpec(
            num_scalar_prefetch=2,
            grid=(bsz, N_EXPERTS // ge, n // tg),
            in_specs=[
                pl.BlockSpec((1, tg, d), lambda b, e, s, offs, small: (b, s, 0)),
                pl.BlockSpec((1, ge, 1, tg), lambda b, e, s, offs, small: (b, e, 0, s)),
                pl.BlockSpec((1, ge, 1, tg), lambda b, e, s, offs, small: (b, e, 0, s)),
            ],
            out_specs=[
                pl.BlockSpec((1, ge, cap, d), lambda b, e, s, offs, small: (b, e, 0, 0),
                             pipeline_mode=pl.Buffered(1)),
                pl.BlockSpec((1, ge, cap, 1), lambda b, e, s, offs, small: (b, e, 0, 0),
                             pipeline_mode=pl.Buffered(1)),
            ],
        ),
        out_shape=[
            jax.ShapeDtypeStruct((bsz, N_EXPERTS, cap, d), BF16),
            jax.ShapeDtypeStruct((bsz, N_EXPERTS, cap, 1), F32),
        ],
        compiler_params=_cparams(("arbitrary", "arbitrary", "arbitrary")),
        name="gather",
    )(offs, small, h2, slot.reshape(bsz, N_EXPERTS, 1, n), aff.reshape(bsz, N_EXPERTS, 1, n))


def _up_kernel(*refs, has_ctx):
    if has_ctx:
        xl_ref, xc_ref, wg_ref, wu_ref, h_ref, xs_ref = refs
        cl = xl_ref.shape[2]

        @pl.when(pl.program_id(2) == 0)
        def _():
            xs_ref[0:cl, :] = xl_ref[0, 0]
            xs_ref[cl:, :] = xc_ref[0, 0]

        xs_of = lambda rs: xs_ref[rs, :]
    else:
        xl_ref, wg_ref, wu_ref, h_ref = refs
        xs_of = lambda rs: xl_ref[0, 0, rs, :]
    wg = wg_ref[0, 0].astype(BF16)
    wu = wu_ref[0, 0].astype(BF16)
    rows = h_ref.shape[2]
    for r in range(UP_ROW_SPLIT):
        rs = slice(r * rows // UP_ROW_SPLIT, (r + 1) * rows // UP_ROW_SPLIT)
        xs = xs_of(rs)
        a = jnp.dot(xs, wg, preferred_element_type=F32)
        u = jnp.dot(xs, wu, preferred_element_type=F32)
        h_ref[0, 0, rs, :] = (a * jax.nn.sigmoid(a) * u).astype(BF16)


def _down_kernel(*refs, has_ctx):
    if has_ctx:
        h_ref, vl_ref, vc_ref, wd_ref, y_ref = refs
    else:
        h_ref, vl_ref, wd_ref, y_ref = refs
    cl = vl_ref.shape[2]
    wd = wd_ref[0, 0].astype(BF16)
    rows = h_ref.shape[2]
    half = rows // 2
    res = jnp.dot(h_ref[0, 0, 0:half, :], wd, preferred_element_type=F32)
    y_ref[0, 0, 0:half, :] = (res * vl_ref[0, 0, 0:half, :]).astype(BF16)
    val = vl_ref[0, 0, half:, :]
    if has_ctx:
        val = jnp.concatenate([val, vc_ref[0, 0]], axis=0)
    res = jnp.dot(h_ref[0, 0, half:, :], wd, preferred_element_type=F32)
    y_ref[0, 0, half:, :] = (res * val).astype(BF16)


def _mlp(layer, xl, vl, xc, vc, wg, wu, wd):
    bsz, ne, cl, d = xl.shape
    fdim = wg.shape[-1]
    has_ctx = xc is not None
    cc = xc.shape[2] if has_ctx else 0
    rows = cl + cc
    fc = UP_CHUNK
    dn = DOWN_CHUNK

    xspecs = [pl.BlockSpec((1, 1, cl, d), lambda e, b, f: (b, e, 0, 0))]
    if has_ctx:
        xspecs.append(pl.BlockSpec((1, 1, cc, d), lambda e, b, f: (b, e, 0, 0)))
    hid = pl.pallas_call(
        functools.partial(_up_kernel, has_ctx=has_ctx),
        grid=(ne, bsz, fdim // fc),
        in_specs=xspecs + [
            pl.BlockSpec((1, 1, d, fc), lambda e, b, f: (layer, e, 0, f)),
            pl.BlockSpec((1, 1, d, fc), lambda e, b, f: (layer, e, 0, f)),
        ],
        out_specs=pl.BlockSpec((1, 1, rows, fc), lambda e, b, f: (b, e, 0, f)),
        out_shape=jax.ShapeDtypeStruct((bsz, ne, rows, fdim), BF16),
        scratch_shapes=[pltpu.VMEM((rows, d), BF16)] if has_ctx else [],
        compiler_params=_cparams(("arbitrary", "arbitrary", "arbitrary")),
        name="experts_up",
    )(*((xl, xc) if has_ctx else (xl,)), wg, wu)

    vspecs = [pl.BlockSpec((1, 1, cl, 1), lambda e, b, j: (b, e, 0, 0))]
    if has_ctx:
        vspecs.append(pl.BlockSpec((1, 1, cc, 1), lambda e, b, j: (b, e, 0, 0)))
    return pl.pallas_call(
        functools.partial(_down_kernel, has_ctx=has_ctx),
        grid=(ne, bsz, d // dn),
        in_specs=[pl.BlockSpec((1, 1, rows, fdim), lambda e, b, j: (b, e, 0, 0))] + vspecs + [
            pl.BlockSpec((1, 1, fdim, dn), lambda e, b, j: (layer, e, 0, j)),
        ],
        out_specs=pl.BlockSpec((1, 1, rows, dn), lambda e, b, j: (b, e, 0, j)),
        out_shape=jax.ShapeDtypeStruct((bsz, ne, rows, d), BF16),
        compiler_params=_cparams(("arbitrary", "arbitrary", "arbitrary")),
        name="experts_down",
    )(hid, *((vl, vc) if has_ctx else (vl,)), wd)


def _combine_window(cap, tile):
    return min(tile, cap)


def _combine_kernel(offs_ref, ovf_ref, small_ref, x1_ref, g2_ref, sl_ref, y_hbm, o_ref,
                    ysm_ref, ybig_ref, yext_ref, sems, semb, semx, *, cap, tile, n_tiles, base, sub):
    b, tp = pl.program_id(0), pl.program_id(1)
    steps_per_sample = n_tiles // sub
    n_steps = pl.num_programs(0) * steps_per_sample
    win = _combine_window(cap, tile)
    small = min(COMBINE_SMALL, win)
    has_big = small < win
    step = b * steps_per_sample + tp
    buf = lax.rem(step, 2)

    def tile_of(stp, u):
        flat = stp * sub + u
        bb = flat // n_tiles
        return bb, flat - bb * n_tiles, flat

    def start_of(bb, tt, e, w):
        return _window_start(offs_ref[(bb * N_EXPERTS + e) * n_tiles + tt], cap, w)

    def small_copy(stp, u, bf, e):
        bb, tt, _ = tile_of(stp, u)
        src = pl.multiple_of(base + start_of(bb, tt, e, small), BF16_ROWS)
        return pltpu.make_async_copy(y_hbm.at[bb, e, pl.ds(src, small), :],
                                     ysm_ref.at[bf, u, pl.ds(e * small, small), :], sems.at[bf, u, e])

    def is_small(stp, u):
        return small_ref[tile_of(stp, u)[2]] > 0 if has_big else True

    def when(cond, fn):
        if cond is True:
            fn()
        else:
            pl.when(cond)(fn)

    def start_small(stp, bf):
        for u in range(sub):
            def go(u=u):
                for e in range(N_EXPERTS):
                    small_copy(stp, u, bf, e).start()
            when(is_small(stp, u), go)

    pl.when(step == 0)(functools.partial(start_small, step, buf))
    pl.when(step < n_steps - 1)(functools.partial(start_small, step + 1, 1 - buf))

    eye = (lax.broadcasted_iota(I32, (tile, tile), 0)
           == lax.broadcasted_iota(I32, (tile, tile), 1)).astype(BF16)

    def slots_by_token(u):
        sl = sl_ref[0, :, u * tile:(u + 1) * tile] + 1
        d_hi = lax.shift_right_logical(sl, SLOT_DIGIT_BITS).astype(F32).astype(BF16)
        d_lo = (sl & ((1 << SLOT_DIGIT_BITS) - 1)).astype(F32).astype(BF16)
        return (lax.dot_general(eye, d_hi, NT_DIMS, preferred_element_type=F32) * float(1 << SLOT_DIGIT_BITS)
                + lax.dot_general(eye, d_lo, NT_DIMS, preferred_element_type=F32)).astype(I32) - 1

    def one_hot_sum(u, w, slot_t, windows, wait):
        bb, tt, _ = tile_of(step, u)
        lane = lax.broadcasted_iota(I32, (tile, w), 1)
        hits = [(lane == slot_t[:, e:e + 1] - start_of(bb, tt, e, w)).astype(BF16) for e in range(N_EXPERTS)]
        wait()
        acc = None
        for g in range(0, N_EXPERTS, COMBINE_GROUP):
            part = jnp.dot(jnp.concatenate(hits[g:g + COMBINE_GROUP], axis=1),
                           windows[g * w:(g + COMBINE_GROUP) * w, :], preferred_element_type=F32)
            acc = part if acc is None else acc + part
        rs = slice(u * tile, (u + 1) * tile)
        o_ref[0, rs, :] = x1_ref[0, rs, :] + g2_ref[0] * acc

    def tile_small(u):
        def wait():
            for e in range(N_EXPERTS):
                small_copy(step, u, buf, e).wait()
        one_hot_sum(u, small, slots_by_token(u), ysm_ref.at[buf, u], wait)

    def tile_big(u):
        bb, tt, flat = tile_of(step, u)
        slot_t = slots_by_token(u)

        def big_copy(e):
            src = pl.multiple_of(base + start_of(bb, tt, e, win), BF16_ROWS)
            return pltpu.make_async_copy(y_hbm.at[bb, e, pl.ds(src, win), :],
                                         ybig_ref.at[pl.ds(e * win, win), :], semb.at[e])

        for e in range(N_EXPERTS):
            big_copy(e).start()

        def wait():
            for e in range(N_EXPERTS):
                big_copy(e).wait()
        one_hot_sum(u, win, slot_t, ybig_ref, wait)

        @pl.when(ovf_ref[flat] > 0)
        def _():
            def ext_start(e):
                s0 = start_of(bb, tt, e, win)
                return s0, pl.multiple_of(jnp.minimum(s0 + win, cap - BF16_ROWS), BF16_ROWS)

            def ext_copy(e):
                src = pl.multiple_of(base + ext_start(e)[1], BF16_ROWS)
                return pltpu.make_async_copy(y_hbm.at[bb, e, pl.ds(src, BF16_ROWS), :],
                                             yext_ref.at[pl.ds(e * BF16_ROWS, BF16_ROWS), :], semx.at[e])

            for e in range(N_EXPERTS):
                ext_copy(e).start()
            lane_x = lax.broadcasted_iota(I32, (tile, BF16_ROWS), 1)
            hits_x = []
            for e in range(N_EXPERTS):
                ext_copy(e).wait()
                s0, s1 = ext_start(e)
                row = s1 + lane_x
                hits_x.append(((row == slot_t[:, e:e + 1]) & (row >= s0 + win)).astype(BF16))
            extra = jnp.dot(jnp.concatenate(hits_x, axis=1), yext_ref[...], preferred_element_type=F32)
            rs = slice(u * tile, (u + 1) * tile)
            o_ref[0, rs, :] += g2_ref[0] * extra

    if not has_big:
        for u in range(sub):
            tile_small(u)
        return
    flags = [small_ref[tile_of(step, u)[2]] for u in range(sub)]
    all_small = functools.reduce(jnp.minimum, flags) > 0

    @pl.when(all_small)
    def _():
        for u in range(sub):
            tile_small(u)

    @pl.when(jnp.logical_not(all_small))
    def _():
        for u in range(sub):
            pl.when(flags[u] > 0)(functools.partial(tile_small, u))
            pl.when(flags[u] <= 0)(functools.partial(tile_big, u))


def _combine(offs, ovf, small, x1, g2, slot, y, cap, tile, base):
    bsz, n, d = x1.shape
    n_tiles = n // tile
    sub = COMBINE_TILES_PER_STEP if n_tiles % COMBINE_TILES_PER_STEP == 0 else 1
    win = _combine_window(cap, tile)
    win_small = min(COMBINE_SMALL, win)
    rows = sub * tile
    kern = functools.partial(_combine_kernel, cap=cap, tile=tile, n_tiles=n_tiles, base=base, sub=sub)
    return pl.pallas_call(
        kern,
        grid_spec=pltpu.PrefetchScalarGridSpec(
            num_scalar_prefetch=3,
            grid=(bsz, n_tiles // sub),
            in_specs=[
                pl.BlockSpec((1, rows, d), lambda b, t, *_: (b, t, 0)),
                pl.BlockSpec((1, 1, d), lambda b, t, *_: (b, 0, 0)),
                pl.BlockSpec((1, N_EXPERTS, rows), lambda b, t, *_: (b, 0, t)),
                pl.BlockSpec(memory_space=pl.ANY),
            ],
            out_specs=pl.BlockSpec((1, rows, d), lambda b, t, *_: (b, t, 0)),
            scratch_shapes=[pltpu.VMEM((2, sub, N_EXPERTS * win_small, d), BF16),
                            pltpu.VMEM((N_EXPERTS * win, d), BF16),
                            pltpu.VMEM((N_EXPERTS * BF16_ROWS, d), BF16),
                            pltpu.SemaphoreType.DMA((2, sub, N_EXPERTS)),
                            pltpu.SemaphoreType.DMA((N_EXPERTS,)),
                            pltpu.SemaphoreType.DMA((N_EXPERTS,))],
        ),
        out_shape=jax.ShapeDtypeStruct((bsz, n, d), F32),
        compiler_params=_cparams(("arbitrary", "arbitrary")),
        name="combine",
    )(offs, ovf, small, x1, g2, slot, y)


def _deinterleave(w, n_heads):
    lead = w.shape[:-1]
    w = w.reshape(lead + (n_heads, HEAD_DIM // 2, 2))
    return jnp.swapaxes(w, -1, -2).reshape(lead + (n_heads * HEAD_DIM,))


def _rope_tables(n):
    rows = n // GRID_W
    row = jnp.repeat(jnp.arange(rows), GRID_W).astype(F32)
    col = jnp.tile(jnp.arange(GRID_W), rows).astype(F32)
    n_freq = HEAD_DIM // 4
    inv = ROPE_THETA ** (-jnp.arange(n_freq, dtype=F32) / n_freq)
    ang = jnp.concatenate([row[:, None] * inv, col[:, None] * inv], axis=-1)
    cos = jnp.tile(jnp.cos(ang), (1, 4))
    sin = jnp.sin(ang)
    sin_signed = jnp.tile(jnp.concatenate([-sin, sin], axis=-1), (1, 2))
    return cos, sin_signed


def _split_bf16(w):
    hi = w.astype(BF16)
    return hi, (w - hi.astype(F32)).astype(BF16)


def _route_select(aff, cap, gather_tile, combine_tile):
    bsz, ne, n = aff.shape
    n_pad = max(n, HALO * LANES)
    a = aff if n_pad == n else jnp.pad(aff, ((0, 0), (0, 0), (0, n_pad - n)))
    slot4, offc4 = _select(a.reshape(bsz, ne, n_pad // LANES, LANES), cap)

    def tile_tables(tile):
        n_tiles = n // tile
        offs = offc4[:, :, ::tile // LANES, 0][:, :, :n_tiles]
        ends = jnp.concatenate([offs[:, :, 1:], jnp.full((bsz, ne, 1), cap, I32)], axis=2)

        def fits(win):
            return ends <= jnp.minimum((offs >> BF16_ROWS_LOG2) << BF16_ROWS_LOG2, cap - win) + win

        return offs.reshape(-1), fits

    g_offs, g_fits = tile_tables(gather_tile)
    gather = {
        "offs": g_offs,
        "small": jnp.all(g_fits(min(GATHER_SMALL, _window(cap, gather_tile)))
                         .reshape(bsz, ne // GATHER_EXPERTS, GATHER_EXPERTS, -1), axis=2).astype(I32).reshape(-1),
    }
    c_offs, c_fits = tile_tables(combine_tile)
    win_c = _combine_window(cap, combine_tile)
    combine = {
        "offs": c_offs,
        "overflow": jnp.any(~c_fits(win_c), axis=1).astype(I32).reshape(-1),
        "small": jnp.all(c_fits(min(COMBINE_SMALL, win_c)), axis=1).astype(I32).reshape(-1),
    }
    return slot4.reshape(bsz, ne, n_pad)[:, :, :n], gather, combine


def kernel(x, c, ctx, c_ctx, w_ada, b_ada, norm1_g, norm2_g, pool_w, pool_scale, attn_w_qkv, attn_w_o,
           attn_q_norm, attn_k_norm, attn_sink, router_w, exp_w_gate, exp_w_up, exp_w_down):
    bsz, n, d = x.shape
    lc = ctx.shape[1]
    cap = 2 * n // N_EXPERTS
    cap_c = 2 * lc // N_EXPERTS
    tile_c = min(COMBINE_TILE, lc)

    cc = jnp.zeros((8, d), F32).at[:bsz].set(c).at[bsz].set(c_ctx)
    mods = _ada(cc, w_ada, b_ada)

    cos, sin = _rope_tables(n)
    cos_c = jnp.ones((lc, LANES), F32)
    sin_c = jnp.zeros((lc, LANES), F32)
    blk = jnp.arange(HEAD_GROUP) // HEAD_DIM
    bavg = ((blk[:, None] == blk[None, :]).astype(F32) / HEAD_DIM).astype(BF16)

    def layer_mods(i):
        m = [mods[i, :, k * d:(k + 1) * d] for k in range(6)]
        lat = [v[:bsz, None, :] for v in m]
        con = [jnp.broadcast_to(v[bsz][None, None, :], (bsz, 1, d)) for v in m]
        return lat, con

    def qkv_params(i):
        j = i // 2
        nq = N_Q_HEADS * HEAD_DIM
        nk = N_KV_HEADS * HEAD_DIM
        wqkv = attn_w_qkv[j]
        wperm = jnp.concatenate([_deinterleave(wqkv[:, :nq], N_Q_HEADS),
                                 _deinterleave(wqkv[:, nq:nq + nk], N_KV_HEADS),
                                 wqkv[:, nq + nk:]], axis=1).astype(BF16)
        qg = jnp.tile(_deinterleave(attn_q_norm[j], 1), 4)[None, :]
        kg = jnp.tile(_deinterleave(attn_k_norm[j], 1), 4)[None, :]
        return wperm, bavg, qg, kg

    for i in range(DEPTH):
        update_ctx = i < DEPTH - 1
        j = i // 2
        is_pool = (i % 2) == 0
        (sh1, sc1, g1, sh2, sc2, g2), (csh1, csc1, cg1, csh2, csc2, cg2) = layer_mods(i)
        n1g = norm1_g[i][None, :]
        n2g = norm2_g[i][None, :]
        wrh, wrl = _split_bf16(router_w[i].T)

        if is_pool:
            pw = pool_w[j].astype(BF16)
            ps = pool_scale[j][None, :]
            x1, h2, aff = _pool_layer(x, n1g, sh1, sc1, g1, pw, ps, n2g, sh2, sc2, wrh, wrl, ROW_TILE)
            if update_ctx:
                c1, hc2, affc = _pool_layer(ctx, n1g, csh1, csc1, cg1, pw, ps, n2g, csh2, csc2, wrh, wrl, lc)
        else:
            wperm, _, qg, kg = qkv_params(i)
            wo = attn_w_o[j].astype(BF16)
            sink = attn_sink[j]
            bound = 1.01 * 8.0 * jnp.max(jnp.abs(attn_q_norm[j])) * jnp.max(jnp.abs(attn_k_norm[j]))
            q, k, v = _qkv_layer(x, n1g, sh1, sc1, wperm, bavg, qg, kg, cos, sin, ROW_TILE)
            qc, kc, vc = _qkv_layer(ctx, n1g, csh1, csc1, wperm, bavg, qg, kg, cos_c, sin_c, lc)
            o = _attn_layer(sink, bound, q, k, v, kc, vc, True)
            x1, h2, aff = _oproj_layer(o, x, wo, g1, n2g, sh2, sc2, wrh, wrl, ROW_TILE)
            if update_ctx:
                oc = _attn_layer(sink, bound, qc, None, None, kc, vc, False)
                c1, hc2, affc = _oproj_layer(oc, ctx, wo, cg1, n2g, csh2, csc2, wrh, wrl, lc)

        slot, tg, tc = _route_select(aff, cap, GATHER_TILE, COMBINE_TILE)
        xs, vals = _gather(tg["offs"], tg["small"], h2, slot, aff, cap, GATHER_TILE)
        if update_ctx:
            slot_c, cg, cc_ = _route_select(affc, cap_c, tile_c, tile_c)
            xs_c, vals_c = _gather(cg["offs"], cg["small"], hc2, slot_c, affc, cap_c, tile_c)
            y = _mlp(i, xs, vals, xs_c, vals_c, exp_w_gate, exp_w_up, exp_w_down)
            ctx = _combine(cc_["offs"], cc_["overflow"], cc_["small"], c1, cg2, slot_c, y, cap_c, tile_c, cap)
        else:
            y = _mlp(i, xs, vals, None, None, exp_w_gate, exp_w_up, exp_w_down)
        x = _combine(tc["offs"], tc["overflow"], tc["small"], x1, g2, slot, y, cap, COMBINE_TILE, 0)
    return x
```

```python
import functools

import jax
import jax.numpy as jnp
from jax import lax
from jax.experimental import pallas as pl
from jax.experimental.pallas import tpu as pltpu

F32 = jnp.float32
BF16 = jnp.bfloat16
I32 = jnp.int32

DEPTH = 4
GRID_W = 64
POOL_WINDOWS = (2, 4, 8, 16)
POOL_GROUP_DIM = 256
HEAD_DIM = 64
N_Q_HEADS = 16
N_KV_HEADS = 4
Q_PER_KV = 4
WINDOW = 128
BLOCK_Q = 128
ROPE_THETA = 10000.0
N_EXPERTS = 16
NORM_EPS = 1e-6
NEG_INF = -1e30

LANES = 128
BF16_ROWS = 16
BF16_ROWS_LOG2 = 4
MXU_DIM = 256
VMEM_LIMIT = 56 * 1024 * 1024

HALO = 8
ROW_TILE = 1024
ADA_COLS = 1536
HEAD_GROUP = MXU_DIM
ROW_SPLIT = 4
MAX_SHIFT = 40.0
SOFTMAX_ROWS = 32
GATHER_ROWS = 1024
GATHER_EXPERTS = 8
COMBINE_GROUP = 4
COMBINE_TILES_PER_STEP = 2
UP_CHUNK = 1024
UP_ROW_SPLIT = 2
DOWN_CHUNK = 1024
SLOT_DIGIT_BITS = 6
GATHER_TILE, GATHER_SMALL = 256, 64
COMBINE_TILE, COMBINE_SMALL = 256, 64

NT_DIMS = (((1,), (1,)), ((), ()))


def _cparams(sem):
    return pltpu.CompilerParams(dimension_semantics=sem, vmem_limit_bytes=VMEM_LIMIT)


def _row_pieces(rows):
    n = ROW_SPLIT if rows % (ROW_SPLIT * LANES) == 0 else 1
    return [slice(r * rows // n, (r + 1) * rows // n) for r in range(n)]


def _norm_mod(x, g, shift, scale):
    ms = jnp.mean(x * x, axis=-1, keepdims=True)
    return (x * lax.rsqrt(ms + NORM_EPS)) * g * (1.0 + scale) + shift


def _ada_kernel(cc_ref, w_ref, b_ref, o_ref):
    cc = cc_ref[...]
    s = cc * jax.nn.sigmoid(cc)
    o_ref[0] = jnp.dot(s, w_ref[0], preferred_element_type=F32,
                       precision=lax.Precision.HIGHEST) + b_ref[0]


def _ada(cc, w_ada, b_ada):
    depth, d, six_d = w_ada.shape
    tn = ADA_COLS
    return pl.pallas_call(
        _ada_kernel,
        grid=(depth, six_d // tn),
        in_specs=[
            pl.BlockSpec((8, d), lambda i, j: (0, 0)),
            pl.BlockSpec((1, d, tn), lambda i, j: (i, 0, j)),
            pl.BlockSpec((1, 1, tn), lambda i, j: (i, 0, j)),
        ],
        out_specs=pl.BlockSpec((1, 8, tn), lambda i, j: (i, 0, j)),
        out_shape=jax.ShapeDtypeStruct((depth, 8, six_d), F32),
        compiler_params=_cparams(("arbitrary", "arbitrary")),
        name="ada",
    )(cc, w_ada, b_ada.reshape(depth, 1, six_d))


def _route(x1, n2g, sh2, sc2, wr_hi, wr_lo):
    h2 = _norm_mod(x1, n2g, sh2, sc2)
    h_hi = h2.astype(BF16)
    h_lo = (h2 - h_hi.astype(F32)).astype(BF16)
    lg = (lax.dot_general(wr_hi, h_hi, NT_DIMS, preferred_element_type=F32)
          + lax.dot_general(wr_hi, h_lo, NT_DIMS, preferred_element_type=F32)
          + lax.dot_general(wr_lo, h_hi, NT_DIMS, preferred_element_type=F32))
    m = jnp.max(lg, axis=0, keepdims=True)
    ex = jnp.exp(lg - m)
    aff = ex / jnp.sum(ex, axis=0, keepdims=True)
    return h_hi, aff


def _pool_kernel(x_ref, xp_ref, xn_ref, n1g_ref, sh1_ref, sc1_ref, g1_ref, pw_ref, ps_ref,
                 n2g_ref, sh2_ref, sc2_ref, wrh_ref, wrl_ref,
                 x1_ref, h2_ref, aff_ref, hs_ref, p1_ref, p2_ref, p3_ref, *, n_tokens):
    t = pl.program_id(1)
    nt = pl.num_programs(1)
    tt = x_ref.shape[1]
    gd = POOL_GROUP_DIM
    n1g, sh1, sc1 = n1g_ref[...], sh1_ref[0], sc1_ref[0]
    x = x_ref[0]
    h = _norm_mod(x, n1g, sh1, sc1)
    hp = _norm_mod(xp_ref[0], n1g, sh1, sc1) * (t > 0).astype(F32)
    hn = _norm_mod(xn_ref[0], n1g, sh1, sc1) * (t < nt - 1).astype(F32)
    ext = tt + 2 * HALO
    hs_ref[0:HALO, :] = hp
    hs_ref[HALO:HALO + tt, :] = h
    hs_ref[HALO + tt:ext, :] = hn
    for ref in (hs_ref, p1_ref, p2_ref):
        ref[ext:, :] = jnp.zeros((ref.shape[0] - ext, ref.shape[1]), F32)
    p1_ref[0:ext, :] = hs_ref[0:ext, gd:] + hs_ref[1:ext + 1, gd:]
    p2_ref[0:ext, :] = p1_ref[0:ext, gd:] + p1_ref[2:ext + 2, gd:]
    p3_ref[...] = p2_ref[0:ext, gd:] + p2_ref[4:ext + 4, gd:]
    for rs in _row_pieces(tt):
        r0, nr = rs.start, rs.stop - rs.start

        def rows(ref, shift, cols=slice(0, gd)):
            return ref[r0 + shift:r0 + shift + nr, cols]

        sums = [
            rows(hs_ref, HALO - 1) + rows(hs_ref, HALO),
            rows(p1_ref, HALO - 2) + rows(p1_ref, HALO),
            rows(p2_ref, HALO - 4) + rows(p2_ref, HALO),
            rows(p3_ref, 0) + rows(p3_ref, HALO),
        ]
        pos = t * tt + r0 + lax.broadcasted_iota(I32, (nr, 1), 0)
        ys = []
        for g, w in enumerate(POOL_WINDOWS):
            half = w // 2
            c0 = g * POOL_GROUP_DIM
            lo = jnp.clip(pos - half, 0, n_tokens)
            hi = jnp.clip(pos + half, 0, n_tokens)
            p = sums[g] / (hi - lo).astype(F32) - rows(hs_ref, HALO, slice(c0, c0 + gd))
            ys.append(jnp.dot(p.astype(BF16), pw_ref[g], preferred_element_type=F32))
        y = jnp.concatenate(ys, axis=1) * ps_ref[...]
        x1 = x_ref[0, rs, :] + g1_ref[0] * y
        x1_ref[0, rs, :] = x1
        h2, aff = _route(x1, n2g_ref[...], sh2_ref[0], sc2_ref[0], wrh_ref[...], wrl_ref[...])
        h2_ref[0, rs, :] = h2
        aff_ref[0, :, rs] = aff


def _row_spec(d):
    return pl.BlockSpec((1, d), lambda b, t: (0, 0))


def _mod_spec(d):
    return pl.BlockSpec((1, 1, d), lambda b, t: (b, 0, 0))


def _pool_layer(x, n1g, sh1, sc1, g1, pw, ps, n2g, sh2, sc2, wrh, wrl, tt):
    bsz, n, d = x.shape
    assert POOL_WINDOWS == (2, 4, 8, 16) and d == 4 * POOL_GROUP_DIM
    nb8 = n // HALO
    r = tt // HALO
    kern = functools.partial(_pool_kernel, n_tokens=n)
    return pl.pallas_call(
        kern,
        grid=(bsz, n // tt),
        in_specs=[
            pl.BlockSpec((1, tt, d), lambda b, t: (b, t, 0)),
            pl.BlockSpec((1, HALO, d), lambda b, t: (b, jnp.maximum(t * r - 1, 0), 0)),
            pl.BlockSpec((1, HALO, d), lambda b, t: (b, jnp.minimum((t + 1) * r, nb8 - 1), 0)),
            _row_spec(d), _mod_spec(d), _mod_spec(d), _mod_spec(d),
            pl.BlockSpec((4, POOL_GROUP_DIM, POOL_GROUP_DIM), lambda b, t: (0, 0, 0)),
            _row_spec(d),
            _row_spec(d), _mod_spec(d), _mod_spec(d),
            pl.BlockSpec((N_EXPERTS, d), lambda b, t: (0, 0)),
            pl.BlockSpec((N_EXPERTS, d), lambda b, t: (0, 0)),
        ],
        out_specs=[
            pl.BlockSpec((1, tt, d), lambda b, t: (b, t, 0)),
            pl.BlockSpec((1, tt, d), lambda b, t: (b, t, 0)),
            pl.BlockSpec((1, N_EXPERTS, tt), lambda b, t: (b, 0, t)),
        ],
        out_shape=[
            jax.ShapeDtypeStruct((bsz, n, d), F32),
            jax.ShapeDtypeStruct((bsz, n, d), BF16),
            jax.ShapeDtypeStruct((bsz, N_EXPERTS, n), F32),
        ],
        scratch_shapes=[pltpu.VMEM((tt + 3 * HALO, d), F32),
                        pltpu.VMEM((tt + 3 * HALO, d - POOL_GROUP_DIM), F32),
                        pltpu.VMEM((tt + 3 * HALO, d - 2 * POOL_GROUP_DIM), F32),
                        pltpu.VMEM((tt + 2 * HALO, d - 3 * POOL_GROUP_DIM), F32)],
        compiler_params=_cparams(("arbitrary", "arbitrary")),
        name="pool_route",
    )(x, x, x, n1g, sh1, sc1, g1, pw, ps, n2g, sh2, sc2, wrh, wrl)


def _head_rms(v, bavg, gain):
    sq = v * v
    sq_hi = sq.astype(BF16)
    sq_lo = (sq - sq_hi.astype(F32)).astype(BF16)
    ms = (jnp.dot(sq_hi, bavg, preferred_element_type=F32)
          + jnp.dot(sq_lo, bavg, preferred_element_type=F32))
    return v * lax.rsqrt(ms + NORM_EPS) * gain


def _rope(v, cos, sin_signed):
    lane = lax.broadcasted_iota(I32, v.shape, 1)
    fwd = pltpu.roll(v, 32, 1)
    bwd = pltpu.roll(v, 96, 1)
    partner = jnp.where((lane % HEAD_DIM) < HEAD_DIM // 2, bwd, fwd)
    return v * cos + partner * sin_signed


def _qkv_kernel(x_ref, n1g_ref, sh1_ref, sc1_ref, w_ref, bavg_ref, qg_ref, kg_ref,
                cos_ref, sin_ref, q_ref, k_ref, v_ref):
    for rs in _row_pieces(x_ref.shape[1]):
        _qkv_rows(rs, x_ref, n1g_ref, sh1_ref, sc1_ref, w_ref, bavg_ref, qg_ref, kg_ref,
                  cos_ref, sin_ref, q_ref, k_ref, v_ref)


def _qkv_rows(rs, x_ref, n1g_ref, sh1_ref, sc1_ref, w_ref, bavg_ref, qg_ref, kg_ref,
              cos_ref, sin_ref, q_ref, k_ref, v_ref):
    h = _norm_mod(x_ref[0, rs, :], n1g_ref[...], sh1_ref[0], sc1_ref[0]).astype(BF16)
    qkv = jnp.dot(h, w_ref[...], preferred_element_type=F32)
    nq = N_Q_HEADS * HEAD_DIM
    nk = N_KV_HEADS * HEAD_DIM
    bavg = bavg_ref[...]
    cos, sin = cos_ref[rs, :], sin_ref[rs, :]
    scale = HEAD_DIM ** -0.5
    for c in range(nq // HEAD_GROUP):
        qn = _head_rms(qkv[:, c * HEAD_GROUP:(c + 1) * HEAD_GROUP], bavg, qg_ref[...])
        for s in range(2):
            qr = _rope(qn[:, s * LANES:(s + 1) * LANES], cos, sin) * scale
            for hh in range(2):
                head = c * 4 + s * 2 + hh
                q_ref[0, head, rs, :] = qr[:, hh * HEAD_DIM:(hh + 1) * HEAD_DIM].astype(BF16)
    kn = _head_rms(qkv[:, nq:nq + nk], bavg, kg_ref[...])
    v = qkv[:, nq + nk:]
    for s in range(2):
        kr = _rope(kn[:, s * LANES:(s + 1) * LANES], cos, sin)
        for hh in range(2):
            head = s * 2 + hh
            k_ref[0, head, rs, :] = kr[:, hh * HEAD_DIM:(hh + 1) * HEAD_DIM].astype(BF16)
            v_ref[0, head, rs, :] = v[:, head * HEAD_DIM:(head + 1) * HEAD_DIM].astype(BF16)


def _qkv_layer(x, n1g, sh1, sc1, w, bavg, qg, kg, cos, sin, tt):
    bsz, n, d = x.shape
    return pl.pallas_call(
        _qkv_kernel,
        grid=(bsz, n // tt),
        in_specs=[
            pl.BlockSpec((1, tt, d), lambda b, t: (b, t, 0)),
            _row_spec(d), _mod_spec(d), _mod_spec(d),
            pl.BlockSpec(w.shape, lambda b, t: (0, 0)),
            pl.BlockSpec((HEAD_GROUP, HEAD_GROUP), lambda b, t: (0, 0)),
            _row_spec(HEAD_GROUP), _row_spec(HEAD_GROUP),
            pl.BlockSpec((tt, LANES), lambda b, t: (t, 0)),
            pl.BlockSpec((tt, LANES), lambda b, t: (t, 0)),
        ],
        out_specs=[
            pl.BlockSpec((1, N_Q_HEADS, tt, HEAD_DIM), lambda b, t: (b, 0, t, 0)),
            pl.BlockSpec((1, N_KV_HEADS, tt, HEAD_DIM), lambda b, t: (b, 0, t, 0)),
            pl.BlockSpec((1, N_KV_HEADS, tt, HEAD_DIM), lambda b, t: (b, 0, t, 0)),
        ],
        out_shape=[
            jax.ShapeDtypeStruct((bsz, N_Q_HEADS, n, HEAD_DIM), BF16),
            jax.ShapeDtypeStruct((bsz, N_KV_HEADS, n, HEAD_DIM), BF16),
            jax.ShapeDtypeStruct((bsz, N_KV_HEADS, n, HEAD_DIM), BF16),
        ],
        compiler_params=_cparams(("arbitrary", "arbitrary")),
        name="qkv",
    )(x, n1g, sh1, sc1, w, bavg, qg, kg, cos, sin)


def _attn_kernel(sink_ref, q_ref, *refs, band, bounded):
    if band:
        kp_ref, kc_ref, kn_ref, vp_ref, vc_ref, vn_ref, kx_ref, vx_ref, bias_ref, o_ref = refs[:10]
    else:
        kx_ref, vx_ref, o_ref = refs[:3]
    if not bounded:
        s_all, p_all = refs[-2:]
    tq = q_ref.shape[2]
    n_chunks = Q_PER_KV * tq // SOFTMAX_ROWS
    for g in range(N_KV_HEADS):
        if band:
            kg = jnp.concatenate([kp_ref[0, g], kc_ref[0, g], kn_ref[0, g], kx_ref[0, g]], axis=0)
            vg = jnp.concatenate([vp_ref[0, g], vc_ref[0, g], vn_ref[0, g], vx_ref[0, g]], axis=0)
        else:
            kg = kx_ref[0, g]
            vg = vx_ref[0, g]
        vext = jnp.concatenate([vg, jnp.ones_like(vg)], axis=1)
        qg = q_ref[0, g * Q_PER_KV:(g + 1) * Q_PER_KV].reshape(Q_PER_KV * tq, HEAD_DIM)
        if bounded:
            shift = sink_ref[N_Q_HEADS]
            s = lax.dot_general(qg, kg, NT_DIMS, preferred_element_type=F32)
            if band:
                s = s + bias_ref[0]
            p = jnp.exp(s - shift).astype(BF16)
            sink_term = jnp.exp(jnp.concatenate(
                [jnp.full((tq, 1), sink_ref[g * Q_PER_KV + hh] - shift, F32) for hh in range(Q_PER_KV)], axis=0))
        else:
            s_ref, p_ref = s_all.at[g], p_all.at[g]
            s_ref[...] = lax.dot_general(qg, kg, NT_DIMS, preferred_element_type=F32)
            sink_terms = []
            for r in range(n_chunks):
                rs = slice(r * SOFTMAX_ROWS, (r + 1) * SOFTMAX_ROWS)
                sc = s_ref[rs, :]
                if band:
                    sc = sc + bias_ref[0, rs, :]
                sk = sink_ref[g * Q_PER_KV + (r * SOFTMAX_ROWS) // tq]
                m = jnp.maximum(jnp.max(sc, axis=-1, keepdims=True), sk)
                p_ref[rs, :] = jnp.exp(sc - m).astype(BF16)
                sink_terms.append(jnp.exp(sk - m))
            p = p_ref[...]
            sink_term = jnp.concatenate(sink_terms, axis=0)
        oe = jnp.dot(p, vext, preferred_element_type=F32)
        den = oe[:, HEAD_DIM:HEAD_DIM + 1] + sink_term
        o = (oe[:, :HEAD_DIM] / den).astype(o_ref.dtype)
        for hh in range(Q_PER_KV):
            head = g * Q_PER_KV + hh
            o_ref[0, :, head * HEAD_DIM:(head + 1) * HEAD_DIM] = o[hh * tq:(hh + 1) * tq]


def _band_bias(lc):
    nband = BLOCK_Q + 2 * WINDOW
    row = jnp.arange(Q_PER_KV * BLOCK_Q)[:, None] % BLOCK_Q
    col = jnp.arange(nband + lc)[None, :]
    dlt = col - row
    inband = (dlt >= 0) & (dlt <= 2 * WINDOW)
    variants = [inband & (col >= WINDOW), inband, inband & (col < WINDOW + BLOCK_Q)]
    return jnp.stack([jnp.where(v | (col >= nband), 0.0, NEG_INF) for v in variants]).astype(F32)


def _attn_layer(sink, score_bound, q, k, v, kx, vx, band):
    sink_ext = jnp.concatenate([sink.astype(F32), score_bound.reshape(1).astype(F32)])
    if not band:
        k, v = kx, vx
    return lax.cond(score_bound <= MAX_SHIFT,
                    functools.partial(_attn_call, band=band, bounded=True),
                    functools.partial(_attn_call, band=band, bounded=False),
                    sink_ext, q, k, v, kx, vx)


def _attn_call(sink, q, k, v, kx, vx, *, band, bounded):
    bsz, _, n, _ = q.shape
    lc = kx.shape[2]
    tq = BLOCK_Q if band else n
    nb = n // tq
    n_keys = (BLOCK_Q + 2 * WINDOW + lc) if band else lc
    qspec = pl.BlockSpec((1, N_Q_HEADS, tq, HEAD_DIM), lambda b, t: (b, 0, t, 0))
    xspec = pl.BlockSpec((1, N_KV_HEADS, lc, HEAD_DIM), lambda b, t: (b, 0, 0, 0))
    if band:
        prev = pl.BlockSpec((1, N_KV_HEADS, tq, HEAD_DIM), lambda b, t: (b, 0, jnp.maximum(t - 1, 0), 0))
        cur = pl.BlockSpec((1, N_KV_HEADS, tq, HEAD_DIM), lambda b, t: (b, 0, t, 0))
        nxt = pl.BlockSpec((1, N_KV_HEADS, tq, HEAD_DIM), lambda b, t: (b, 0, jnp.minimum(t + 1, nb - 1), 0))
        bias = _band_bias(lc)
        bspec = pl.BlockSpec((1,) + bias.shape[1:],
                             lambda b, t: (jnp.where(t == 0, 0, jnp.where(t == nb - 1, 2, 1)), 0, 0))
        in_specs = [qspec, prev, cur, nxt, prev, cur, nxt, xspec, xspec, bspec]
        args = (q, k, k, k, v, v, v, kx, vx, bias)
    else:
        in_specs = [qspec, xspec, xspec]
        args = (q, kx, vx)
    scratch = [] if bounded else [pltpu.VMEM((N_KV_HEADS, Q_PER_KV * tq, n_keys), F32),
                                  pltpu.VMEM((N_KV_HEADS, Q_PER_KV * tq, n_keys), BF16)]
    return pl.pallas_call(
        functools.partial(_attn_kernel, band=band, bounded=bounded),
        grid=(bsz, nb),
        in_specs=[pl.BlockSpec(memory_space=pltpu.SMEM)] + in_specs,
        out_specs=pl.BlockSpec((1, tq, N_Q_HEADS * HEAD_DIM), lambda b, t: (b, t, 0)),
        out_shape=jax.ShapeDtypeStruct((bsz, n, N_Q_HEADS * HEAD_DIM), BF16),
        scratch_shapes=scratch,
        compiler_params=_cparams(("arbitrary", "arbitrary")),
        name=("attn_band" if band else "attn_ctx") + ("_bounded" if bounded else ""),
    )(sink, *args)


def _oproj_kernel(o_ref, x_ref, wo_ref, g1_ref, n2g_ref, sh2_ref, sc2_ref, wrh_ref, wrl_ref,
                  x1_ref, h2_ref, aff_ref):
    y = jnp.dot(o_ref[0], wo_ref[...], preferred_element_type=F32)
    x1 = x_ref[0] + g1_ref[0] * y
    x1_ref[0] = x1
    h2, aff = _route(x1, n2g_ref[...], sh2_ref[0], sc2_ref[0], wrh_ref[...], wrl_ref[...])
    h2_ref[0] = h2
    aff_ref[0] = aff


def _oproj_layer(o, x, wo, g1, n2g, sh2, sc2, wrh, wrl, tt):
    bsz, n, d = x.shape
    return pl.pallas_call(
        _oproj_kernel,
        grid=(bsz, n // tt),
        in_specs=[
            pl.BlockSpec((1, tt, d), lambda b, t: (b, t, 0)),
            pl.BlockSpec((1, tt, d), lambda b, t: (b, t, 0)),
            pl.BlockSpec((d, d), lambda b, t: (0, 0)),
            _mod_spec(d), _row_spec(d), _mod_spec(d), _mod_spec(d),
            pl.BlockSpec((N_EXPERTS, d), lambda b, t: (0, 0)),
            pl.BlockSpec((N_EXPERTS, d), lambda b, t: (0, 0)),
        ],
        out_specs=[
            pl.BlockSpec((1, tt, d), lambda b, t: (b, t, 0)),
            pl.BlockSpec((1, tt, d), lambda b, t: (b, t, 0)),
            pl.BlockSpec((1, N_EXPERTS, tt), lambda b, t: (b, 0, t)),
        ],
        out_shape=[
            jax.ShapeDtypeStruct((bsz, n, d), F32),
            jax.ShapeDtypeStruct((bsz, n, d), BF16),
            jax.ShapeDtypeStruct((bsz, N_EXPERTS, n), F32),
        ],
        compiler_params=_cparams(("arbitrary", "arbitrary")),
        name="oproj_route",
    )(o, x, wo, g1, n2g, sh2, sc2, wrh, wrl)


def _prefix_counts(mask, n_chunks):
    ne = mask.shape[0]
    m2 = mask.reshape(ne * n_chunks, LANES).astype(BF16)
    ri = lax.broadcasted_iota(I32, (LANES, LANES), 0)
    ci = lax.broadcasted_iota(I32, (LANES, LANES), 1)
    upper = (ri <= ci).astype(BF16)
    within = jnp.dot(m2, upper, preferred_element_type=F32).reshape(ne, n_chunks, LANES)
    tot = jnp.dot(m2, jnp.ones((LANES, LANES), BF16), preferred_element_type=F32).reshape(ne, n_chunks, LANES)
    rc = lax.broadcasted_iota(I32, (n_chunks, n_chunks), 0)
    cc = lax.broadcasted_iota(I32, (n_chunks, n_chunks), 1)
    strict_lower = (cc < rc).astype(F32)
    off = jnp.stack([jnp.dot(strict_lower, tot[e], preferred_element_type=F32) for e in range(ne)])
    return within + off, off, tot


def _select_kernel(a_ref, slot_ref, offc_ref, flag_ref, *, cap, tile):
    a = a_ref[0]
    ne, n_chunks, _ = a.shape
    def count(pred):
        c = jnp.sum(pred.astype(F32), axis=1, keepdims=True)
        return jnp.sum(c, axis=2, keepdims=True)

    def body(i, thr):
        cand = thr | lax.shift_left(jnp.int32(1), 30 - i)
        return jnp.where(count(a >= lax.bitcast_convert_type(cand, F32)) >= cap, cand, thr)

    thr = lax.bitcast_convert_type(lax.fori_loop(0, 31, body, jnp.zeros((ne, 1, 1), I32)), F32)
    gt = a > thr
    eq = a == thr
    need = cap - count(gt)
    eq_rank, _, _ = _prefix_counts(eq.astype(F32), n_chunks)
    sel = gt | (eq & (eq_rank <= need))
    sel_f = sel.astype(F32)
    incl, off, tot = _prefix_counts(sel_f, n_chunks)
    slot_ref[0] = jnp.where(sel, (incl - 1.0).astype(I32), -1)
    offc_ref[0] = off.astype(I32)

    cpt = tile // LANES
    ends = off + tot
    if cpt > 1:
        ends = pltpu.roll(ends, n_chunks - (cpt - 1), 1)
    aligned = jnp.floor(off * (1.0 / BF16_ROWS)) * BF16_ROWS

    def fits(win):
        return (ends <= jnp.minimum(aligned, float(cap - win)) + win).astype(F32)

    win_c = _combine_window(cap, tile)
    fit_g = fits(min(GATHER_SMALL, _window(cap, tile)))
    rows = [jnp.min(fits(min(COMBINE_SMALL, win_c)), axis=0), 1.0 - jnp.min(fits(win_c), axis=0)]
    rows += [jnp.min(fit_g[g:g + GATHER_EXPERTS], axis=0) for g in range(0, ne, GATHER_EXPERTS)]
    flag_ref[0] = jnp.stack(rows).astype(I32)


def _select(aff4, cap, tile):
    bsz, ne, n_chunks, cw = aff4.shape
    n_flags = 2 + ne // GATHER_EXPERTS
    spec = pl.BlockSpec((1, ne, n_chunks, cw), lambda b: (b, 0, 0, 0))
    return pl.pallas_call(
        functools.partial(_select_kernel, cap=cap, tile=tile),
        grid=(bsz,),
        in_specs=[spec],
        out_specs=[spec, spec, pl.BlockSpec((1, n_flags, n_chunks, cw), lambda b: (b, 0, 0, 0))],
        out_shape=[jax.ShapeDtypeStruct(aff4.shape, I32), jax.ShapeDtypeStruct(aff4.shape, I32),
                   jax.ShapeDtypeStruct((bsz, n_flags, n_chunks, cw), I32)],
        compiler_params=_cparams(("arbitrary",)),
        name="select",
    )(aff4)


def _window(cap, tile):
    return min(tile + BF16_ROWS, cap)


def _window_start(off, cap, win):
    start = lax.shift_left(lax.shift_right_logical(off, BF16_ROWS_LOG2), BF16_ROWS_LOG2)
    return pl.multiple_of(jnp.minimum(start, cap - win), BF16_ROWS)


def _gather_kernel(offs_ref, small_ref, h2_ref, sl_ref, af_ref, xs_ref, val_ref, *, cap, tile, n_tiles):
    b, eg, s = pl.program_id(0), pl.program_id(1), pl.program_id(2)
    ge = sl_ref.shape[1]
    sub = h2_ref.shape[1] // tile
    full = _window(cap, tile)
    small = min(GATHER_SMALL, full)

    @pl.when(s == 0)
    def _():
        xs_ref[...] = jnp.zeros_like(xs_ref)
        val_ref[...] = jnp.zeros_like(val_ref)

    def tile_rows(u, win):
        ts = slice(u * tile, (u + 1) * tile)
        starts, hits = [], []
        for j in range(ge):
            off = offs_ref[(b * N_EXPERTS + eg * ge + j) * n_tiles + s * sub + u]
            start = _window_start(off, cap, win)
            rows = start + lax.broadcasted_iota(I32, (win, tile), 0)
            starts.append(start)
            hits.append(rows == sl_ref[0, j, :, ts])
        stacked = jnp.concatenate([h.astype(BF16) for h in hits], axis=0)
        res = jnp.dot(stacked, h2_ref[0, ts, :], preferred_element_type=F32)
        for j in range(ge):
            dst = pl.ds(starts[j], win)
            xs_ref[0, j, dst, :] = (xs_ref[0, j, dst, :].astype(F32) + res[j * win:(j + 1) * win]).astype(BF16)
            val_ref[0, j, dst, :] += jnp.sum(jnp.where(hits[j], af_ref[0, j, :, ts], 0.0), axis=1, keepdims=True)

    if small == full:
        for u in range(sub):
            tile_rows(u, full)
        return
    flags = [small_ref[(b * pl.num_programs(1) + eg) * n_tiles + s * sub + u] for u in range(sub)]
    all_fit = functools.reduce(jnp.minimum, flags) > 0

    @pl.when(all_fit)
    def _():
        for u in range(sub):
            tile_rows(u, small)

    @pl.when(jnp.logical_not(all_fit))
    def _():
        for u in range(sub):
            pl.when(flags[u] > 0)(functools.partial(tile_rows, u, small))
            pl.when(flags[u] <= 0)(functools.partial(tile_rows, u, full))


def _gather(offs, small, h2, slot, aff, cap, tile):
    bsz, n, d = h2.shape
    tg = min(n, GATHER_ROWS)
    n_tiles = n // tile
    ge = GATHER_EXPERTS
    kern = functools.partial(_gather_kernel, cap=cap, tile=tile, n_tiles=n_tiles)
    return pl.pallas_call(
        kern,
        grid_spec=pltpu.PrefetchScalarGridSpec(
            num_scalar_prefetch=2,
            grid=(bsz, N_EXPERTS // ge, n // tg),
            in_specs=[
                pl.BlockSpec((1, tg, d), lambda b, e, s, offs, small: (b, s, 0)),
                pl.BlockSpec((1, ge, 1, tg), lambda b, e, s, offs, small: (b, e, 0, s)),
                pl.BlockSpec((1, ge, 1, tg), lambda b, e, s, offs, small: (b, e, 0, s)),
            ],
            out_specs=[
                pl.BlockSpec((1, ge, cap, d), lambda b, e, s, offs, small: (b, e, 0, 0),
                             pipeline_mode=pl.Buffered(1)),
                pl.BlockSpec((1, ge, cap, 1), lambda b, e, s, offs, small: (b, e, 0, 0),
                             pipeline_mode=pl.Buffered(1)),
            ],
        ),
        out_shape=[
            jax.ShapeDtypeStruct((bsz, N_EXPERTS, cap, d), BF16),
            jax.ShapeDtypeStruct((bsz, N_EXPERTS, cap, 1), F32),
        ],
        compiler_params=_cparams(("arbitrary", "arbitrary", "arbitrary")),
        name="gather",
    )(offs, small, h2, slot.reshape(bsz, N_EXPERTS, 1, n), aff.reshape(bsz, N_EXPERTS, 1, n))


def _up_kernel(*refs, has_ctx):
    if has_ctx:
        xl_ref, xc_ref, wg_ref, wu_ref, h_ref, xs_ref = refs
        cl = xl_ref.shape[2]

        @pl.when(pl.program_id(2) == 0)
        def _():
            xs_ref[0:cl, :] = xl_ref[0, 0]
            xs_ref[cl:, :] = xc_ref[0, 0]

        xs_of = lambda rs: xs_ref[rs, :]
    else:
        xl_ref, wg_ref, wu_ref, h_ref = refs
        xs_of = lambda rs: xl_ref[0, 0, rs, :]
    wg = wg_ref[0, 0].astype(BF16)
    wu = wu_ref[0, 0].astype(BF16)
    rows = h_ref.shape[2]
    for r in range(UP_ROW_SPLIT):
        rs = slice(r * rows // UP_ROW_SPLIT, (r + 1) * rows // UP_ROW_SPLIT)
        xs = xs_of(rs)
        a = jnp.dot(xs, wg, preferred_element_type=F32)
        u = jnp.dot(xs, wu, preferred_element_type=F32)
        h_ref[0, 0, rs, :] = (a * jax.nn.sigmoid(a) * u).astype(BF16)


def _down_kernel(*refs, has_ctx):
    if has_ctx:
        h_ref, vl_ref, vc_ref, wd_ref, y_ref = refs
    else:
        h_ref, vl_ref, wd_ref, y_ref = refs
    cl = vl_ref.shape[2]
    wd = wd_ref[0, 0].astype(BF16)
    rows = h_ref.shape[2]
    half = rows // 2
    res = jnp.dot(h_ref[0, 0, 0:half, :], wd, preferred_element_type=F32)
    y_ref[0, 0, 0:half, :] = (res * vl_ref[0, 0, 0:half, :]).astype(BF16)
    val = vl_ref[0, 0, half:, :]
    if has_ctx:
        val = jnp.concatenate([val, vc_ref[0, 0]], axis=0)
    res = jnp.dot(h_ref[0, 0, half:, :], wd, preferred_element_type=F32)
    y_ref[0, 0, half:, :] = (res * val).astype(BF16)


def _mlp(layer, xl, vl, xc, vc, wg, wu, wd):
    bsz, ne, cl, d = xl.shape
    fdim = wg.shape[-1]
    has_ctx = xc is not None
    cc = xc.shape[2] if has_ctx else 0
    rows = cl + cc
    fc = UP_CHUNK
    dn = DOWN_CHUNK

    xspecs = [pl.BlockSpec((1, 1, cl, d), lambda e, b, f: (b, e, 0, 0))]
    if has_ctx:
        xspecs.append(pl.BlockSpec((1, 1, cc, d), lambda e, b, f: (b, e, 0, 0)))
    hid = pl.pallas_call(
        functools.partial(_up_kernel, has_ctx=has_ctx),
        grid=(ne, bsz, fdim // fc),
        in_specs=xspecs + [
            pl.BlockSpec((1, 1, d, fc), lambda e, b, f: (layer, e, 0, f)),
            pl.BlockSpec((1, 1, d, fc), lambda e, b, f: (layer, e, 0, f)),
        ],
        out_specs=pl.BlockSpec((1, 1, rows, fc), lambda e, b, f: (b, e, 0, f)),
        out_shape=jax.ShapeDtypeStruct((bsz, ne, rows, fdim), BF16),
        scratch_shapes=[pltpu.VMEM((rows, d), BF16)] if has_ctx else [],
        compiler_params=_cparams(("arbitrary", "arbitrary", "arbitrary")),
        name="experts_up",
    )(*((xl, xc) if has_ctx else (xl,)), wg, wu)

    vspecs = [pl.BlockSpec((1, 1, cl, 1), lambda e, b, j: (b, e, 0, 0))]
    if has_ctx:
        vspecs.append(pl.BlockSpec((1, 1, cc, 1), lambda e, b, j: (b, e, 0, 0)))
    return pl.pallas_call(
        functools.partial(_down_kernel, has_ctx=has_ctx),
        grid=(ne, bsz, d // dn),
        in_specs=[pl.BlockSpec((1, 1, rows, fdim), lambda e, b, j: (b, e, 0, 0))] + vspecs + [
            pl.BlockSpec((1, 1, fdim, dn), lambda e, b, j: (layer, e, 0, j)),
        ],
        out_specs=pl.BlockSpec((1, 1, rows, dn), lambda e, b, j: (b, e, 0, j)),
        out_shape=jax.ShapeDtypeStruct((bsz, ne, rows, d), BF16),
        compiler_params=_cparams(("arbitrary", "arbitrary", "arbitrary")),
        name="experts_down",
    )(hid, *((vl, vc) if has_ctx else (vl,)), wd)


def _combine_window(cap, tile):
    return min(tile, cap)


def _combine_kernel(offs_ref, ovf_ref, small_ref, x1_ref, g2_ref, sl_ref, y_hbm, o_ref,
                    ysm_ref, ybig_ref, yext_ref, sems, semb, semx, *, cap, tile, n_tiles, base, sub):
    b, tp = pl.program_id(0), pl.program_id(1)
    steps_per_sample = n_tiles // sub
    n_steps = pl.num_programs(0) * steps_per_sample
    win = _combine_window(cap, tile)
    small = min(COMBINE_SMALL, win)
    has_big = small < win
    step = b * steps_per_sample + tp
    buf = lax.rem(step, 2)

    def tile_of(stp, u):
        flat = stp * sub + u
        bb = flat // n_tiles
        return bb, flat - bb * n_tiles, flat

    def start_of(bb, tt, e, w):
        return _window_start(offs_ref[(bb * N_EXPERTS + e) * n_tiles + tt], cap, w)

    def small_copy(stp, u, bf, e):
        bb, tt, _ = tile_of(stp, u)
        src = pl.multiple_of(base + start_of(bb, tt, e, small), BF16_ROWS)
        return pltpu.make_async_copy(y_hbm.at[bb, e, pl.ds(src, small), :],
                                     ysm_ref.at[bf, u, pl.ds(e * small, small), :], sems.at[bf, u, e])

    def is_small(stp, u):
        return small_ref[tile_of(stp, u)[2]] > 0 if has_big else True

    def when(cond, fn):
        if cond is True:
            fn()
        else:
            pl.when(cond)(fn)

    def start_small(stp, bf):
        for u in range(sub):
            def go(u=u):
                for e in range(N_EXPERTS):
                    small_copy(stp, u, bf, e).start()
            when(is_small(stp, u), go)

    pl.when(step == 0)(functools.partial(start_small, step, buf))
    pl.when(step < n_steps - 1)(functools.partial(start_small, step + 1, 1 - buf))

    eye = (lax.broadcasted_iota(I32, (tile, tile), 0)
           == lax.broadcasted_iota(I32, (tile, tile), 1)).astype(BF16)

    def slots_by_token(u):
        sl = sl_ref[0, :, u * tile:(u + 1) * tile] + 1
        d_hi = lax.shift_right_logical(sl, SLOT_DIGIT_BITS).astype(F32).astype(BF16)
        d_lo = (sl & ((1 << SLOT_DIGIT_BITS) - 1)).astype(F32).astype(BF16)
        return (lax.dot_general(eye, d_hi, NT_DIMS, preferred_element_type=F32) * float(1 << SLOT_DIGIT_BITS)
                + lax.dot_general(eye, d_lo, NT_DIMS, preferred_element_type=F32)).astype(I32) - 1

    def one_hot_sum(u, w, slot_t, windows, wait):
        bb, tt, _ = tile_of(step, u)
        lane = lax.broadcasted_iota(I32, (tile, w), 1)
        hits = [(lane == slot_t[:, e:e + 1] - start_of(bb, tt, e, w)).astype(BF16) for e in range(N_EXPERTS)]
        wait()
        acc = None
        for g in range(0, N_EXPERTS, COMBINE_GROUP):
            part = jnp.dot(jnp.concatenate(hits[g:g + COMBINE_GROUP], axis=1),
                           windows[g * w:(g + COMBINE_GROUP) * w, :], preferred_element_type=F32)
            acc = part if acc is None else acc + part
        rs = slice(u * tile, (u + 1) * tile)
        o_ref[0, rs, :] = x1_ref[0, rs, :] + g2_ref[0] * acc

    def tile_small(u):
        def wait():
            for e in range(N_EXPERTS):
                small_copy(step, u, buf, e).wait()
        one_hot_sum(u, small, slots_by_token(u), ysm_ref.at[buf, u], wait)

    def tile_big(u):
        bb, tt, flat = tile_of(step, u)
        slot_t = slots_by_token(u)

        def big_copy(e):
            src = pl.multiple_of(base + start_of(bb, tt, e, win), BF16_ROWS)
            return pltpu.make_async_copy(y_hbm.at[bb, e, pl.ds(src, win), :],
                                         ybig_ref.at[pl.ds(e * win, win), :], semb.at[e])

        for e in range(N_EXPERTS):
            big_copy(e).start()

        def wait():
            for e in range(N_EXPERTS):
                big_copy(e).wait()
        one_hot_sum(u, win, slot_t, ybig_ref, wait)

        @pl.when(ovf_ref[flat] > 0)
        def _():
            def ext_start(e):
                s0 = start_of(bb, tt, e, win)
                return s0, pl.multiple_of(jnp.minimum(s0 + win, cap - BF16_ROWS), BF16_ROWS)

            def ext_copy(e):
                src = pl.multiple_of(base + ext_start(e)[1], BF16_ROWS)
                return pltpu.make_async_copy(y_hbm.at[bb, e, pl.ds(src, BF16_ROWS), :],
                                             yext_ref.at[pl.ds(e * BF16_ROWS, BF16_ROWS), :], semx.at[e])

            for e in range(N_EXPERTS):
                ext_copy(e).start()
            lane_x = lax.broadcasted_iota(I32, (tile, BF16_ROWS), 1)
            hits_x = []
            for e in range(N_EXPERTS):
                ext_copy(e).wait()
                s0, s1 = ext_start(e)
                row = s1 + lane_x
                hits_x.append(((row == slot_t[:, e:e + 1]) & (row >= s0 + win)).astype(BF16))
            extra = jnp.dot(jnp.concatenate(hits_x, axis=1), yext_ref[...], preferred_element_type=F32)
            rs = slice(u * tile, (u + 1) * tile)
            o_ref[0, rs, :] += g2_ref[0] * extra

    if not has_big:
        for u in range(sub):
            tile_small(u)
        return
    flags = [small_ref[tile_of(step, u)[2]] for u in range(sub)]
    all_small = functools.reduce(jnp.minimum, flags) > 0

    @pl.when(all_small)
    def _():
        for u in range(sub):
            tile_small(u)

    @pl.when(jnp.logical_not(all_small))
    def _():
        for u in range(sub):
            pl.when(flags[u] > 0)(functools.partial(tile_small, u))
            pl.when(flags[u] <= 0)(functools.partial(tile_big, u))


def _combine(offs, ovf, small, x1, g2, slot, y, cap, tile, base):
    bsz, n, d = x1.shape
    n_tiles = n // tile
    sub = COMBINE_TILES_PER_STEP if n_tiles % COMBINE_TILES_PER_STEP == 0 else 1
    win = _combine_window(cap, tile)
    win_small = min(COMBINE_SMALL, win)
    rows = sub * tile
    kern = functools.partial(_combine_kernel, cap=cap, tile=tile, n_tiles=n_tiles, base=base, sub=sub)
    return pl.pallas_call(
        kern,
        grid_spec=pltpu.PrefetchScalarGridSpec(
            num_scalar_prefetch=3,
            grid=(bsz, n_tiles // sub),
            in_specs=[
                pl.BlockSpec((1, rows, d), lambda b, t, *_: (b, t, 0)),
                pl.BlockSpec((1, 1, d), lambda b, t, *_: (b, 0, 0)),
                pl.BlockSpec((1, N_EXPERTS, rows), lambda b, t, *_: (b, 0, t)),
                pl.BlockSpec(memory_space=pl.ANY),
            ],
            out_specs=pl.BlockSpec((1, rows, d), lambda b, t, *_: (b, t, 0)),
            scratch_shapes=[pltpu.VMEM((2, sub, N_EXPERTS * win_small, d), BF16),
                            pltpu.VMEM((N_EXPERTS * win, d), BF16),
                            pltpu.VMEM((N_EXPERTS * BF16_ROWS, d), BF16),
                            pltpu.SemaphoreType.DMA((2, sub, N_EXPERTS)),
                            pltpu.SemaphoreType.DMA((N_EXPERTS,)),
                            pltpu.SemaphoreType.DMA((N_EXPERTS,))],
        ),
        out_shape=jax.ShapeDtypeStruct((bsz, n, d), F32),
        compiler_params=_cparams(("arbitrary", "arbitrary")),
        name="combine",
    )(offs, ovf, small, x1, g2, slot, y)


def _deinterleave(w, n_heads):
    lead = w.shape[:-1]
    w = w.reshape(lead + (n_heads, HEAD_DIM // 2, 2))
    return jnp.swapaxes(w, -1, -2).reshape(lead + (n_heads * HEAD_DIM,))


def _rope_tables(n):
    rows = n // GRID_W
    row = jnp.repeat(jnp.arange(rows), GRID_W).astype(F32)
    col = jnp.tile(jnp.arange(GRID_W), rows).astype(F32)
    n_freq = HEAD_DIM // 4
    inv = ROPE_THETA ** (-jnp.arange(n_freq, dtype=F32) / n_freq)
    ang = jnp.concatenate([row[:, None] * inv, col[:, None] * inv], axis=-1)
    cos = jnp.tile(jnp.cos(ang), (1, 4))
    sin = jnp.sin(ang)
    sin_signed = jnp.tile(jnp.concatenate([-sin, sin], axis=-1), (1, 2))
    return cos, sin_signed


def _split_bf16(w):
    hi = w.astype(BF16)
    return hi, (w - hi.astype(F32)).astype(BF16)


def _route_select(aff, cap, gather_tile, combine_tile):
    bsz, ne, n = aff.shape
    n_pad = max(n, HALO * LANES)
    a = aff if n_pad == n else jnp.pad(aff, ((0, 0), (0, 0), (0, n_pad - n)))
    assert gather_tile == combine_tile
    tile = gather_tile
    slot4, offc4, flags = _select(a.reshape(bsz, ne, n_pad // LANES, LANES), cap, tile)

    def per_tile(table):
        return table[..., ::tile // LANES, 0][..., :n // tile].reshape(-1)

    offs = per_tile(offc4)
    gather = {"offs": offs, "small": per_tile(flags[:, 2:])}
    combine = {"offs": offs, "small": per_tile(flags[:, 0]), "overflow": per_tile(flags[:, 1])}
    return slot4.reshape(bsz, ne, n_pad)[:, :, :n], gather, combine


def kernel(x, c, ctx, c_ctx, w_ada, b_ada, norm1_g, norm2_g, pool_w, pool_scale, attn_w_qkv, attn_w_o,
           attn_q_norm, attn_k_norm, attn_sink, router_w, exp_w_gate, exp_w_up, exp_w_down):
    bsz, n, d = x.shape
    lc = ctx.shape[1]
    cap = 2 * n // N_EXPERTS
    cap_c = 2 * lc // N_EXPERTS
    tile_c = min(COMBINE_TILE, lc)

    cc = jnp.zeros((8, d), F32).at[:bsz].set(c).at[bsz].set(c_ctx)
    mods = _ada(cc, w_ada, b_ada)

    cos, sin = _rope_tables(n)
    cos_c = jnp.ones((lc, LANES), F32)
    sin_c = jnp.zeros((lc, LANES), F32)
    blk = jnp.arange(HEAD_GROUP) // HEAD_DIM
    bavg = ((blk[:, None] == blk[None, :]).astype(F32) / HEAD_DIM).astype(BF16)

    def layer_mods(i):
        m = [mods[i, :, k * d:(k + 1) * d] for k in range(6)]
        lat = [v[:bsz, None, :] for v in m]
        con = [jnp.broadcast_to(v[bsz][None, None, :], (bsz, 1, d)) for v in m]
        return lat, con

    def qkv_params(i):
        j = i // 2
        nq = N_Q_HEADS * HEAD_DIM
        nk = N_KV_HEADS * HEAD_DIM
        wqkv = attn_w_qkv[j]
        wperm = jnp.concatenate([_deinterleave(wqkv[:, :nq], N_Q_HEADS),
                                 _deinterleave(wqkv[:, nq:nq + nk], N_KV_HEADS),
                                 wqkv[:, nq + nk:]], axis=1).astype(BF16)
        qg = jnp.tile(_deinterleave(attn_q_norm[j], 1), 4)[None, :]
        kg = jnp.tile(_deinterleave(attn_k_norm[j], 1), 4)[None, :]
        return wperm, bavg, qg, kg

    for i in range(DEPTH):
        update_ctx = i < DEPTH - 1
        j = i // 2
        is_pool = (i % 2) == 0
        (sh1, sc1, g1, sh2, sc2, g2), (csh1, csc1, cg1, csh2, csc2, cg2) = layer_mods(i)
        n1g = norm1_g[i][None, :]
        n2g = norm2_g[i][None, :]
        wrh, wrl = _split_bf16(router_w[i].T)

        if is_pool:
            pw = pool_w[j].astype(BF16)
            ps = pool_scale[j][None, :]
            x1, h2, aff = _pool_layer(x, n1g, sh1, sc1, g1, pw, ps, n2g, sh2, sc2, wrh, wrl, ROW_TILE)
            if update_ctx:
                c1, hc2, affc = _pool_layer(ctx, n1g, csh1, csc1, cg1, pw, ps, n2g, csh2, csc2, wrh, wrl, lc)
        else:
            wperm, _, qg, kg = qkv_params(i)
            wo = attn_w_o[j].astype(BF16)
            sink = attn_sink[j]
            bound = 1.01 * 8.0 * jnp.max(jnp.abs(attn_q_norm[j])) * jnp.max(jnp.abs(attn_k_norm[j]))
            q, k, v = _qkv_layer(x, n1g, sh1, sc1, wperm, bavg, qg, kg, cos, sin, ROW_TILE)
            qc, kc, vc = _qkv_layer(ctx, n1g, csh1, csc1, wperm, bavg, qg, kg, cos_c, sin_c, lc)
            o = _attn_layer(sink, bound, q, k, v, kc, vc, True)
            x1, h2, aff = _oproj_layer(o, x, wo, g1, n2g, sh2, sc2, wrh, wrl, ROW_TILE)
            if update_ctx:
                oc = _attn_layer(sink, bound, qc, None, None, kc, vc, False)
                c1, hc2, affc = _oproj_layer(oc, ctx, wo, cg1, n2g, csh2, csc2, wrh, wrl, lc)

        slot, tg, tc = _route_select(aff, cap, GATHER_TILE, COMBINE_TILE)
        xs, vals = _gather(tg["offs"], tg["small"], h2, slot, aff, cap, GATHER_TILE)
        if update_ctx:
            slot_c, cg, cc_ = _route_select(affc, cap_c, tile_c, tile_c)
            xs_c, vals_c = _gather(cg["offs"], cg["small"], hc2, slot_c, affc, cap_c, tile_c)
            y = _mlp(i, xs, vals, xs_c, vals_c, exp_w_gate, exp_w_up, exp_w_down)
            ctx = _combine(cc_["offs"], cc_["overflow"], cc_["small"], c1, cg2, slot_c, y, cap_c, tile_c, cap)
        else:
            y = _mlp(i, xs, vals, None, None, exp_w_gate, exp_w_up, exp_w_down)
        x = _combine(tc["offs"], tc["overflow"], tc["small"], x1, g2, slot, y, cap, COMBINE_TILE, 0)
    return x
```

```python
import functools

import jax
import jax.numpy as jnp
from jax import lax
from jax.experimental import pallas as pl
from jax.experimental.pallas import tpu as pltpu

F32 = jnp.float32
BF16 = jnp.bfloat16
I32 = jnp.int32

DEPTH = 4
GRID_W = 64
POOL_WINDOWS = (2, 4, 8, 16)
POOL_GROUP_DIM = 256
HEAD_DIM = 64
N_Q_HEADS = 16
N_KV_HEADS = 4
Q_PER_KV = 4
WINDOW = 128
BLOCK_Q = 128
ROPE_THETA = 10000.0
N_EXPERTS = 16
NORM_EPS = 1e-6
NEG_INF = -1e30

LANES = 128
BF16_ROWS = 16
BF16_ROWS_LOG2 = 4
MXU_DIM = 256
VMEM_LIMIT = 56 * 1024 * 1024

HALO = 8
ROW_TILE = 1024
ADA_COLS = 1536
HEAD_GROUP = MXU_DIM
ROW_SPLIT = 8
MAX_SHIFT = 40.0
SOFTMAX_ROWS = 32
GATHER_ROWS = 2048
GATHER_EXPERTS = 8
COMBINE_GROUP = 4
COMBINE_TILES_PER_STEP = 2
UP_CHUNK = 1024
UP_ROW_SPLIT = 2
DOWN_CHUNK = 1024
SLOT_DIGIT_BITS = 6
GATHER_TILE, GATHER_SMALL = 256, 64
COMBINE_TILE, COMBINE_SMALL = 256, 64

NT_DIMS = (((1,), (1,)), ((), ()))


def _cparams(sem):
    return pltpu.CompilerParams(dimension_semantics=sem, vmem_limit_bytes=VMEM_LIMIT)


def _row_pieces(rows):
    n = ROW_SPLIT if rows % (ROW_SPLIT * LANES) == 0 else 1
    return [slice(r * rows // n, (r + 1) * rows // n) for r in range(n)]


def _norm_mod(x, g, shift, scale):
    ms = jnp.mean(x * x, axis=-1, keepdims=True)
    return (x * lax.rsqrt(ms + NORM_EPS)) * g * (1.0 + scale) + shift


def _ada_kernel(cc_ref, w_ref, b_ref, o_ref):
    cc = cc_ref[...]
    s = cc * jax.nn.sigmoid(cc)
    o_ref[0] = jnp.dot(s, w_ref[0], preferred_element_type=F32,
                       precision=lax.Precision.HIGHEST) + b_ref[0]


def _ada(cc, w_ada, b_ada):
    depth, d, six_d = w_ada.shape
    tn = ADA_COLS
    return pl.pallas_call(
        _ada_kernel,
        grid=(depth, six_d // tn),
        in_specs=[
            pl.BlockSpec((8, d), lambda i, j: (0, 0)),
            pl.BlockSpec((1, d, tn), lambda i, j: (i, 0, j)),
            pl.BlockSpec((1, 1, tn), lambda i, j: (i, 0, j)),
        ],
        out_specs=pl.BlockSpec((1, 8, tn), lambda i, j: (i, 0, j)),
        out_shape=jax.ShapeDtypeStruct((depth, 8, six_d), F32),
        compiler_params=_cparams(("arbitrary", "arbitrary")),
        name="ada",
    )(cc, w_ada, b_ada.reshape(depth, 1, six_d))


def _route(x1, n2g, sh2, sc2, wr_hi, wr_lo):
    h2 = _norm_mod(x1, n2g, sh2, sc2)
    h_hi = h2.astype(BF16)
    h_lo = (h2 - h_hi.astype(F32)).astype(BF16)
    lg = (lax.dot_general(wr_hi, h_hi, NT_DIMS, preferred_element_type=F32)
          + lax.dot_general(wr_hi, h_lo, NT_DIMS, preferred_element_type=F32)
          + lax.dot_general(wr_lo, h_hi, NT_DIMS, preferred_element_type=F32))
    m = jnp.max(lg, axis=0, keepdims=True)
    ex = jnp.exp(lg - m)
    aff = ex / jnp.sum(ex, axis=0, keepdims=True)
    return h_hi, aff


def _pool_kernel(x_ref, xp_ref, xn_ref, n1g_ref, sh1_ref, sc1_ref, g1_ref, pw_ref, ps_ref,
                 n2g_ref, sh2_ref, sc2_ref, wrh_ref, wrl_ref,
                 x1_ref, h2_ref, aff_ref, hs_ref, p1_ref, p2_ref, p3_ref, *, n_tokens):
    t = pl.program_id(1)
    nt = pl.num_programs(1)
    tt = x_ref.shape[1]
    gd = POOL_GROUP_DIM
    n1g, sh1, sc1 = n1g_ref[...], sh1_ref[0], sc1_ref[0]
    x = x_ref[0]
    h = _norm_mod(x, n1g, sh1, sc1)
    hp = _norm_mod(xp_ref[0], n1g, sh1, sc1) * (t > 0).astype(F32)
    hn = _norm_mod(xn_ref[0], n1g, sh1, sc1) * (t < nt - 1).astype(F32)
    ext = tt + 2 * HALO
    hs_ref[0:HALO, :] = hp
    hs_ref[HALO:HALO + tt, :] = h
    hs_ref[HALO + tt:ext, :] = hn
    for ref in (hs_ref, p1_ref, p2_ref):
        ref[ext:, :] = jnp.zeros((ref.shape[0] - ext, ref.shape[1]), F32)
    p1_ref[0:ext, :] = hs_ref[0:ext, gd:] + hs_ref[1:ext + 1, gd:]
    p2_ref[0:ext, :] = p1_ref[0:ext, gd:] + p1_ref[2:ext + 2, gd:]
    p3_ref[...] = p2_ref[0:ext, gd:] + p2_ref[4:ext + 4, gd:]
    for rs in _row_pieces(tt):
        r0, nr = rs.start, rs.stop - rs.start

        def rows(ref, shift, cols=slice(0, gd)):
            return ref[r0 + shift:r0 + shift + nr, cols]

        sums = [
            rows(hs_ref, HALO - 1) + rows(hs_ref, HALO),
            rows(p1_ref, HALO - 2) + rows(p1_ref, HALO),
            rows(p2_ref, HALO - 4) + rows(p2_ref, HALO),
            rows(p3_ref, 0) + rows(p3_ref, HALO),
        ]
        pos = t * tt + r0 + lax.broadcasted_iota(I32, (nr, 1), 0)
        ys = []
        for g, w in enumerate(POOL_WINDOWS):
            half = w // 2
            c0 = g * POOL_GROUP_DIM
            lo = jnp.clip(pos - half, 0, n_tokens)
            hi = jnp.clip(pos + half, 0, n_tokens)
            p = sums[g] / (hi - lo).astype(F32) - rows(hs_ref, HALO, slice(c0, c0 + gd))
            ys.append(jnp.dot(p.astype(BF16), pw_ref[g], preferred_element_type=F32))
        y = jnp.concatenate(ys, axis=1) * ps_ref[...]
        x1 = x_ref[0, rs, :] + g1_ref[0] * y
        x1_ref[0, rs, :] = x1
        h2, aff = _route(x1, n2g_ref[...], sh2_ref[0], sc2_ref[0], wrh_ref[...], wrl_ref[...])
        h2_ref[0, rs, :] = h2
        aff_ref[0, :, rs] = aff


def _row_spec(d):
    return pl.BlockSpec((1, d), lambda b, t: (0, 0))


def _mod_spec(d):
    return pl.BlockSpec((1, 1, d), lambda b, t: (b, 0, 0))


def _pool_layer(x, n1g, sh1, sc1, g1, pw, ps, n2g, sh2, sc2, wrh, wrl, tt):
    bsz, n, d = x.shape
    assert POOL_WINDOWS == (2, 4, 8, 16) and d == 4 * POOL_GROUP_DIM
    nb8 = n // HALO
    r = tt // HALO
    kern = functools.partial(_pool_kernel, n_tokens=n)
    return pl.pallas_call(
        kern,
        grid=(bsz, n // tt),
        in_specs=[
            pl.BlockSpec((1, tt, d), lambda b, t: (b, t, 0)),
            pl.BlockSpec((1, HALO, d), lambda b, t: (b, jnp.maximum(t * r - 1, 0), 0)),
            pl.BlockSpec((1, HALO, d), lambda b, t: (b, jnp.minimum((t + 1) * r, nb8 - 1), 0)),
            _row_spec(d), _mod_spec(d), _mod_spec(d), _mod_spec(d),
            pl.BlockSpec((4, POOL_GROUP_DIM, POOL_GROUP_DIM), lambda b, t: (0, 0, 0)),
            _row_spec(d),
            _row_spec(d), _mod_spec(d), _mod_spec(d),
            pl.BlockSpec((N_EXPERTS, d), lambda b, t: (0, 0)),
            pl.BlockSpec((N_EXPERTS, d), lambda b, t: (0, 0)),
        ],
        out_specs=[
            pl.BlockSpec((1, tt, d), lambda b, t: (b, t, 0)),
            pl.BlockSpec((1, tt, d), lambda b, t: (b, t, 0)),
            pl.BlockSpec((1, N_EXPERTS, tt), lambda b, t: (b, 0, t)),
        ],
        out_shape=[
            jax.ShapeDtypeStruct((bsz, n, d), F32),
            jax.ShapeDtypeStruct((bsz, n, d), BF16),
            jax.ShapeDtypeStruct((bsz, N_EXPERTS, n), F32),
        ],
        scratch_shapes=[pltpu.VMEM((tt + 3 * HALO, d), F32),
                        pltpu.VMEM((tt + 3 * HALO, d - POOL_GROUP_DIM), F32),
                        pltpu.VMEM((tt + 3 * HALO, d - 2 * POOL_GROUP_DIM), F32),
                        pltpu.VMEM((tt + 2 * HALO, d - 3 * POOL_GROUP_DIM), F32)],
        compiler_params=_cparams(("arbitrary", "arbitrary")),
        name="pool_route",
    )(x, x, x, n1g, sh1, sc1, g1, pw, ps, n2g, sh2, sc2, wrh, wrl)


def _head_rms(v, bavg, gain):
    sq = v * v
    sq_hi = sq.astype(BF16)
    sq_lo = (sq - sq_hi.astype(F32)).astype(BF16)
    ms = (jnp.dot(sq_hi, bavg, preferred_element_type=F32)
          + jnp.dot(sq_lo, bavg, preferred_element_type=F32))
    return v * lax.rsqrt(ms + NORM_EPS) * gain


def _rope(v, cos, sin_signed):
    lane = lax.broadcasted_iota(I32, v.shape, 1)
    fwd = pltpu.roll(v, 32, 1)
    bwd = pltpu.roll(v, 96, 1)
    partner = jnp.where((lane % HEAD_DIM) < HEAD_DIM // 2, bwd, fwd)
    return v * cos + partner * sin_signed


def _qkv_kernel(x_ref, n1g_ref, sh1_ref, sc1_ref, w_ref, bavg_ref, qg_ref, kg_ref,
                cos_ref, sin_ref, q_ref, k_ref, v_ref):
    for rs in _row_pieces(x_ref.shape[1]):
        _qkv_rows(rs, x_ref, n1g_ref, sh1_ref, sc1_ref, w_ref, bavg_ref, qg_ref, kg_ref,
                  cos_ref, sin_ref, q_ref, k_ref, v_ref)


def _qkv_rows(rs, x_ref, n1g_ref, sh1_ref, sc1_ref, w_ref, bavg_ref, qg_ref, kg_ref,
              cos_ref, sin_ref, q_ref, k_ref, v_ref):
    h = _norm_mod(x_ref[0, rs, :], n1g_ref[...], sh1_ref[0], sc1_ref[0]).astype(BF16)
    qkv = jnp.dot(h, w_ref[...], preferred_element_type=F32)
    nq = N_Q_HEADS * HEAD_DIM
    nk = N_KV_HEADS * HEAD_DIM
    bavg = bavg_ref[...]
    cos, sin = cos_ref[rs, :], sin_ref[rs, :]
    scale = HEAD_DIM ** -0.5
    for c in range(nq // HEAD_GROUP):
        qn = _head_rms(qkv[:, c * HEAD_GROUP:(c + 1) * HEAD_GROUP], bavg, qg_ref[...])
        for s in range(2):
            qr = _rope(qn[:, s * LANES:(s + 1) * LANES], cos, sin) * scale
            for hh in range(2):
                head = c * 4 + s * 2 + hh
                q_ref[0, head, rs, :] = qr[:, hh * HEAD_DIM:(hh + 1) * HEAD_DIM].astype(BF16)
    kn = _head_rms(qkv[:, nq:nq + nk], bavg, kg_ref[...])
    v = qkv[:, nq + nk:]
    for s in range(2):
        kr = _rope(kn[:, s * LANES:(s + 1) * LANES], cos, sin)
        for hh in range(2):
            head = s * 2 + hh
            k_ref[0, head, rs, :] = kr[:, hh * HEAD_DIM:(hh + 1) * HEAD_DIM].astype(BF16)
            v_ref[0, head, rs, :] = v[:, head * HEAD_DIM:(head + 1) * HEAD_DIM].astype(BF16)


def _qkv_layer(x, n1g, sh1, sc1, w, bavg, qg, kg, cos, sin, tt):
    bsz, n, d = x.shape
    return pl.pallas_call(
        _qkv_kernel,
        grid=(bsz, n // tt),
        in_specs=[
            pl.BlockSpec((1, tt, d), lambda b, t: (b, t, 0)),
            _row_spec(d), _mod_spec(d), _mod_spec(d),
            pl.BlockSpec(w.shape, lambda b, t: (0, 0)),
            pl.BlockSpec((HEAD_GROUP, HEAD_GROUP), lambda b, t: (0, 0)),
            _row_spec(HEAD_GROUP), _row_spec(HEAD_GROUP),
            pl.BlockSpec((tt, LANES), lambda b, t: (t, 0)),
            pl.BlockSpec((tt, LANES), lambda b, t: (t, 0)),
        ],
        out_specs=[
            pl.BlockSpec((1, N_Q_HEADS, tt, HEAD_DIM), lambda b, t: (b, 0, t, 0)),
            pl.BlockSpec((1, N_KV_HEADS, tt, HEAD_DIM), lambda b, t: (b, 0, t, 0)),
            pl.BlockSpec((1, N_KV_HEADS, tt, HEAD_DIM), lambda b, t: (b, 0, t, 0)),
        ],
        out_shape=[
            jax.ShapeDtypeStruct((bsz, N_Q_HEADS, n, HEAD_DIM), BF16),
            jax.ShapeDtypeStruct((bsz, N_KV_HEADS, n, HEAD_DIM), BF16),
            jax.ShapeDtypeStruct((bsz, N_KV_HEADS, n, HEAD_DIM), BF16),
        ],
        compiler_params=_cparams(("arbitrary", "arbitrary")),
        name="qkv",
    )(x, n1g, sh1, sc1, w, bavg, qg, kg, cos, sin)


def _attn_kernel(sink_ref, q_ref, *refs, band, bounded):
    if band:
        kp_ref, kc_ref, kn_ref, vp_ref, vc_ref, vn_ref, kx_ref, vx_ref, bias_ref, o_ref = refs[:10]
    else:
        kx_ref, vx_ref, o_ref = refs[:3]
    if not bounded:
        s_all, p_all = refs[-2:]
    tq = q_ref.shape[2]
    n_chunks = Q_PER_KV * tq // SOFTMAX_ROWS
    for g in range(N_KV_HEADS):
        if band:
            kg = jnp.concatenate([kp_ref[0, g], kc_ref[0, g], kn_ref[0, g], kx_ref[0, g]], axis=0)
            vg = jnp.concatenate([vp_ref[0, g], vc_ref[0, g], vn_ref[0, g], vx_ref[0, g]], axis=0)
        else:
            kg = kx_ref[0, g]
            vg = vx_ref[0, g]
        vext = jnp.concatenate([vg, jnp.ones_like(vg)], axis=1)
        qg = q_ref[0, g * Q_PER_KV:(g + 1) * Q_PER_KV].reshape(Q_PER_KV * tq, HEAD_DIM)
        if bounded:
            shift = sink_ref[N_Q_HEADS]
            s = lax.dot_general(qg, kg, NT_DIMS, preferred_element_type=F32)
            if band:
                s = s + bias_ref[0]
            p = jnp.exp(s - shift).astype(BF16)
            sink_term = jnp.exp(jnp.concatenate(
                [jnp.full((tq, 1), sink_ref[g * Q_PER_KV + hh] - shift, F32) for hh in range(Q_PER_KV)], axis=0))
        else:
            s_ref, p_ref = s_all.at[g], p_all.at[g]
            s_ref[...] = lax.dot_general(qg, kg, NT_DIMS, preferred_element_type=F32)
            sink_terms = []
            for r in range(n_chunks):
                rs = slice(r * SOFTMAX_ROWS, (r + 1) * SOFTMAX_ROWS)
                sc = s_ref[rs, :]
                if band:
                    sc = sc + bias_ref[0, rs, :]
                sk = sink_ref[g * Q_PER_KV + (r * SOFTMAX_ROWS) // tq]
                m = jnp.maximum(jnp.max(sc, axis=-1, keepdims=True), sk)
                p_ref[rs, :] = jnp.exp(sc - m).astype(BF16)
                sink_terms.append(jnp.exp(sk - m))
            p = p_ref[...]
            sink_term = jnp.concatenate(sink_terms, axis=0)
        oe = jnp.dot(p, vext, preferred_element_type=F32)
        den = oe[:, HEAD_DIM:HEAD_DIM + 1] + sink_term
        o = (oe[:, :HEAD_DIM] / den).astype(o_ref.dtype)
        for hh in range(Q_PER_KV):
            head = g * Q_PER_KV + hh
            o_ref[0, :, head * HEAD_DIM:(head + 1) * HEAD_DIM] = o[hh * tq:(hh + 1) * tq]


def _band_bias(lc):
    nband = BLOCK_Q + 2 * WINDOW
    row = jnp.arange(Q_PER_KV * BLOCK_Q)[:, None] % BLOCK_Q
    col = jnp.arange(nband + lc)[None, :]
    dlt = col - row
    inband = (dlt >= 0) & (dlt <= 2 * WINDOW)
    variants = [inband & (col >= WINDOW), inband, inband & (col < WINDOW + BLOCK_Q)]
    return jnp.stack([jnp.where(v | (col >= nband), 0.0, NEG_INF) for v in variants]).astype(F32)


def _attn_layer(sink, score_bound, q, k, v, kx, vx, band):
    sink_ext = jnp.concatenate([sink.astype(F32), score_bound.reshape(1).astype(F32)])
    if not band:
        k, v = kx, vx
    return lax.cond(score_bound <= MAX_SHIFT,
                    functools.partial(_attn_call, band=band, bounded=True),
                    functools.partial(_attn_call, band=band, bounded=False),
                    sink_ext, q, k, v, kx, vx)


def _attn_call(sink, q, k, v, kx, vx, *, band, bounded):
    bsz, _, n, _ = q.shape
    lc = kx.shape[2]
    tq = BLOCK_Q if band else n
    nb = n // tq
    n_keys = (BLOCK_Q + 2 * WINDOW + lc) if band else lc
    qspec = pl.BlockSpec((1, N_Q_HEADS, tq, HEAD_DIM), lambda b, t: (b, 0, t, 0))
    xspec = pl.BlockSpec((1, N_KV_HEADS, lc, HEAD_DIM), lambda b, t: (b, 0, 0, 0))
    if band:
        prev = pl.BlockSpec((1, N_KV_HEADS, tq, HEAD_DIM), lambda b, t: (b, 0, jnp.maximum(t - 1, 0), 0))
        cur = pl.BlockSpec((1, N_KV_HEADS, tq, HEAD_DIM), lambda b, t: (b, 0, t, 0))
        nxt = pl.BlockSpec((1, N_KV_HEADS, tq, HEAD_DIM), lambda b, t: (b, 0, jnp.minimum(t + 1, nb - 1), 0))
        bias = _band_bias(lc)
        bspec = pl.BlockSpec((1,) + bias.shape[1:],
                             lambda b, t: (jnp.where(t == 0, 0, jnp.where(t == nb - 1, 2, 1)), 0, 0))
        in_specs = [qspec, prev, cur, nxt, prev, cur, nxt, xspec, xspec, bspec]
        args = (q, k, k, k, v, v, v, kx, vx, bias)
    else:
        in_specs = [qspec, xspec, xspec]
        args = (q, kx, vx)
    scratch = [] if bounded else [pltpu.VMEM((N_KV_HEADS, Q_PER_KV * tq, n_keys), F32),
                                  pltpu.VMEM((N_KV_HEADS, Q_PER_KV * tq, n_keys), BF16)]
    return pl.pallas_call(
        functools.partial(_attn_kernel, band=band, bounded=bounded),
        grid=(bsz, nb),
        in_specs=[pl.BlockSpec(memory_space=pltpu.SMEM)] + in_specs,
        out_specs=pl.BlockSpec((1, tq, N_Q_HEADS * HEAD_DIM), lambda b, t: (b, t, 0)),
        out_shape=jax.ShapeDtypeStruct((bsz, n, N_Q_HEADS * HEAD_DIM), BF16),
        scratch_shapes=scratch,
        compiler_params=_cparams(("arbitrary", "arbitrary")),
        name=("attn_band" if band else "attn_ctx") + ("_bounded" if bounded else ""),
    )(sink, *args)


def _oproj_kernel(o_ref, x_ref, wo_ref, g1_ref, n2g_ref, sh2_ref, sc2_ref, wrh_ref, wrl_ref,
                  x1_ref, h2_ref, aff_ref):
    y = jnp.dot(o_ref[0], wo_ref[...], preferred_element_type=F32)
    x1 = x_ref[0] + g1_ref[0] * y
    x1_ref[0] = x1
    h2, aff = _route(x1, n2g_ref[...], sh2_ref[0], sc2_ref[0], wrh_ref[...], wrl_ref[...])
    h2_ref[0] = h2
    aff_ref[0] = aff


def _oproj_layer(o, x, wo, g1, n2g, sh2, sc2, wrh, wrl, tt):
    bsz, n, d = x.shape
    return pl.pallas_call(
        _oproj_kernel,
        grid=(bsz, n // tt),
        in_specs=[
            pl.BlockSpec((1, tt, d), lambda b, t: (b, t, 0)),
            pl.BlockSpec((1, tt, d), lambda b, t: (b, t, 0)),
            pl.BlockSpec((d, d), lambda b, t: (0, 0)),
            _mod_spec(d), _row_spec(d), _mod_spec(d), _mod_spec(d),
            pl.BlockSpec((N_EXPERTS, d), lambda b, t: (0, 0)),
            pl.BlockSpec((N_EXPERTS, d), lambda b, t: (0, 0)),
        ],
        out_specs=[
            pl.BlockSpec((1, tt, d), lambda b, t: (b, t, 0)),
            pl.BlockSpec((1, tt, d), lambda b, t: (b, t, 0)),
            pl.BlockSpec((1, N_EXPERTS, tt), lambda b, t: (b, 0, t)),
        ],
        out_shape=[
            jax.ShapeDtypeStruct((bsz, n, d), F32),
            jax.ShapeDtypeStruct((bsz, n, d), BF16),
            jax.ShapeDtypeStruct((bsz, N_EXPERTS, n), F32),
        ],
        compiler_params=_cparams(("arbitrary", "arbitrary")),
        name="oproj_route",
    )(o, x, wo, g1, n2g, sh2, sc2, wrh, wrl)


def _prefix_counts(mask, n_chunks):
    ne = mask.shape[0]
    m2 = mask.reshape(ne * n_chunks, LANES).astype(BF16)
    ri = lax.broadcasted_iota(I32, (LANES, LANES), 0)
    ci = lax.broadcasted_iota(I32, (LANES, LANES), 1)
    upper = (ri <= ci).astype(BF16)
    within = jnp.dot(m2, upper, preferred_element_type=F32).reshape(ne, n_chunks, LANES)
    tot = jnp.dot(m2, jnp.ones((LANES, LANES), BF16), preferred_element_type=F32).reshape(ne, n_chunks, LANES)
    rc = lax.broadcasted_iota(I32, (n_chunks, n_chunks), 0)
    cc = lax.broadcasted_iota(I32, (n_chunks, n_chunks), 1)
    strict_lower = (cc < rc).astype(F32)
    off = jnp.stack([jnp.dot(strict_lower, tot[e], preferred_element_type=F32) for e in range(ne)])
    return within + off, off, tot


def _select_kernel(a_ref, slot_ref, offc_ref, flag_ref, *, cap, tile):
    a = a_ref[0]
    ne, n_chunks, _ = a.shape
    def count(pred):
        c = jnp.sum(pred.astype(F32), axis=1, keepdims=True)
        return jnp.sum(c, axis=2, keepdims=True)

    def body(i, thr):
        cand = thr | lax.shift_left(jnp.int32(1), 30 - i)
        return jnp.where(count(a >= lax.bitcast_convert_type(cand, F32)) >= cap, cand, thr)

    thr = lax.bitcast_convert_type(lax.fori_loop(0, 31, body, jnp.zeros((ne, 1, 1), I32)), F32)
    gt = a > thr
    eq = a == thr
    need = cap - count(gt)
    eq_rank, _, _ = _prefix_counts(eq.astype(F32), n_chunks)
    sel = gt | (eq & (eq_rank <= need))
    sel_f = sel.astype(F32)
    incl, off, tot = _prefix_counts(sel_f, n_chunks)
    slot_ref[0] = jnp.where(sel, (incl - 1.0).astype(I32), -1)
    offc_ref[0] = off.astype(I32)

    cpt = tile // LANES
    ends = off + tot
    if cpt > 1:
        ends = pltpu.roll(ends, n_chunks - (cpt - 1), 1)
    aligned = jnp.floor(off * (1.0 / BF16_ROWS)) * BF16_ROWS

    def fits(win):
        return (ends <= jnp.minimum(aligned, float(cap - win)) + win).astype(F32)

    win_c = _combine_window(cap, tile)
    fit_g = fits(min(GATHER_SMALL, _window(cap, tile)))
    rows = [jnp.min(fits(min(COMBINE_SMALL, win_c)), axis=0), 1.0 - jnp.min(fits(win_c), axis=0)]
    rows += [jnp.min(fit_g[g:g + GATHER_EXPERTS], axis=0) for g in range(0, ne, GATHER_EXPERTS)]
    flag_ref[0] = jnp.stack(rows).astype(I32)


def _select(aff4, cap, tile):
    bsz, ne, n_chunks, cw = aff4.shape
    n_flags = 2 + ne // GATHER_EXPERTS
    spec = pl.BlockSpec((1, ne, n_chunks, cw), lambda b: (b, 0, 0, 0))
    return pl.pallas_call(
        functools.partial(_select_kernel, cap=cap, tile=tile),
        grid=(bsz,),
        in_specs=[spec],
        out_specs=[spec, spec, pl.BlockSpec((1, n_flags, n_chunks, cw), lambda b: (b, 0, 0, 0))],
        out_shape=[jax.ShapeDtypeStruct(aff4.shape, I32), jax.ShapeDtypeStruct(aff4.shape, I32),
                   jax.ShapeDtypeStruct((bsz, n_flags, n_chunks, cw), I32)],
        compiler_params=_cparams(("arbitrary",)),
        name="select",
    )(aff4)


def _window(cap, tile):
    return min(tile + BF16_ROWS, cap)


def _window_start(off, cap, win):
    start = lax.shift_left(lax.shift_right_logical(off, BF16_ROWS_LOG2), BF16_ROWS_LOG2)
    return pl.multiple_of(jnp.minimum(start, cap - win), BF16_ROWS)


def _gather_kernel(offs_ref, small_ref, h2_ref, sl_ref, af_ref, xs_ref, val_ref, *, cap, tile, n_tiles):
    b, eg, s = pl.program_id(0), pl.program_id(1), pl.program_id(2)
    ge = sl_ref.shape[1]
    sub = h2_ref.shape[1] // tile
    full = _window(cap, tile)
    small = min(GATHER_SMALL, full)

    @pl.when(s == 0)
    def _():
        xs_ref[...] = jnp.zeros_like(xs_ref)
        val_ref[...] = jnp.zeros_like(val_ref)

    def tile_rows(u, win):
        ts = slice(u * tile, (u + 1) * tile)
        starts, hits = [], []
        for j in range(ge):
            off = offs_ref[(b * N_EXPERTS + eg * ge + j) * n_tiles + s * sub + u]
            start = _window_start(off, cap, win)
            rows = start + lax.broadcasted_iota(I32, (win, tile), 0)
            starts.append(start)
            hits.append(rows == sl_ref[0, j, :, ts])
        stacked = jnp.concatenate([h.astype(BF16) for h in hits], axis=0)
        res = jnp.dot(stacked, h2_ref[0, ts, :], preferred_element_type=F32)
        for j in range(ge):
            dst = pl.ds(starts[j], win)
            xs_ref[0, j, dst, :] = (xs_ref[0, j, dst, :].astype(F32) + res[j * win:(j + 1) * win]).astype(BF16)
            val_ref[0, j, dst, :] += jnp.sum(jnp.where(hits[j], af_ref[0, j, :, ts], 0.0), axis=1, keepdims=True)

    if small == full:
        for u in range(sub):
            tile_rows(u, full)
        return
    flags = [small_ref[(b * pl.num_programs(1) + eg) * n_tiles + s * sub + u] for u in range(sub)]
    all_fit = functools.reduce(jnp.minimum, flags) > 0

    @pl.when(all_fit)
    def _():
        for u in range(sub):
            tile_rows(u, small)

    @pl.when(jnp.logical_not(all_fit))
    def _():
        for u in range(sub):
            pl.when(flags[u] > 0)(functools.partial(tile_rows, u, small))
            pl.when(flags[u] <= 0)(functools.partial(tile_rows, u, full))


def _gather(offs, small, h2, slot, aff, cap, tile):
    bsz, n, d = h2.shape
    tg = min(n, GATHER_ROWS)
    n_tiles = n // tile
    ge = GATHER_EXPERTS
    kern = functools.partial(_gather_kernel, cap=cap, tile=tile, n_tiles=n_tiles)
    return pl.pallas_call(
        kern,
        grid_spec=pltpu.PrefetchScalarGridSpec(
            num_scalar_prefetch=2,
            grid=(bsz, N_EXPERTS // ge, n // tg),
            in_specs=[
                pl.BlockSpec((1, tg, d), lambda b, e, s, offs, small: (b, s, 0)),
                pl.BlockSpec((1, ge, 1, tg), lambda b, e, s, offs, small: (b, e, 0, s)),
                pl.BlockSpec((1, ge, 1, tg), lambda b, e, s, offs, small: (b, e, 0, s)),
            ],
            out_specs=[
                pl.BlockSpec((1, ge, cap, d), lambda b, e, s, offs, small: (b, e, 0, 0),
                             pipeline_mode=pl.Buffered(1)),
                pl.BlockSpec((1, ge, cap, 1), lambda b, e, s, offs, small: (b, e, 0, 0),
                             pipeline_mode=pl.Buffered(1)),
            ],
        ),
        out_shape=[
            jax.ShapeDtypeStruct((bsz, N_EXPERTS, cap, d), BF16),
            jax.ShapeDtypeStruct((bsz, N_EXPERTS, cap, 1), F32),
        ],
        compiler_params=_cparams(("arbitrary", "arbitrary", "arbitrary")),
        name="gather",
    )(offs, small, h2, slot.reshape(bsz, N_EXPERTS, 1, n), aff.reshape(bsz, N_EXPERTS, 1, n))


def _up_kernel(*refs, has_ctx):
    if has_ctx:
        xl_ref, xc_ref, wg_ref, wu_ref, h_ref, xs_ref = refs
        cl = xl_ref.shape[2]

        @pl.when(pl.program_id(2) == 0)
        def _():
            xs_ref[0:cl, :] = xl_ref[0, 0]
            xs_ref[cl:, :] = xc_ref[0, 0]

        xs_of = lambda rs: xs_ref[rs, :]
    else:
        xl_ref, wg_ref, wu_ref, h_ref = refs
        xs_of = lambda rs: xl_ref[0, 0, rs, :]
    wg = wg_ref[0, 0].astype(BF16)
    wu = wu_ref[0, 0].astype(BF16)
    rows = h_ref.shape[2]
    for r in range(UP_ROW_SPLIT):
        rs = slice(r * rows // UP_ROW_SPLIT, (r + 1) * rows // UP_ROW_SPLIT)
        xs = xs_of(rs)
        a = jnp.dot(xs, wg, preferred_element_type=F32)
        u = jnp.dot(xs, wu, preferred_element_type=F32)
        h_ref[0, 0, rs, :] = (a * jax.nn.sigmoid(a) * u).astype(BF16)


def _down_kernel(*refs, has_ctx):
    if has_ctx:
        h_ref, vl_ref, vc_ref, wd_ref, y_ref = refs
    else:
        h_ref, vl_ref, wd_ref, y_ref = refs
    cl = vl_ref.shape[2]
    wd = wd_ref[0, 0].astype(BF16)
    rows = h_ref.shape[2]
    half = rows // 2
    res = jnp.dot(h_ref[0, 0, 0:half, :], wd, preferred_element_type=F32)
    y_ref[0, 0, 0:half, :] = (res * vl_ref[0, 0, 0:half, :]).astype(BF16)
    val = vl_ref[0, 0, half:, :]
    if has_ctx:
        val = jnp.concatenate([val, vc_ref[0, 0]], axis=0)
    res = jnp.dot(h_ref[0, 0, half:, :], wd, preferred_element_type=F32)
    y_ref[0, 0, half:, :] = (res * val).astype(BF16)


def _mlp(layer, xl, vl, xc, vc, wg, wu, wd):
    bsz, ne, cl, d = xl.shape
    fdim = wg.shape[-1]
    has_ctx = xc is not None
    cc = xc.shape[2] if has_ctx else 0
    rows = cl + cc
    fc = UP_CHUNK
    dn = DOWN_CHUNK

    xspecs = [pl.BlockSpec((1, 1, cl, d), lambda e, b, f: (b, e, 0, 0))]
    if has_ctx:
        xspecs.append(pl.BlockSpec((1, 1, cc, d), lambda e, b, f: (b, e, 0, 0)))
    hid = pl.pallas_call(
        functools.partial(_up_kernel, has_ctx=has_ctx),
        grid=(ne, bsz, fdim // fc),
        in_specs=xspecs + [
            pl.BlockSpec((1, 1, d, fc), lambda e, b, f: (layer, e, 0, f)),
            pl.BlockSpec((1, 1, d, fc), lambda e, b, f: (layer, e, 0, f)),
        ],
        out_specs=pl.BlockSpec((1, 1, rows, fc), lambda e, b, f: (b, e, 0, f)),
        out_shape=jax.ShapeDtypeStruct((bsz, ne, rows, fdim), BF16),
        scratch_shapes=[pltpu.VMEM((rows, d), BF16)] if has_ctx else [],
        compiler_params=_cparams(("arbitrary", "arbitrary", "arbitrary")),
        name="experts_up",
    )(*((xl, xc) if has_ctx else (xl,)), wg, wu)

    vspecs = [pl.BlockSpec((1, 1, cl, 1), lambda e, b, j: (b, e, 0, 0))]
    if has_ctx:
        vspecs.append(pl.BlockSpec((1, 1, cc, 1), lambda e, b, j: (b, e, 0, 0)))
    return pl.pallas_call(
        functools.partial(_down_kernel, has_ctx=has_ctx),
        grid=(ne, bsz, d // dn),
        in_specs=[pl.BlockSpec((1, 1, rows, fdim), lambda e, b, j: (b, e, 0, 0))] + vspecs + [
            pl.BlockSpec((1, 1, fdim, dn), lambda e, b, j: (layer, e, 0, j)),
        ],
        out_specs=pl.BlockSpec((1, 1, rows, dn), lambda e, b, j: (b, e, 0, j)),
        out_shape=jax.ShapeDtypeStruct((bsz, ne, rows, d), BF16),
        compiler_params=_cparams(("arbitrary", "arbitrary", "arbitrary")),
        name="experts_down",
    )(hid, *((vl, vc) if has_ctx else (vl,)), wd)


def _combine_window(cap, tile):
    return min(tile, cap)


def _combine_kernel(offs_ref, ovf_ref, small_ref, x1_ref, g2_ref, sl_ref, y_hbm, o_ref,
                    ysm_ref, ybig_ref, yext_ref, sems, semb, semx, *, cap, tile, n_tiles, base, sub):
    b, tp = pl.program_id(0), pl.program_id(1)
    steps_per_sample = n_tiles // sub
    n_steps = pl.num_programs(0) * steps_per_sample
    win = _combine_window(cap, tile)
    small = min(COMBINE_SMALL, win)
    has_big = small < win
    step = b * steps_per_sample + tp
    buf = lax.rem(step, 2)

    def tile_of(stp, u):
        flat = stp * sub + u
        bb = flat // n_tiles
        return bb, flat - bb * n_tiles, flat

    def start_of(bb, tt, e, w):
        return _window_start(offs_ref[(bb * N_EXPERTS + e) * n_tiles + tt], cap, w)

    def small_copy(stp, u, bf, e):
        bb, tt, _ = tile_of(stp, u)
        src = pl.multiple_of(base + start_of(bb, tt, e, small), BF16_ROWS)
        return pltpu.make_async_copy(y_hbm.at[bb, e, pl.ds(src, small), :],
                                     ysm_ref.at[bf, u, pl.ds(e * small, small), :], sems.at[bf, u, e])

    def is_small(stp, u):
        return small_ref[tile_of(stp, u)[2]] > 0 if has_big else True

    def when(cond, fn):
        if cond is True:
            fn()
        else:
            pl.when(cond)(fn)

    def start_small(stp, bf):
        for u in range(sub):
            def go(u=u):
                for e in range(N_EXPERTS):
                    small_copy(stp, u, bf, e).start()
            when(is_small(stp, u), go)

    pl.when(step == 0)(functools.partial(start_small, step, buf))
    pl.when(step < n_steps - 1)(functools.partial(start_small, step + 1, 1 - buf))

    eye = (lax.broadcasted_iota(I32, (tile, tile), 0)
           == lax.broadcasted_iota(I32, (tile, tile), 1)).astype(BF16)

    def slots_by_token(u):
        sl = sl_ref[0, :, u * tile:(u + 1) * tile] + 1
        d_hi = lax.shift_right_logical(sl, SLOT_DIGIT_BITS).astype(F32).astype(BF16)
        d_lo = (sl & ((1 << SLOT_DIGIT_BITS) - 1)).astype(F32).astype(BF16)
        return (lax.dot_general(eye, d_hi, NT_DIMS, preferred_element_type=F32) * float(1 << SLOT_DIGIT_BITS)
                + lax.dot_general(eye, d_lo, NT_DIMS, preferred_element_type=F32)).astype(I32) - 1

    def one_hot_sum(u, w, slot_t, windows, wait):
        bb, tt, _ = tile_of(step, u)
        lane = lax.broadcasted_iota(I32, (tile, w), 1)
        hits = [(lane == slot_t[:, e:e + 1] - start_of(bb, tt, e, w)).astype(BF16) for e in range(N_EXPERTS)]
        wait()
        acc = None
        for g in range(0, N_EXPERTS, COMBINE_GROUP):
            part = jnp.dot(jnp.concatenate(hits[g:g + COMBINE_GROUP], axis=1),
                           windows[g * w:(g + COMBINE_GROUP) * w, :], preferred_element_type=F32)
            acc = part if acc is None else acc + part
        rs = slice(u * tile, (u + 1) * tile)
        o_ref[0, rs, :] = x1_ref[0, rs, :] + g2_ref[0] * acc

    def tile_small(u):
        def wait():
            for e in range(N_EXPERTS):
                small_copy(step, u, buf, e).wait()
        one_hot_sum(u, small, slots_by_token(u), ysm_ref.at[buf, u], wait)

    def tile_big(u):
        bb, tt, flat = tile_of(step, u)
        slot_t = slots_by_token(u)

        def big_copy(e):
            src = pl.multiple_of(base + start_of(bb, tt, e, win), BF16_ROWS)
            return pltpu.make_async_copy(y_hbm.at[bb, e, pl.ds(src, win), :],
                                         ybig_ref.at[pl.ds(e * win, win), :], semb.at[e])

        for e in range(N_EXPERTS):
            big_copy(e).start()

        def wait():
            for e in range(N_EXPERTS):
                big_copy(e).wait()
        one_hot_sum(u, win, slot_t, ybig_ref, wait)

        @pl.when(ovf_ref[flat] > 0)
        def _():
            def ext_start(e):
                s0 = start_of(bb, tt, e, win)
                return s0, pl.multiple_of(jnp.minimum(s0 + win, cap - BF16_ROWS), BF16_ROWS)

            def ext_copy(e):
                src = pl.multiple_of(base + ext_start(e)[1], BF16_ROWS)
                return pltpu.make_async_copy(y_hbm.at[bb, e, pl.ds(src, BF16_ROWS), :],
                                             yext_ref.at[pl.ds(e * BF16_ROWS, BF16_ROWS), :], semx.at[e])

            for e in range(N_EXPERTS):
                ext_copy(e).start()
            lane_x = lax.broadcasted_iota(I32, (tile, BF16_ROWS), 1)
            hits_x = []
            for e in range(N_EXPERTS):
                ext_copy(e).wait()
                s0, s1 = ext_start(e)
                row = s1 + lane_x
                hits_x.append(((row == slot_t[:, e:e + 1]) & (row >= s0 + win)).astype(BF16))
            extra = jnp.dot(jnp.concatenate(hits_x, axis=1), yext_ref[...], preferred_element_type=F32)
            rs = slice(u * tile, (u + 1) * tile)
            o_ref[0, rs, :] += g2_ref[0] * extra

    if not has_big:
        for u in range(sub):
            tile_small(u)
        return
    flags = [small_ref[tile_of(step, u)[2]] for u in range(sub)]
    all_small = functools.reduce(jnp.minimum, flags) > 0

    @pl.when(all_small)
    def _():
        for u in range(sub):
            tile_small(u)

    @pl.when(jnp.logical_not(all_small))
    def _():
        for u in range(sub):
            pl.when(flags[u] > 0)(functools.partial(tile_small, u))
            pl.when(flags[u] <= 0)(functools.partial(tile_big, u))


def _combine(offs, ovf, small, x1, g2, slot, y, cap, tile, base):
    bsz, n, d = x1.shape
    n_tiles = n // tile
    sub = COMBINE_TILES_PER_STEP if n_tiles % COMBINE_TILES_PER_STEP == 0 else 1
    win = _combine_window(cap, tile)
    win_small = min(COMBINE_SMALL, win)
    rows = sub * tile
    kern = functools.partial(_combine_kernel, cap=cap, tile=tile, n_tiles=n_tiles, base=base, sub=sub)
    return pl.pallas_call(
        kern,
        grid_spec=pltpu.PrefetchScalarGridSpec(
            num_scalar_prefetch=3,
            grid=(bsz, n_tiles // sub),
            in_specs=[
                pl.BlockSpec((1, rows, d), lambda b, t, *_: (b, t, 0)),
                pl.BlockSpec((1, 1, d), lambda b, t, *_: (b, 0, 0)),
                pl.BlockSpec((1, N_EXPERTS, rows), lambda b, t, *_: (b, 0, t)),
                pl.BlockSpec(memory_space=pl.ANY),
            ],
            out_specs=pl.BlockSpec((1, rows, d), lambda b, t, *_: (b, t, 0)),
            scratch_shapes=[pltpu.VMEM((2, sub, N_EXPERTS * win_small, d), BF16),
                            pltpu.VMEM((N_EXPERTS * win, d), BF16),
                            pltpu.VMEM((N_EXPERTS * BF16_ROWS, d), BF16),
                            pltpu.SemaphoreType.DMA((2, sub, N_EXPERTS)),
                            pltpu.SemaphoreType.DMA((N_EXPERTS,)),
                            pltpu.SemaphoreType.DMA((N_EXPERTS,))],
        ),
        out_shape=jax.ShapeDtypeStruct((bsz, n, d), F32),
        compiler_params=_cparams(("arbitrary", "arbitrary")),
        name="combine",
    )(offs, ovf, small, x1, g2, slot, y)


def _deinterleave(w, n_heads):
    lead = w.shape[:-1]
    w = w.reshape(lead + (n_heads, HEAD_DIM // 2, 2))
    return jnp.swapaxes(w, -1, -2).reshape(lead + (n_heads * HEAD_DIM,))


def _rope_tables(n):
    rows = n // GRID_W
    row = jnp.repeat(jnp.arange(rows), GRID_W).astype(F32)
    col = jnp.tile(jnp.arange(GRID_W), rows).astype(F32)
    n_freq = HEAD_DIM // 4
    inv = ROPE_THETA ** (-jnp.arange(n_freq, dtype=F32) / n_freq)
    ang = jnp.concatenate([row[:, None] * inv, col[:, None] * inv], axis=-1)
    cos = jnp.tile(jnp.cos(ang), (1, 4))
    sin = jnp.sin(ang)
    sin_signed = jnp.tile(jnp.concatenate([-sin, sin], axis=-1), (1, 2))
    return cos, sin_signed


def _split_bf16(w):
    hi = w.astype(BF16)
    return hi, (w - hi.astype(F32)).astype(BF16)


def _route_select(aff, cap, gather_tile, combine_tile):
    bsz, ne, n = aff.shape
    n_pad = max(n, HALO * LANES)
    a = aff if n_pad == n else jnp.pad(aff, ((0, 0), (0, 0), (0, n_pad - n)))
    assert gather_tile == combine_tile
    tile = gather_tile
    slot4, offc4, flags = _select(a.reshape(bsz, ne, n_pad // LANES, LANES), cap, tile)

    def per_tile(table):
        return table[..., ::tile // LANES, 0][..., :n // tile].reshape(-1)

    offs = per_tile(offc4)
    gather = {"offs": offs, "small": per_tile(flags[:, 2:])}
    combine = {"offs": offs, "small": per_tile(flags[:, 0]), "overflow": per_tile(flags[:, 1])}
    return slot4.reshape(bsz, ne, n_pad)[:, :, :n], gather, combine


def kernel(x, c, ctx, c_ctx, w_ada, b_ada, norm1_g, norm2_g, pool_w, pool_scale, attn_w_qkv, attn_w_o,
           attn_q_norm, attn_k_norm, attn_sink, router_w, exp_w_gate, exp_w_up, exp_w_down):
    bsz, n, d = x.shape
    lc = ctx.shape[1]
    cap = 2 * n // N_EXPERTS
    cap_c = 2 * lc // N_EXPERTS
    tile_c = min(COMBINE_TILE, lc)

    cc = jnp.zeros((8, d), F32).at[:bsz].set(c).at[bsz].set(c_ctx)
    mods = _ada(cc, w_ada, b_ada)

    cos, sin = _rope_tables(n)
    cos_c = jnp.ones((lc, LANES), F32)
    sin_c = jnp.zeros((lc, LANES), F32)
    blk = jnp.arange(HEAD_GROUP) // HEAD_DIM
    bavg = ((blk[:, None] == blk[None, :]).astype(F32) / HEAD_DIM).astype(BF16)

    def layer_mods(i):
        m = [mods[i, :, k * d:(k + 1) * d] for k in range(6)]
        lat = [v[:bsz, None, :] for v in m]
        con = [jnp.broadcast_to(v[bsz][None, None, :], (bsz, 1, d)) for v in m]
        return lat, con

    def qkv_params(i):
        j = i // 2
        nq = N_Q_HEADS * HEAD_DIM
        nk = N_KV_HEADS * HEAD_DIM
        wqkv = attn_w_qkv[j]
        wperm = jnp.concatenate([_deinterleave(wqkv[:, :nq], N_Q_HEADS),
                                 _deinterleave(wqkv[:, nq:nq + nk], N_KV_HEADS),
                                 wqkv[:, nq + nk:]], axis=1).astype(BF16)
        qg = jnp.tile(_deinterleave(attn_q_norm[j], 1), 4)[None, :]
        kg = jnp.tile(_deinterleave(attn_k_norm[j], 1), 4)[None, :]
        return wperm, bavg, qg, kg

    for i in range(DEPTH):
        update_ctx = i < DEPTH - 1
        j = i // 2
        is_pool = (i % 2) == 0
        (sh1, sc1, g1, sh2, sc2, g2), (csh1, csc1, cg1, csh2, csc2, cg2) = layer_mods(i)
        n1g = norm1_g[i][None, :]
        n2g = norm2_g[i][None, :]
        wrh, wrl = _split_bf16(router_w[i].T)

        if is_pool:
            pw = pool_w[j].astype(BF16)
            ps = pool_scale[j][None, :]
            x1, h2, aff = _pool_layer(x, n1g, sh1, sc1, g1, pw, ps, n2g, sh2, sc2, wrh, wrl, ROW_TILE)
            if update_ctx:
                c1, hc2, affc = _pool_layer(ctx, n1g, csh1, csc1, cg1, pw, ps, n2g, csh2, csc2, wrh, wrl, lc)
        else:
            wperm, _, qg, kg = qkv_params(i)
            wo = attn_w_o[j].astype(BF16)
            sink = attn_sink[j]
            bound = 1.01 * 8.0 * jnp.max(jnp.abs(attn_q_norm[j])) * jnp.max(jnp.abs(attn_k_norm[j]))
            q, k, v = _qkv_layer(x, n1g, sh1, sc1, wperm, bavg, qg, kg, cos, sin, ROW_TILE)
            qc, kc, vc = _qkv_layer(ctx, n1g, csh1, csc1, wperm, bavg, qg, kg, cos_c, sin_c, lc)
            o = _attn_layer(sink, bound, q, k, v, kc, vc, True)
            x1, h2, aff = _oproj_layer(o, x, wo, g1, n2g, sh2, sc2, wrh, wrl, ROW_TILE)
            if update_ctx:
                oc = _attn_layer(sink, bound, qc, None, None, kc, vc, False)
                c1, hc2, affc = _oproj_layer(oc, ctx, wo, cg1, n2g, csh2, csc2, wrh, wrl, lc)

        slot, tg, tc = _route_select(aff, cap, GATHER_TILE, COMBINE_TILE)
        xs, vals = _gather(tg["offs"], tg["small"], h2, slot, aff, cap, GATHER_TILE)
        if update_ctx:
            slot_c, cg, cc_ = _route_select(affc, cap_c, tile_c, tile_c)
            xs_c, vals_c = _gather(cg["offs"], cg["small"], hc2, slot_c, affc, cap_c, tile_c)
            y = _mlp(i, xs, vals, xs_c, vals_c, exp_w_gate, exp_w_up, exp_w_down)
            ctx = _combine(cc_["offs"], cc_["overflow"], cc_["small"], c1, cg2, slot_c, y, cap_c, tile_c, cap)
        else:
            y = _mlp(i, xs, vals, None, None, exp_w_gate, exp_w_up, exp_w_down)
        x = _combine(tc["offs"], tc["overflow"], tc["small"], x1, g2, slot, y, cap, COMBINE_TILE, 0)
    return x
```
